```python
import math
import jax, jax.numpy as jnp
from jax import lax
import numpy as np

D_MODEL = 1024
BATCH = 2
SEQ = 8192
DEPTH = 4
DEC_BATCH = 32
DEC_SEQ = 1
PAST_LEN = 8192
PAGE_SIZE = 128

EPS = 1e-6
F32 = jnp.float32
HEAD_DIM = 64
A_HPG = 4
A_GROUPS = ((128, 1), (512, 4), (2048, 16))
A_HEADS = A_HPG * len(A_GROUPS)
A_WIDTH = A_HEADS * HEAD_DIM
A_OUT = A_HPG * HEAD_DIM
BLK = 128
ROPE_THETA = 10000.0
LRU_WIDTH = 768
LRU_BLOCKS = 12
LRU_BLOCK = LRU_WIDTH // LRU_BLOCKS
LRU_C = 8.0
CONV_W = 4
SGU_WIDTH = 768
SGU_GROUPS = 4
SGU_CHUNK = 128
SSD_INNER = 768
SSD_HEAD_DIM = 64
SSD_HEADS = SSD_INNER // SSD_HEAD_DIM
SSD_GROUPS = 2
SSD_STATE = 128
SSD_CHUNK = 128
SSD_CONV_CH = SSD_INNER + 2 * SSD_GROUPS * SSD_STATE
N_BRANCH = 4
N_EXPERT_GROUPS = 4
EXPERTS_PER_GROUP = 4
N_EXPERTS = N_EXPERT_GROUPS * EXPERTS_PER_GROUP
MOE_TOPK = 2
D_EXPERT = 512
IN_SIZES = (A_WIDTH, A_WIDTH, A_WIDTH, LRU_WIDTH, 2 * SGU_WIDTH, SSD_INNER, SSD_CONV_CH, SSD_HEADS, N_BRANCH * D_MODEL)
SPLIT_POINTS = tuple(sum(IN_SIZES[:i + 1]) for i in range(len(IN_SIZES) - 1))
N_IN = sum(IN_SIZES)

kernel_name = "hybrid_gated_branch_decoder_step"


def rmsnorm(x, g):
    xf = x.astype(F32)
    return (xf * lax.rsqrt(jnp.mean(xf * xf, -1, keepdims=True) + EPS) * g).astype(x.dtype)


def rope(x, pos):
    half = HEAD_DIM // 2
    inv = ROPE_THETA ** (-jnp.arange(half, dtype=F32) / half)
    ang = pos.astype(F32)[:, None] * inv[None]
    cos, sin = jnp.cos(ang)[None, :, None, :], jnp.sin(ang)[None, :, None, :]
    x1, x2 = x[..., :half].astype(F32), x[..., half:].astype(F32)
    return jnp.concatenate([x1 * cos - x2 * sin, x1 * sin + x2 * cos], -1).astype(x.dtype)


def masked_softmax_lse(s, mask):
    s = jnp.where(mask, s.astype(F32), -jnp.inf)
    m = jnp.max(s, -1, keepdims=True)
    e = jnp.exp(s - m)
    den = jnp.sum(e, -1, keepdims=True)
    return e / den, (m + jnp.log(den))[..., 0]


def dilated_attn_prompt(q, k, v, dil, window):
    B, S, H, Dh = q.shape
    n_back = window // dil
    unit = dil * BLK
    Sp = -(-S // unit) * unit
    nc = Sp // unit

    def strided(t):
        t = jnp.pad(t, ((0, 0), (0, Sp - S), (0, 0), (0, 0)))
        return t.reshape(B, nc, BLK, dil, H, Dh).transpose(0, 3, 1, 2, 4, 5)

    def with_prev(t):
        prev = jnp.pad(t, ((0, 0), (0, 0), (1, 0), (0, 0), (0, 0), (0, 0)))[:, :, :-1]
        return jnp.concatenate([prev, t], axis=3)

    qs = strided(q)
    kk, vv = with_prev(strided(k)), with_prev(strided(v))
    s = jnp.einsum('brcqhd,brckhd->brchqk', qs, kk) * (Dh ** -0.5)
    qi = jnp.arange(BLK)[:, None]
    kj = jnp.arange(2 * BLK)[None, :]
    dist = qi + BLK - kj
    band = (dist >= 0) & (dist <= n_back)
    exists = (jnp.arange(nc)[:, None, None] > 0) | (kj[None] >= BLK)
    mask = (band[None] & exists)[None, None, :, None]
    p, lse = masked_softmax_lse(s, mask)
    o = jnp.einsum('brchqk,brckhd->brcqhd', p.astype(v.dtype), vv)
    o = o.transpose(0, 2, 3, 1, 4, 5).reshape(B, Sp, H, Dh)[:, :S]
    lse = lse.transpose(0, 2, 4, 1, 3).reshape(B, Sp, H)[:, :S]
    return o, lse


def dilated_attn_step(q, k_all, v_all, n_buf, dil, window):
    T, Dh = q.shape[1], q.shape[-1]
    n_back = window // dil
    idx = n_buf + jnp.arange(T)[:, None] - dil * jnp.arange(n_back + 1)[None, :]
    valid = idx >= 0
    idx = jnp.maximum(idx, 0)
    kg = jnp.take(k_all, idx, axis=1)
    vg = jnp.take(v_all, idx, axis=1)
    s = jnp.einsum('bthd,btmhd->bthm', q, kg) * (Dh ** -0.5)
    p, lse = masked_softmax_lse(s, valid[None, :, None, :])
    o = jnp.einsum('bthm,btmhd->bthd', p.astype(v_all.dtype), vg)
    return o, lse


def mixer_a(q, k, v, kv_bufs):
    B, L = q.shape[:2]
    outs, lses, kv_new = [], [], []
    for gi, (window, dil) in enumerate(A_GROUPS):
        sl = slice(gi * A_HPG, (gi + 1) * A_HPG)
        qg, kg, vg = q[:, :, sl], k[:, :, sl], v[:, :, sl]
        if kv_bufs is None:
            o, lse = dilated_attn_prompt(qg, kg, vg, dil, window)
            keep = min(window, L)
            kv_new.append(jnp.stack([kg[:, L - keep:], vg[:, L - keep:]], axis=2))
        else:
            buf = kv_bufs[gi]
            k_all = jnp.concatenate([buf[:, :, 0].astype(kg.dtype), kg], axis=1)
            v_all = jnp.concatenate([buf[:, :, 1].astype(vg.dtype), vg], axis=1)
            o, lse = dilated_attn_step(qg, k_all, v_all, buf.shape[1], dil, window)
            kv_new.append(jnp.stack([kg, vg], axis=2))
        outs.append(o.astype(F32))
        lses.append(lse)
    w = jax.nn.softmax(jnp.stack(lses), axis=0)
    o = jnp.sum(w[..., None] * jnp.stack(outs), axis=0)
    return o.reshape(B, L, A_OUT), kv_new


def causal_conv(x, buf, w, b):
    L = x.shape[1]
    xp = jnp.concatenate([buf.astype(x.dtype), x], axis=1)
    out = b
    for kk in range(CONV_W):
        out = out + xp[:, kk:kk + L] * w[kk]
    return out, xp[:, L:]


def linear_scan(a, b, h0):
    b = b.at[:, 0].add(a[:, 0] * h0)

    def comb(left, right):
        return left[0] * right[0], right[0] * left[1] + right[1]

    return lax.associative_scan(comb, (a, b), axis=1)[1]


def mixer_b(xb, buf, h0, conv_w, conv_b, wa, ba, wx, bx, lam):
    xc, new_buf = causal_conv(xb, buf, conv_w, conv_b)
    xc = xc.astype(F32)
    B, L, _ = xc.shape
    xblk = xc.reshape(B, L, LRU_BLOCKS, LRU_BLOCK)
    r = jax.nn.sigmoid(jnp.einsum('blhi,hij->blhj', xblk, wa).reshape(B, L, LRU_WIDTH) + ba)
    i = jax.nn.sigmoid(jnp.einsum('blhi,hij->blhj', xblk, wx).reshape(B, L, LRU_WIDTH) + bx)
    log_a = -LRU_C * r * jax.nn.softplus(-lam.astype(F32))
    a = jnp.exp(log_a)
    bin_ = jnp.sqrt(-jnp.expm1(2.0 * log_a)) * (i * xc)
    h = linear_scan(a, bin_, h0.astype(F32))
    return h, new_buf, h[:, -1]


def mixer_c(uv, ln_g, ln_b, ws, bs):
    uv = jax.nn.gelu(uv.astype(F32))
    u, v = jnp.split(uv, 2, axis=-1)
    mu = jnp.mean(v, -1, keepdims=True)
    var = jnp.mean(jnp.square(v - mu), -1, keepdims=True)
    v = (v - mu) * lax.rsqrt(var + EPS) * ln_g + ln_b
    B, L, _ = v.shape
    Lp = -(-L // SGU_CHUNK) * SGU_CHUNK
    vc = jnp.pad(v, ((0, 0), (0, Lp - L), (0, 0))).reshape(B, Lp // SGU_CHUNK, SGU_CHUNK, SGU_GROUPS, SGU_WIDTH // SGU_GROUPS)
    wm = ws * jnp.tril(jnp.ones((SGU_CHUNK, SGU_CHUNK), ws.dtype))
    mixed = jnp.einsum('gij,bcjgd->bcigd', wm, vc) + bs.T[None, None, :, :, None]
    mixed = mixed.reshape(B, Lp, SGU_WIDTH)[:, :L]
    return u * mixed, v


def segsum(x):
    T = x.shape[-1]
    xx = jnp.broadcast_to(x[..., :, None], x.shape + (T,))
    xx = jnp.where(jnp.tril(jnp.ones((T, T), bool), -1), xx, 0.0)
    cs = jnp.cumsum(xx, axis=-2)
    return jnp.where(jnp.tril(jnp.ones((T, T), bool)), cs, -jnp.inf)


def ssd(X, A, Bm, Cm, h0):
    b, L, H, P = X.shape
    Q = min(SSD_CHUNK, L)
    Lp = -(-L // Q) * Q

    def pad(t):
        return jnp.pad(t, ((0, 0), (0, Lp - L)) + ((0, 0),) * (t.ndim - 2))

    nc = Lp // Q
    X = pad(X).reshape(b, nc, Q, H, P)
    Bm = pad(Bm).reshape(b, nc, Q, H, SSD_STATE)
    Cm = pad(Cm).reshape(b, nc, Q, H, SSD_STATE)
    A = pad(A).reshape(b, nc, Q, H).transpose(0, 3, 1, 2)
    A_cs = jnp.cumsum(A, -1)
    Lmat = jnp.exp(segsum(A))
    G = jnp.einsum('bclhn,bcshn->bhcls', Cm, Bm)
    y_diag = jnp.einsum('bhcls,bcshp->bclhp', G * Lmat, X)
    decay_states = jnp.exp(A_cs[..., -1:] - A_cs)
    states = jnp.einsum('bclhn,bhcl,bclhp->bchpn', Bm, decay_states, X)
    states = jnp.concatenate([h0[:, None], states], axis=1)
    decay_chunk = jnp.exp(segsum(jnp.pad(A_cs[..., -1], ((0, 0), (0, 0), (1, 0)))))
    new_states = jnp.einsum('bhzc,bchpn->bzhpn', decay_chunk, states)
    states, final = new_states[:, :-1], new_states[:, -1]
    y_off = jnp.einsum('bclhn,bchpn,bhcl->bclhp', Cm, states, jnp.exp(A_cs))
    y = (y_diag + y_off).reshape(b, Lp, H, P)[:, :L]
    return y, final


def mixer_d(z, xbc, dt, buf, h0, conv_w, conv_b, dt_bias, a_log, d_skip, norm_g):
    xbc, new_buf = causal_conv(xbc, buf, conv_w, conv_b)
    xbc = jax.nn.silu(xbc.astype(F32))
    gn = SSD_GROUPS * SSD_STATE
    xs, Bm, Cm = jnp.split(xbc, [SSD_INNER, SSD_INNER + gn], axis=-1)
    b, L, _ = xs.shape
    hpg = SSD_HEADS // SSD_GROUPS
    xs = xs.reshape(b, L, SSD_HEADS, SSD_HEAD_DIM)
    Bm = jnp.repeat(Bm.reshape(b, L, SSD_GROUPS, SSD_STATE), hpg, axis=2)
    Cm = jnp.repeat(Cm.reshape(b, L, SSD_GROUPS, SSD_STATE), hpg, axis=2)
    dt = jax.nn.softplus(dt.astype(F32) + dt_bias)
    A = -jnp.exp(a_log.astype(F32))
    y, h_new = ssd(xs * dt[..., None], A * dt, Bm, Cm, h0.astype(F32))
    y = y + d_skip[:, None] * xs
    y = (y.reshape(b, L, SSD_INNER) * jax.nn.silu(z.astype(F32))).reshape(b, L, SSD_GROUPS, -1)
    y = y * lax.rsqrt(jnp.mean(y * y, -1, keepdims=True) + EPS)
    return y.reshape(b, L, SSD_INNER) * norm_g, new_buf, h_new


def hier_moe(h, wg, bg, we, be, w_gate, w_up, w_down):
    B, L, D = h.shape
    t = h.reshape(B * L, D)
    gl = jnp.einsum('nd,dg->ng', t, wg).astype(F32) + bg
    gsel = jnp.argmax(gl, -1)
    pg = jnp.take_along_axis(jax.nn.softmax(gl, -1), gsel[:, None], -1)[:, 0]
    el = jnp.einsum('nd,gde->nge', t, we).astype(F32) + be
    el = jnp.take_along_axis(el, gsel[:, None, None], 1)[:, 0]
    tv, ti = lax.top_k(el, MOE_TOPK)
    wts = jax.nn.softmax(tv, -1) * pg[:, None]
    eid = gsel[:, None] * EXPERTS_PER_GROUP + ti
    comb = jnp.sum(jax.nn.one_hot(eid, N_EXPERTS, dtype=F32) * wts[..., None], 1)
    hg = jnp.einsum('nd,edf->nef', t, w_gate)
    hu = jnp.einsum('nd,edf->nef', t, w_up)
    act = jax.nn.silu(hg) * hu * comb[..., None].astype(hg.dtype)
    return jnp.einsum('nef,efd->nd', act, w_down).reshape(B, L, D)


def layer(x, pos0, kv_bufs, conv_b_buf, h_b0, conv_d_buf, ssm_d0, p):
    B, L, _ = x.shape
    pos = pos0 + jnp.arange(L)
    h = rmsnorm(x, p['norm1_g'])
    proj = jnp.einsum('bld,dn->bln', h, p['w_in'])
    qa, ka, va, xb, uvc, zd, xbcd, dtd, gates = jnp.split(proj, SPLIT_POINTS, axis=-1)
    shp = (B, L, A_HEADS, HEAD_DIM)
    qa = rope(rmsnorm(qa.reshape(shp), p['q_norm_g']), pos)
    ka = rope(rmsnorm(ka.reshape(shp), p['k_norm_g']), pos)
    oa, kv_new = mixer_a(qa, ka, va.reshape(shp), kv_bufs)
    ob, conv_b_new, h_b_new = mixer_b(xb, conv_b_buf, h_b0, p['conv_b_w'], p['conv_b_b'], p['lru_wa'], p['lru_ba'], p['lru_wx'], p['lru_bx'], p['lru_lambda'])
    oc, v_c_new = mixer_c(uvc, p['sgu_ln_g'], p['sgu_ln_b'], p['sgu_w'], p['sgu_b'])
    od, conv_d_new, ssm_d_new = mixer_d(zd, xbcd, dtd, conv_d_buf, ssm_d0, p['ssd_conv_w'], p['ssd_conv_b'], p['ssd_dt_bias'], p['ssd_a_log'], p['ssd_d'], p['ssd_norm_g'])
    g = jax.nn.sigmoid(gates.astype(F32)).reshape(B, L, N_BRANCH, D_MODEL)
    merged = (g[:, :, 0] * (oa @ p['w_pa']) + g[:, :, 1] * (ob @ p['w_pb'])
              + g[:, :, 2] * (oc @ p['w_pc']) + g[:, :, 3] * (od @ p['w_pd']))
    x = x + (merged @ p['w_o']).astype(x.dtype)
    x = x + hier_moe(rmsnorm(x, p['norm2_g']), p['router_group_w'], p['router_group_b'], p['router_exp_w'], p['router_exp_b'], p['moe_w_gate'], p['moe_w_up'], p['moe_w_down']).astype(x.dtype)
    return x, kv_new, conv_b_new, h_b_new, v_c_new, conv_d_new, ssm_d_new


def setup_inputs(seed: int = 0) -> dict:
    key = jax.random.key(seed)
    keys = jax.random.split(key, 64)
    ctr = [0]

    def nk():
        ctr[0] += 1
        return keys[ctr[0] - 1]

    def nrm(shape, scale=1.0):
        return scale * jax.random.normal(nk(), shape, F32)

    def gain(shape):
        return 1.0 + 0.1 * jax.random.normal(nk(), shape, F32)

    kvshape = lambda w: (DEPTH, DEC_BATCH, min(w, PAST_LEN), 2, A_HPG, HEAD_DIM)
    a0 = jax.random.uniform(nk(), (DEPTH, LRU_WIDTH), F32, 0.9, 0.999)
    s0 = a0 ** (1.0 / LRU_C)
    dt0 = jnp.exp(jax.random.uniform(nk(), (DEPTH, SSD_HEADS), F32, math.log(1e-3), math.log(1e-1)))
    return {
        'x_prompt': nrm((BATCH, SEQ, D_MODEL)),
        'x_sample': nrm((DEC_BATCH, DEC_SEQ, D_MODEL)),
        'cache_kv_a1': nrm(kvshape(A_GROUPS[0][0])),
        'cache_kv_a2': nrm(kvshape(A_GROUPS[1][0])),
        'cache_kv_a3': nrm(kvshape(A_GROUPS[2][0])),
        'state_conv_b': nrm((DEPTH, DEC_BATCH, CONV_W - 1, LRU_WIDTH)),
        'state_h_b': nrm((DEPTH, DEC_BATCH, LRU_WIDTH), 0.5),
        'state_conv_d': nrm((DEPTH, DEC_BATCH, CONV_W - 1, SSD_CONV_CH)),
        'state_ssm_d': nrm((DEPTH, DEC_BATCH, SSD_HEADS, SSD_HEAD_DIM, SSD_STATE), 0.1),
        'norm1_g': gain((DEPTH, D_MODEL)),
        'w_in': nrm((DEPTH, D_MODEL, N_IN), D_MODEL ** -0.5),
        'q_norm_g': gain((DEPTH, HEAD_DIM)),
        'k_norm_g': gain((DEPTH, HEAD_DIM)),
        'conv_b_w': nrm((DEPTH, CONV_W, LRU_WIDTH), CONV_W ** -0.5),
        'conv_b_b': nrm((DEPTH, LRU_WIDTH), 0.1),
        'lru_wa': nrm((DEPTH, LRU_BLOCKS, LRU_BLOCK, LRU_BLOCK), LRU_BLOCK ** -0.5),
        'lru_ba': nrm((DEPTH, LRU_WIDTH), 0.1),
        'lru_wx': nrm((DEPTH, LRU_BLOCKS, LRU_BLOCK, LRU_BLOCK), LRU_BLOCK ** -0.5),
        'lru_bx': nrm((DEPTH, LRU_WIDTH), 0.1),
        'lru_lambda': jnp.log(s0) - jnp.log1p(-s0),
        'sgu_ln_g': gain((DEPTH, SGU_WIDTH)),
        'sgu_ln_b': nrm((DEPTH, SGU_WIDTH), 0.1),
        'sgu_w': nrm((DEPTH, SGU_GROUPS, SGU_CHUNK, SGU_CHUNK), SGU_CHUNK ** -0.5),
        'sgu_b': gain((DEPTH, SGU_GROUPS, SGU_CHUNK)),
        'ssd_conv_w': nrm((DEPTH, CONV_W, SSD_CONV_CH), CONV_W ** -0.5),
        'ssd_conv_b': nrm((DEPTH, SSD_CONV_CH), 0.1),
        'ssd_dt_bias': dt0 + jnp.log(-jnp.expm1(-dt0)),
        'ssd_a_log': jnp.log(jax.random.uniform(nk(), (DEPTH, SSD_HEADS), F32, 1.0, 16.0)),
        'ssd_d': gain((DEPTH, SSD_HEADS)),
        'ssd_norm_g': gain((DEPTH, SSD_INNER)),
        'w_pa': nrm((DEPTH, A_OUT, D_MODEL), A_OUT ** -0.5),
        'w_pb': nrm((DEPTH, LRU_WIDTH, D_MODEL), LRU_WIDTH ** -0.5),
        'w_pc': nrm((DEPTH, SGU_WIDTH, D_MODEL), SGU_WIDTH ** -0.5),
        'w_pd': nrm((DEPTH, SSD_INNER, D_MODEL), SSD_INNER ** -0.5),
        'w_o': nrm((DEPTH, D_MODEL, D_MODEL), D_MODEL ** -0.5),
        'norm2_g': gain((DEPTH, D_MODEL)),
        'router_group_w': nrm((DEPTH, D_MODEL, N_EXPERT_GROUPS), D_MODEL ** -0.5),
        'router_group_b': nrm((DEPTH, N_EXPERT_GROUPS), 0.01),
        'router_exp_w': nrm((DEPTH, N_EXPERT_GROUPS, D_MODEL, EXPERTS_PER_GROUP), D_MODEL ** -0.5),
        'router_exp_b': nrm((DEPTH, N_EXPERT_GROUPS, EXPERTS_PER_GROUP), 0.01),
        'moe_w_gate': nrm((DEPTH, N_EXPERTS, D_MODEL, D_EXPERT), D_MODEL ** -0.5),
        'moe_w_up': nrm((DEPTH, N_EXPERTS, D_MODEL, D_EXPERT), D_MODEL ** -0.5),
        'moe_w_down': nrm((DEPTH, N_EXPERTS, D_EXPERT, D_MODEL), D_EXPERT ** -0.5),
    }


def reference(x_prompt, x_sample, cache_kv_a1, cache_kv_a2, cache_kv_a3, state_conv_b, state_h_b, state_conv_d, state_ssm_d,
              norm1_g, w_in, q_norm_g, k_norm_g, conv_b_w, conv_b_b, lru_wa, lru_ba, lru_wx, lru_bx, lru_lambda,
              sgu_ln_g, sgu_ln_b, sgu_w, sgu_b, ssd_conv_w, ssd_conv_b, ssd_dt_bias, ssd_a_log, ssd_d, ssd_norm_g,
              w_pa, w_pb, w_pc, w_pd, w_o, norm2_g, router_group_w, router_group_b, router_exp_w, router_exp_b,
              moe_w_gate, moe_w_up, moe_w_down):
    yp, ys = x_prompt, x_sample
    Bp = x_prompt.shape[0]
    P_kv1, P_kv2, P_kv3, P_cb, P_hb, P_cd, P_sd = [], [], [], [], [], [], []
    S_kv1, S_kv2, S_kv3, S_cb, S_hb, S_vc, S_cd, S_sd = [], [], [], [], [], [], [], []
    for l in range(DEPTH):
        p = dict(norm1_g=norm1_g[l], w_in=w_in[l], q_norm_g=q_norm_g[l], k_norm_g=k_norm_g[l],
                 conv_b_w=conv_b_w[l], conv_b_b=conv_b_b[l], lru_wa=lru_wa[l], lru_ba=lru_ba[l],
                 lru_wx=lru_wx[l], lru_bx=lru_bx[l], lru_lambda=lru_lambda[l],
                 sgu_ln_g=sgu_ln_g[l], sgu_ln_b=sgu_ln_b[l], sgu_w=sgu_w[l], sgu_b=sgu_b[l],
                 ssd_conv_w=ssd_conv_w[l], ssd_conv_b=ssd_conv_b[l], ssd_dt_bias=ssd_dt_bias[l],
                 ssd_a_log=ssd_a_log[l], ssd_d=ssd_d[l], ssd_norm_g=ssd_norm_g[l],
                 w_pa=w_pa[l], w_pb=w_pb[l], w_pc=w_pc[l], w_pd=w_pd[l], w_o=w_o[l], norm2_g=norm2_g[l],
                 router_group_w=router_group_w[l], router_group_b=router_group_b[l],
                 router_exp_w=router_exp_w[l], router_exp_b=router_exp_b[l],
                 moe_w_gate=moe_w_gate[l], moe_w_up=moe_w_up[l], moe_w_down=moe_w_down[l])
        yp, kv, cb, hb, _, cd, sd = layer(
            yp, 0, None,
            jnp.zeros((Bp, CONV_W - 1, LRU_WIDTH), yp.dtype), jnp.zeros((Bp, LRU_WIDTH), F32),
            jnp.zeros((Bp, CONV_W - 1, SSD_CONV_CH), yp.dtype),
            jnp.zeros((Bp, SSD_HEADS, SSD_HEAD_DIM, SSD_STATE), F32), p)
        P_kv1.append(kv[0]); P_kv2.append(kv[1]); P_kv3.append(kv[2])
        P_cb.append(cb); P_hb.append(hb); P_cd.append(cd); P_sd.append(sd)
        ys, kv, cb, hb, vc, cd, sd = layer(
            ys, PAST_LEN, (cache_kv_a1[l], cache_kv_a2[l], cache_kv_a3[l]),
            state_conv_b[l], state_h_b[l], state_conv_d[l], state_ssm_d[l], p)
        S_kv1.append(kv[0]); S_kv2.append(kv[1]); S_kv3.append(kv[2])
        S_cb.append(cb); S_hb.append(hb); S_vc.append(vc); S_cd.append(cd); S_sd.append(sd)
    st = jnp.stack
    return (yp, ys,
            st(P_kv1), st(P_kv2), st(P_kv3), st(P_cb), st(P_hb), st(P_cd), st(P_sd),
            st(S_kv1), st(S_kv2), st(S_kv3), st(S_cb), st(S_hb), st(S_vc), st(S_cd), st(S_sd))
```

```python
import functools

import jax
import jax.numpy as jnp
from jax import lax
from jax.experimental import pallas as pl
from jax.experimental.pallas import tpu as pltpu

F32 = jnp.float32
BF16 = jnp.bfloat16
HIGHEST = lax.Precision.HIGHEST

D_MODEL = 1024
DEPTH = 4
PAST_LEN = 8192
EPS = 1e-6
HEAD_DIM = 64
A_HPG = 4
A_GROUPS = ((128, 1), (512, 4), (2048, 16))
A_WIDTH = 768
A_OUT = 256
BLK = 128
ROPE_THETA = 10000.0
LRU_WIDTH = 768
LRU_C = 8.0
CONV_W = 4
SGU_WIDTH = 768
SGU_GROUPS = 4
SGU_CHUNK = 128
SSD_INNER = 768
SSD_HEADS = 12
SSD_GROUPS = 2
SSD_STATE = 128
SSD_CHUNK = 128
SSD_CONV_CH = 1280
N_BRANCH = 4
N_EXPERT_GROUPS = 4
EXPERTS_PER_GROUP = 4
N_EXPERTS = 16
D_EXPERT = 512
IN_SIZES = (768, 768, 768, 768, 1536, 768, 1280, 12, 4096)

LANES = 128
SUBLANES = 8
VMEM_LIMIT = 56 * 1024 * 1024


def _cp(*sem):
    return pltpu.CompilerParams(dimension_semantics=sem, vmem_limit_bytes=VMEM_LIMIT)


def _const_spec(shape):
    nd = len(shape)
    return pl.BlockSpec(shape, lambda *_: (0,) * nd, pipeline_mode=pl.Buffered(1))


def _rms(x, g):
    return x * lax.rsqrt(jnp.mean(x * x, axis=-1, keepdims=True) + EPS) * g


def _bdot(a, b):
    return jnp.dot(a.astype(BF16), b.astype(BF16), preferred_element_type=F32)


def _mm(a, w):
    if w.dtype == F32:
        return jnp.dot(a.astype(F32), w, preferred_element_type=F32, precision=HIGHEST)
    return jnp.dot(a.astype(BF16), w, preferred_element_type=F32)


def _bdot_nt(a, b):
    return lax.dot_general(a.astype(BF16), b.astype(BF16), (((1,), (1,)), ((), ())),
                           preferred_element_type=F32)


def _bdot_tn(a, b):
    return lax.dot_general(a.astype(BF16), b.astype(BF16), (((0,), (0,)), ((), ())),
                           preferred_element_type=F32)


def _sigmoid(x):
    return jax.nn.sigmoid(x)


def _silu(x):
    return x * jax.nn.sigmoid(x)


def _neg_expm1(x):
    u = jnp.exp(x)
    safe = jnp.logical_and(u != 1.0, u > 0.0)
    lu = jnp.log(jnp.where(safe, u, 0.5))
    em1 = jnp.where(u == 1.0, x, jnp.where(u > 0.0, (u - 1.0) * x / lu, -1.0))
    return -em1


def _lanes6(t):
    return jnp.concatenate([t] * 6, axis=1)


def _qkv_body(x_ref, g1_ref, wq_ref, wk_ref, wv_ref, qg_ref, kg_ref, gbd_ref,
              cos_ref, sa_ref, sb_ref, q_ref, k_ref, v_ref):
    h = _rms(x_ref[...], g1_ref[...]).astype(wq_ref.dtype)
    cos = _lanes6(cos_ref[...])
    sa = _lanes6(sa_ref[...])
    sb = _lanes6(sb_ref[...])
    gbd = gbd_ref[...]

    def normed_rotated(w_ref, hg_ref):
        t = _mm(h, w_ref[...])
        t2 = t * t
        if gbd.dtype == F32:
            ss = _mm(t2, gbd)
        else:
            hi = t2.astype(BF16)
            ss = _mm(hi, gbd) + _mm(t2 - hi.astype(F32), gbd)
        tn = t * lax.rsqrt(ss * (1.0 / HEAD_DIM) + EPS) * hg_ref[...]
        return (tn * cos + pltpu.roll(tn, A_WIDTH - HEAD_DIM // 2, 1) * sa
                + pltpu.roll(tn, HEAD_DIM // 2, 1) * sb)

    q_ref[...] = (normed_rotated(wq_ref, qg_ref) * (HEAD_DIM ** -0.5)).astype(q_ref.dtype)
    k_ref[...] = normed_rotated(wk_ref, kg_ref)
    v_ref[...] = _mm(h, wv_ref[...])


def _qkv(x2d, lw, tabs, tm):
    R = x2d.shape[0]
    cos, sa, sb = tabs
    npos = cos.shape[0] // tm
    row = lambda i: (i, 0)
    pos = lambda i: (i % npos, 0)
    return pl.pallas_call(
        _qkv_body,
        grid=(R // tm,),
        in_specs=[pl.BlockSpec((tm, D_MODEL), row), _const_spec((1, D_MODEL)),
                  _const_spec((D_MODEL, A_WIDTH)), _const_spec((D_MODEL, A_WIDTH)),
                  _const_spec((D_MODEL, A_WIDTH)),
                  _const_spec((1, A_WIDTH)), _const_spec((1, A_WIDTH)),
                  _const_spec((A_WIDTH, A_WIDTH)),
                  pl.BlockSpec((tm, LANES), pos), pl.BlockSpec((tm, LANES), pos),
                  pl.BlockSpec((tm, LANES), pos)],
        out_specs=[pl.BlockSpec((tm, A_WIDTH), row)] * 3,
        out_shape=[jax.ShapeDtypeStruct((R, A_WIDTH), lw["wq"].dtype),
                   jax.ShapeDtypeStruct((R, A_WIDTH), F32),
                   jax.ShapeDtypeStruct((R, A_WIDTH), F32)],
        compiler_params=_cp("parallel"),
        name="qkv_proj",
    )(x2d, lw["norm1_g"], lw["wq"], lw["wk"], lw["wv"], lw["q_norm_g"], lw["k_norm_g"],
      lw["gbd"], cos, sa, sb)


def _attn_group(q_ref, kp_ref, kc_ref, vp_ref, vc_ref, o_ref, l_ref, has_prev):
    qi = lax.broadcasted_iota(jnp.int32, (BLK, BLK), 0)
    kj = lax.broadcasted_iota(jnp.int32, (BLK, BLK), 1)
    mask_c = kj <= qi
    mask_p = jnp.logical_and(kj >= qi, has_prev)
    for hh in range(A_HPG):
        sl = slice(hh * HEAD_DIM, (hh + 1) * HEAD_DIM)
        qh = q_ref[:, sl]
        sp = jnp.where(mask_p, _bdot_nt(qh, kp_ref[:, sl]), -jnp.inf)
        sc = jnp.where(mask_c, _bdot_nt(qh, kc_ref[:, sl]), -jnp.inf)
        m = jnp.maximum(jnp.max(sp, axis=-1, keepdims=True), jnp.max(sc, axis=-1, keepdims=True))
        ep = jnp.exp(sp - m)
        ec = jnp.exp(sc - m)
        den = jnp.sum(ep, axis=-1, keepdims=True) + jnp.sum(ec, axis=-1, keepdims=True)
        o = _bdot(ep / den, vp_ref[:, sl]) + _bdot(ec / den, vc_ref[:, sl])
        o_ref[:, sl] = o
        l_ref[:, sl] = jnp.broadcast_to(m + jnp.log(den), (BLK, HEAD_DIM))


def _attn_prompt_body(*refs):
    i = pl.program_id(1)
    ins, outs = refs[:15], refs[15:]
    for gi, (_, dil) in enumerate(A_GROUPS):
        q_ref, kp_ref, kc_ref, vp_ref, vc_ref = ins[5 * gi:5 * gi + 5]
        o_ref, l_ref = outs[2 * gi:2 * gi + 2]
        _attn_group(q_ref, kp_ref, kc_ref, vp_ref, vc_ref, o_ref, l_ref, (i // dil) > 0)


def _attn_prompt(q, k, v):
    B, S, _ = q.shape
    nblk = S // BLK
    args, in_specs, out_specs, out_shape = [], [], [], []
    for gi, (_, dil) in enumerate(A_GROUPS):
        assert S % (dil * BLK) == 0
        rows = S // dil
        cur = lambda b, i, gi=gi, dil=dil: (b, i // dil, 3 * (i % dil) + gi)
        prev = lambda b, i, gi=gi, dil=dil: (b, jnp.maximum(i // dil - 1, 0), 3 * (i % dil) + gi)
        out = lambda b, i, dil=dil: (b, i // dil, i % dil)
        blk = (None, BLK, A_OUT)
        qv = q.reshape(B, rows, dil * A_WIDTH)
        kv = k.reshape(B, rows, dil * A_WIDTH)
        vv = v.reshape(B, rows, dil * A_WIDTH)
        args += [qv, kv, kv, vv, vv]
        in_specs += [pl.BlockSpec(blk, cur), pl.BlockSpec(blk, prev), pl.BlockSpec(blk, cur),
                     pl.BlockSpec(blk, prev), pl.BlockSpec(blk, cur)]
        out_specs += [pl.BlockSpec(blk, out)] * 2
        out_shape += [jax.ShapeDtypeStruct((B, rows, dil * A_OUT), F32)] * 2
    res = pl.pallas_call(
        _attn_prompt_body, grid=(B, nblk), in_specs=in_specs, out_specs=out_specs,
        out_shape=out_shape, compiler_params=_cp("parallel", "parallel"), name="attn_prompt",
    )(*args)
    return [r.reshape(B * S, A_OUT) for r in res]


def _attn_step_body(q_ref, k_ref, v_ref, c1_ref, c2_ref, c3_ref, *outs):
    lane = lax.broadcasted_iota(jnp.int32, (SUBLANES, A_OUT), 1)
    srow = lax.broadcasted_iota(jnp.int32, (SUBLANES, A_OUT), 0)
    head_mask = (lane // HEAD_DIM) == srow
    for gi, c_ref in enumerate((c1_ref, c2_ref, c3_ref)):
        o_ref, l_ref = outs[2 * gi:2 * gi + 2]
        sl = slice(gi * A_OUT, (gi + 1) * A_OUT)
        q = q_ref[:, sl]
        kn = k_ref[:, sl]
        vn = v_ref[:, sl]
        kc = c_ref[:, 0:A_OUT]
        vc = c_ref[:, A_OUT:2 * A_OUT]
        qm = jnp.where(head_mask, jnp.broadcast_to(q, (SUBLANES, A_OUT)), 0.0)
        s_c = lax.dot_general(qm, kc, (((1,), (1,)), ((), ())), preferred_element_type=F32,
                              precision=HIGHEST)
        s_n = jnp.sum(qm * kn, axis=-1, keepdims=True)
        m = jnp.maximum(jnp.max(s_c, axis=-1, keepdims=True), s_n)
        e_c = jnp.exp(s_c - m)
        e_n = jnp.exp(s_n - m)
        den = jnp.sum(e_c, axis=-1, keepdims=True) + e_n
        o8 = jnp.dot(e_c / den, vc, preferred_element_type=F32, precision=HIGHEST) + (e_n / den) * vn
        o_ref[...] = jnp.sum(jnp.where(head_mask, o8, 0.0), axis=0, keepdims=True)
        lse8 = jnp.broadcast_to(m + jnp.log(den), (SUBLANES, A_OUT))
        l_ref[...] = jnp.sum(jnp.where(head_mask, lse8, 0.0), axis=0, keepdims=True)


def _attn_step(q, k, v, caches):
    Bd = q.shape[0]
    row = lambda b: (b, 0, 0)
    args = [q.reshape(Bd, 1, A_WIDTH), k.reshape(Bd, 1, A_WIDTH), v.reshape(Bd, 1, A_WIDTH)]
    in_specs = [pl.BlockSpec((None, 1, A_WIDTH), row)] * 3
    for (window, dil), c in zip(A_GROUPS, caches):
        n_back = window // dil
        assert c.shape[1] == window and n_back == BLK
        args.append(c.reshape(Bd, n_back, dil * 2 * A_OUT))
        in_specs.append(pl.BlockSpec((None, n_back, 2 * A_OUT), row))
    res = pl.pallas_call(
        _attn_step_body, grid=(Bd,), in_specs=in_specs,
        out_specs=[pl.BlockSpec((None, 1, A_OUT), row)] * 6,
        out_shape=[jax.ShapeDtypeStruct((Bd, 1, A_OUT), F32)] * 6,
        compiler_params=_cp("parallel"), name="attn_step",
    )(*args)
    return [r.reshape(Bd, A_OUT) for r in res]


def _lru_gates(xc, wa_ref, ba_ref, wx_ref, bx_ref, lam_ref):
    r = _sigmoid(_mm(xc, wa_ref[...]) + ba_ref[...])
    i = _sigmoid(_mm(xc, wx_ref[...]) + bx_ref[...])
    log_a = -LRU_C * r * jax.nn.softplus(-lam_ref[...])
    a = jnp.exp(log_a)
    b = jnp.sqrt(_neg_expm1(2.0 * log_a)) * (i * xc)
    return a, b


def _scan_rows(a, b):
    T = a.shape[0]
    row = lax.broadcasted_iota(jnp.int32, a.shape, 0)
    k = 1
    while k < T:
        keep = row >= k
        a_s = jnp.where(keep, pltpu.roll(a, k, 0), 1.0)
        b_s = jnp.where(keep, pltpu.roll(b, k, 0), 0.0)
        b = a * b_s + b
        a = a * a_s
        k *= 2
    return a, b


def _lru_body(x_ref, g1_ref, w_ref, cw_ref, cb_ref, wa_ref, ba_ref, wx_ref, bx_ref, lam_ref,
              ob_ref, tail_ref, ext_ref, hc_ref, *, T):
    @pl.when(pl.program_id(1) == 0)
    def _():
        ext_ref[0:SUBLANES, :] = jnp.zeros((SUBLANES, LRU_WIDTH), F32)
        hc_ref[...] = jnp.zeros_like(hc_ref)

    h = _rms(x_ref[...], g1_ref[...])
    xb = _bdot(h, w_ref[...])
    ext_ref[SUBLANES:SUBLANES + T, :] = xb
    xc = cb_ref[...]
    for kk in range(CONV_W - 1):
        xc = xc + ext_ref[pl.ds(SUBLANES - (CONV_W - 1) + kk, T), :] * cw_ref[kk:kk + 1, :]
    xc = xc + xb * cw_ref[CONV_W - 1:CONV_W, :]
    tail = xb[T - SUBLANES:T]
    ext_ref[0:SUBLANES, :] = tail
    tail_ref[...] = tail

    a, b = _lru_gates(xc, wa_ref, ba_ref, wx_ref, bx_ref, lam_ref)
    a_cum, hloc = _scan_rows(a, b)
    hfull = hloc + a_cum * hc_ref[0:1, :]
    ob_ref[...] = hfull
    hc_ref[...] = jnp.broadcast_to(hfull[T - 1:T], hc_ref.shape)


def _lru_prompt(x, lw, T):
    B, S, _ = x.shape
    tile = lambda b, s: (b, s, 0)
    return pl.pallas_call(
        functools.partial(_lru_body, T=T),
        grid=(B, S // T),
        in_specs=[pl.BlockSpec((None, T, D_MODEL), tile), _const_spec((1, D_MODEL)),
                  _const_spec((D_MODEL, LRU_WIDTH)), _const_spec((CONV_W, LRU_WIDTH)),
                  _const_spec((1, LRU_WIDTH)), _const_spec((LRU_WIDTH, LRU_WIDTH)),
                  _const_spec((1, LRU_WIDTH)), _const_spec((LRU_WIDTH, LRU_WIDTH)),
                  _const_spec((1, LRU_WIDTH)), _const_spec((1, LRU_WIDTH))],
        out_specs=[pl.BlockSpec((None, T, LRU_WIDTH), tile),
                   pl.BlockSpec((None, SUBLANES, LRU_WIDTH), lambda b, s: (b, 0, 0))],
        out_shape=[jax.ShapeDtypeStruct((B, S, LRU_WIDTH), F32),
                   jax.ShapeDtypeStruct((B, SUBLANES, LRU_WIDTH), F32)],
        scratch_shapes=[pltpu.VMEM((T + SUBLANES, LRU_WIDTH), F32),
                        pltpu.VMEM((SUBLANES, LRU_WIDTH), F32)],
        compiler_params=_cp("parallel", "arbitrary"), name="lru_prompt",
    )(x, lw["norm1_g"], lw["w_xb"], lw["conv_b_w"], lw["conv_b_b"], lw["lru_wa_bd"], lw["lru_ba"],
      lw["lru_wx_bd"], lw["lru_bx"], lw["lru_lambda"])


def _gelu_ln(uv, lg_ref, lb_ref):
    uv = jax.nn.gelu(uv)
    u = uv[:, :SGU_WIDTH]
    v = uv[:, SGU_WIDTH:]
    mu = jnp.mean(v, axis=-1, keepdims=True)
    var = jnp.mean(jnp.square(v - mu), axis=-1, keepdims=True)
    v = (v - mu) * lax.rsqrt(var + EPS) * lg_ref[...] + lb_ref[...]
    return u, v


def _sgu_body(x_ref, g1_ref, w_ref, lg_ref, lb_ref, ws_ref, bs_ref, oc_ref, *, T):
    h = _rms(x_ref[...], g1_ref[...])
    u, v = _gelu_ln(_bdot(h, w_ref[...]), lg_ref, lb_ref)
    qi = lax.broadcasted_iota(jnp.int32, (SGU_CHUNK, SGU_CHUNK), 0)
    kj = lax.broadcasted_iota(jnp.int32, (SGU_CHUNK, SGU_CHUNK), 1)
    tril = (kj <= qi).astype(F32)
    lane = lax.broadcasted_iota(jnp.int32, (SGU_CHUNK, SGU_WIDTH), 1)
    gw = SGU_WIDTH // SGU_GROUPS
    wms = [(ws_ref[g] * tril).astype(BF16) for g in range(SGU_GROUPS)]
    for c in range(T // SGU_CHUNK):
        rows = slice(c * SGU_CHUNK, (c + 1) * SGU_CHUNK)
        vc = v[rows].astype(BF16)
        mixed = jnp.dot(wms[SGU_GROUPS - 1], vc, preferred_element_type=F32)
        for g in range(SGU_GROUPS - 2, -1, -1):
            mixed = jnp.where(lane < (g + 1) * gw, jnp.dot(wms[g], vc, preferred_element_type=F32), mixed)
        oc_ref[rows, :] = u[rows] * (mixed + bs_ref[...])


def _sgu_prompt(x2d, lw, T):
    R = x2d.shape[0]
    row = lambda i: (i, 0)
    return pl.pallas_call(
        functools.partial(_sgu_body, T=T),
        grid=(R // T,),
        in_specs=[pl.BlockSpec((T, D_MODEL), row), _const_spec((1, D_MODEL)),
                  _const_spec((D_MODEL, 2 * SGU_WIDTH)), _const_spec((1, SGU_WIDTH)),
                  _const_spec((1, SGU_WIDTH)), _const_spec((SGU_GROUPS, SGU_CHUNK, SGU_CHUNK)),
                  _const_spec((SGU_CHUNK, SGU_WIDTH))],
        out_specs=pl.BlockSpec((T, SGU_WIDTH), row),
        out_shape=jax.ShapeDtypeStruct((R, SGU_WIDTH), F32),
        compiler_params=_cp("parallel"), name="sgu_prompt",
    )(x2d, lw["norm1_g"], lw["w_uvc"], lw["sgu_ln_g"], lw["sgu_ln_b"], lw["sgu_w"], lw["sgu_b_tab"])


def _ssd_dt_a(dtr, dtb_ref, alog_ref):
    lane = lax.broadcasted_iota(jnp.int32, (1, LANES), 1)
    dt = jax.nn.softplus(dtr + dtb_ref[...])
    A = jnp.where(lane < SSD_HEADS, -jnp.exp(alog_ref[...]), 0.0)
    return dt, A * dt


def _ssd_gate_norm(y, xs, z, dsk_ref, ng_ref):
    y = y + dsk_ref[...] * xs
    y = y * _silu(z)
    gw = SSD_INNER // SSD_GROUPS
    parts = []
    for g in range(SSD_GROUPS):
        yg = y[:, g * gw:(g + 1) * gw]
        parts.append(yg * lax.rsqrt(jnp.mean(yg * yg, axis=-1, keepdims=True) + EPS))
    return jnp.concatenate(parts, axis=1) * ng_ref[...]


def _ssd_chunk(xs, dt, a, Bm, Cm, st_ref, y_ref, row0):
    Q = SSD_CHUNK
    qi = lax.broadcasted_iota(jnp.int32, (Q, Q), 0)
    kj = lax.broadcasted_iota(jnp.int32, (Q, Q), 1)
    tril = kj <= qi
    cs = jnp.dot(tril.astype(F32), a, preferred_element_type=F32, precision=HIGHEST)
    cs_t = cs.T
    ecs = jnp.exp(cs)
    cs_last = cs[Q - 1:Q, :]
    to_end = jnp.exp(cs_last - cs)
    e_last = jnp.exp(cs_last)
    hpg = SSD_HEADS // SSD_GROUPS
    for g in range(SSD_GROUPS):
        Cg = Cm[:, g * SSD_STATE:(g + 1) * SSD_STATE].astype(BF16)
        Bg = Bm[:, g * SSD_STATE:(g + 1) * SSD_STATE].astype(BF16)
        G = _bdot_nt(Cg, Bg)
        for hh in range(hpg):
            hd = g * hpg + hh
            sl = slice(hd * HEAD_DIM, (hd + 1) * HEAD_DIM)
            Lh = jnp.exp(jnp.where(tril, cs[:, hd:hd + 1] - cs_t[hd:hd + 1, :], -jnp.inf))
            Xh = xs[:, sl] * dt[:, hd:hd + 1]
            Sp = st_ref[sl, :]
            y = _bdot(G * Lh, Xh) + ecs[:, hd:hd + 1] * _bdot_nt(Cg, Sp)
            y_ref[row0:row0 + Q, sl] = y
            st_ref[sl, :] = e_last[:, hd:hd + 1] * Sp + _bdot_tn(Xh * to_end[:, hd:hd + 1], Bg)


def _ssd_body(x_ref, g1_ref, wz_ref, wxbc_ref, wdt_ref, cw_ref, cb_ref, dtb_ref, alog_ref, dsk_ref,
              ng_ref, od_ref, tail_ref, fin_ref, ext_ref, st_ref, y_ref, *, T):
    @pl.when(pl.program_id(1) == 0)
    def _():
        ext_ref[0:SUBLANES, :] = jnp.zeros((SUBLANES, SSD_CONV_CH), F32)
        st_ref[...] = jnp.zeros_like(st_ref)

    h = _rms(x_ref[...], g1_ref[...]).astype(BF16)
    z = jnp.dot(h, wz_ref[...], preferred_element_type=F32)
    xbc = jnp.dot(h, wxbc_ref[...], preferred_element_type=F32)
    dtr = jnp.dot(h, wdt_ref[...], preferred_element_type=F32)
    ext_ref[SUBLANES:SUBLANES + T, :] = xbc
    xc = cb_ref[...]
    for kk in range(CONV_W - 1):
        xc = xc + ext_ref[pl.ds(SUBLANES - (CONV_W - 1) + kk, T), :] * cw_ref[kk:kk + 1, :]
    xc = xc + xbc * cw_ref[CONV_W - 1:CONV_W, :]
    tail = xbc[T - SUBLANES:T]
    ext_ref[0:SUBLANES, :] = tail
    tail_ref[...] = tail

    xc = _silu(xc)
    xs = xc[:, :SSD_INNER]
    gn = SSD_GROUPS * SSD_STATE
    Bm = xc[:, SSD_INNER:SSD_INNER + gn]
    Cm = xc[:, SSD_INNER + gn:]
    dt, a = _ssd_dt_a(dtr, dtb_ref, alog_ref)
    for c in range(T // SSD_CHUNK):
        rows = slice(c * SSD_CHUNK, (c + 1) * SSD_CHUNK)
        _ssd_chunk(xs[rows], dt[rows], a[rows], Bm[rows], Cm[rows], st_ref, y_ref, c * SSD_CHUNK)
    od_ref[...] = _ssd_gate_norm(y_ref[...], xs, z, dsk_ref, ng_ref)
    fin_ref[...] = st_ref[...]


def _ssd_prompt(x, lw, T):
    B, S, _ = x.shape
    tile = lambda b, s: (b, s, 0)
    perb = lambda b, s: (b, 0, 0)
    return pl.pallas_call(
        functools.partial(_ssd_body, T=T),
        grid=(B, S // T),
        in_specs=[pl.BlockSpec((None, T, D_MODEL), tile), _const_spec((1, D_MODEL)),
                  _const_spec((D_MODEL, SSD_INNER)), _const_spec((D_MODEL, SSD_CONV_CH)),
                  _const_spec((D_MODEL, LANES)), _const_spec((CONV_W, SSD_CONV_CH)),
                  _const_spec((1, SSD_CONV_CH)), _const_spec((1, LANES)), _const_spec((1, LANES)),
                  _const_spec((1, SSD_INNER)), _const_spec((1, SSD_INNER))],
        out_specs=[pl.BlockSpec((None, T, SSD_INNER), tile),
                   pl.BlockSpec((None, SUBLANES, SSD_CONV_CH), perb),
                   pl.BlockSpec((None, SSD_INNER, SSD_STATE), perb)],
        out_shape=[jax.ShapeDtypeStruct((B, S, SSD_INNER), F32),
                   jax.ShapeDtypeStruct((B, SUBLANES, SSD_CONV_CH), F32),
                   jax.ShapeDtypeStruct((B, SSD_INNER, SSD_STATE), F32)],
        scratch_shapes=[pltpu.VMEM((T + SUBLANES, SSD_CONV_CH), F32),
                        pltpu.VMEM((SSD_INNER, SSD_STATE), F32),
                        pltpu.VMEM((T, SSD_INNER), F32)],
        compiler_params=_cp("parallel", "arbitrary"), name="ssd_prompt",
    )(x, lw["norm1_g"], lw["w_z"], lw["w_xbc"], lw["w_dt"], lw["ssd_conv_w"], lw["ssd_conv_b"],
      lw["ssd_dt_bias"], lw["ssd_a_log"], lw["ssd_d_tab"], lw["ssd_norm_g"])


def _proj_body(x_ref, g1_ref, w_ref, o_ref):
    o_ref[...] = _mm(_rms(x_ref[...], g1_ref[...]), w_ref[...])


def _proj(x2d, g1, w, tn):
    R, N = x2d.shape[0], w.shape[1]
    return pl.pallas_call(
        _proj_body, grid=(N // tn,),
        in_specs=[_const_spec((R, D_MODEL)), _const_spec((1, D_MODEL)),
                  pl.BlockSpec((D_MODEL, tn), lambda j: (0, j))],
        out_specs=pl.BlockSpec((R, tn), lambda j: (0, j)),
        out_shape=jax.ShapeDtypeStruct((R, N), F32),
        compiler_params=_cp("parallel"), name="proj_rest",
    )(x2d, g1, w)


def _conv_step(xnew, st_ref, cw_ref, cb_ref, C):
    out = cb_ref[...]
    for kk in range(CONV_W - 1):
        out = out + st_ref[:, kk * C:(kk + 1) * C] * cw_ref[kk:kk + 1, :]
    return out + xnew * cw_ref[CONV_W - 1:CONV_W, :]


def _dec_branches_body(proj_ref, scb_ref, hb_ref, scd_ref,
                       cwb_ref, cbb_ref, wa_ref, ba_ref, wx_ref, bx_ref, lam_ref,
                       lg_ref, lb_ref, w00_ref, b0_ref,
                       cwd_ref, cbd_ref, dtb_ref, alog_ref,
                       ob_ref, ncb_ref, oc_ref, vc_ref, ncd_ref, xs_ref, xdt_ref, ea_ref, bc_ref,
                       *, offs):
    o_xb, o_uv, o_z, o_xbc, o_dt = offs
    xb = proj_ref[:, o_xb:o_xb + LRU_WIDTH]
    xc = _conv_step(xb, scb_ref, cwb_ref, cbb_ref, LRU_WIDTH)
    a, b = _lru_gates(xc, wa_ref, ba_ref, wx_ref, bx_ref, lam_ref)
    ob_ref[...] = a * hb_ref[...] + b
    ncb_ref[:, 0:2 * LRU_WIDTH] = scb_ref[:, LRU_WIDTH:3 * LRU_WIDTH]
    ncb_ref[:, 2 * LRU_WIDTH:3 * LRU_WIDTH] = xb
    u, v = _gelu_ln(proj_ref[:, o_uv:o_uv + 2 * SGU_WIDTH], lg_ref, lb_ref)
    vc_ref[...] = v
    oc_ref[...] = u * (w00_ref[...] * v + b0_ref[...])
    xbc = proj_ref[:, o_xbc:o_xbc + SSD_CONV_CH]
    xcd = _silu(_conv_step(xbc, scd_ref, cwd_ref, cbd_ref, SSD_CONV_CH))
    ncd_ref[:, 0:2 * SSD_CONV_CH] = scd_ref[:, SSD_CONV_CH:3 * SSD_CONV_CH]
    ncd_ref[:, 2 * SSD_CONV_CH:3 * SSD_CONV_CH] = xbc
    xs = xcd[:, :SSD_INNER]
    dt, a_dt = _ssd_dt_a(proj_ref[:, o_dt:o_dt + LANES], dtb_ref, alog_ref)
    hrow = lax.broadcasted_iota(jnp.int32, (LANES, SSD_INNER), 0)
    hlane = lax.broadcasted_iota(jnp.int32, (LANES, SSD_INNER), 1)
    expand = (hlane // HEAD_DIM == hrow).astype(F32)
    dt_e = jnp.dot(dt, expand, preferred_element_type=F32, precision=HIGHEST)
    a_e = jnp.dot(a_dt, expand, preferred_element_type=F32, precision=HIGHEST)
    xs_ref[...] = xs
    xdt_ref[...] = xs * dt_e
    ea_ref[...] = jnp.exp(a_e)
    bc_ref[...] = xcd[:, SSD_INNER:]


def _dec_branches(proj, offs, scb, hb, scd, lw):
    Bd = proj.shape[0]
    f = lambda n: jax.ShapeDtypeStruct((Bd, n), F32)
    return pl.pallas_call(
        functools.partial(_dec_branches_body, offs=offs),
        out_shape=[f(LRU_WIDTH), f(3 * LRU_WIDTH), f(SGU_WIDTH), f(SGU_WIDTH), f(3 * SSD_CONV_CH),
                   f(SSD_INNER), f(SSD_INNER), f(SSD_INNER), f(2 * SSD_GROUPS * SSD_STATE)],
        compiler_params=pltpu.CompilerParams(vmem_limit_bytes=VMEM_LIMIT), name="dec_branches",
    )(proj, scb, hb, scd,
      lw["conv_b_w"], lw["conv_b_b"], lw["lru_wa_bd"], lw["lru_ba"], lw["lru_wx_bd"], lw["lru_bx"],
      lw["lru_lambda"], lw["sgu_ln_g"], lw["sgu_ln_b"], lw["sgu_w00_tab"], lw["sgu_b0_tab"],
      lw["ssd_conv_w"], lw["ssd_conv_b"], lw["ssd_dt_bias"], lw["ssd_a_log"])


def _dec_ssd_body(xs_ref, xdt_ref, ea_ref, bc_ref, z_ref, st_ref, dsk_ref, ng_ref, od_ref, ns_ref):
    gn = SSD_GROUPS * SSD_STATE
    half = SSD_INNER // SSD_GROUPS
    ridx = lax.broadcasted_iota(jnp.int32, (LANES, SSD_INNER), 0)
    rows = jnp.where(ridx == 0, jnp.broadcast_to(xdt_ref[...], (LANES, SSD_INNER)),
                     jnp.where(ridx == 1, jnp.broadcast_to(ea_ref[...], (LANES, SSD_INNER)), 0.0))
    cols = rows.T
    xdt_c = cols[:, 0:1]
    ea_c = cols[:, 1:2]
    bc = bc_ref[...]
    Bm, Cm = bc[:, :gn], bc[:, gn:]
    rowi = lax.broadcasted_iota(jnp.int32, (SSD_INNER, SSD_STATE), 0)
    b_full = jnp.where(rowi < half, jnp.broadcast_to(Bm[:, :SSD_STATE], (SSD_INNER, SSD_STATE)),
                       jnp.broadcast_to(Bm[:, SSD_STATE:], (SSD_INNER, SSD_STATE)))
    new = ea_c * st_ref[...] + xdt_c * b_full
    ns_ref[...] = new
    cidx = lax.broadcasted_iota(jnp.int32, (SUBLANES, SSD_STATE), 0)
    c8 = jnp.where(cidx == 0, jnp.broadcast_to(Cm[:, :SSD_STATE], (SUBLANES, SSD_STATE)),
                   jnp.where(cidx == 1, jnp.broadcast_to(Cm[:, SSD_STATE:], (SUBLANES, SSD_STATE)), 0.0))
    y8 = lax.dot_general(c8, new, (((1,), (1,)), ((), ())), preferred_element_type=F32,
                         precision=HIGHEST)
    lane = lax.broadcasted_iota(jnp.int32, (1, SSD_INNER), 1)
    y = jnp.where(lane < half, y8[0:1], y8[1:2])
    od_ref[...] = _ssd_gate_norm(y, xs_ref[...], z_ref[...], dsk_ref, ng_ref)


def _dec_ssd(xs, xdt, ea, bc, z, state, lw):
    Bd = xs.shape[0]
    row = lambda b: (b, 0, 0)
    r3 = lambda t: t.reshape(Bd, 1, t.shape[-1])
    st = state.reshape(Bd, SSD_INNER, SSD_STATE)
    od, ns = pl.pallas_call(
        _dec_ssd_body, grid=(Bd,),
        in_specs=[pl.BlockSpec((None, 1, SSD_INNER), row)] * 3
                 + [pl.BlockSpec((None, 1, 2 * SSD_GROUPS * SSD_STATE), row),
                    pl.BlockSpec((None, 1, SSD_INNER), row),
                    pl.BlockSpec((None, SSD_INNER, SSD_STATE), row),
                    _const_spec((1, SSD_INNER)), _const_spec((1, SSD_INNER))],
        out_specs=[pl.BlockSpec((None, 1, SSD_INNER), row),
                   pl.BlockSpec((None, SSD_INNER, SSD_STATE), row)],
        out_shape=[jax.ShapeDtypeStruct((Bd, 1, SSD_INNER), F32),
                   jax.ShapeDtypeStruct((Bd, SSD_INNER, SSD_STATE), F32)],
        compiler_params=_cp("parallel"), name="dec_ssd",
    )(r3(xs), r3(xdt), r3(ea), r3(bc), r3(z), st, lw["ssd_d_tab"], lw["ssd_norm_g"])
    return od.reshape(Bd, SSD_INNER), ns.reshape(state.shape)


def _merge_body(x_ref, o1_ref, l1_ref, o2_ref, l2_ref, o3_ref, l3_ref, ob_ref, oc_ref, od_ref,
                g1_ref, wg_ref, wpa_ref, wpb_ref, wpc_ref, wpd_ref, wo_ref, out_ref):
    x = x_ref[...]
    h = _rms(x, g1_ref[...]).astype(wg_ref.dtype)
    l1, l2, l3 = l1_ref[...], l2_ref[...], l3_ref[...]
    m = jnp.maximum(jnp.maximum(l1, l2), l3)
    e1, e2, e3 = jnp.exp(l1 - m), jnp.exp(l2 - m), jnp.exp(l3 - m)
    den = e1 + e2 + e3
    oa = (e1 / den) * o1_ref[...] + (e2 / den) * o2_ref[...] + (e3 / den) * o3_ref[...]
    merged = None
    for bi, (o, w_ref) in enumerate(((oa, wpa_ref), (ob_ref[...], wpb_ref),
                                     (oc_ref[...], wpc_ref), (od_ref[...], wpd_ref))):
        gate = _sigmoid(_mm(h, wg_ref[:, bi * D_MODEL:(bi + 1) * D_MODEL]))
        term = gate * _mm(o, w_ref[...])
        merged = term if merged is None else merged + term
    out_ref[...] = x + _mm(merged, wo_ref[...])


def _merge(x2d, att, ob, oc, od, lw, tm):
    R = x2d.shape[0]
    row = lambda i: (i, 0)
    return pl.pallas_call(
        _merge_body, grid=(R // tm,),
        in_specs=[pl.BlockSpec((tm, D_MODEL), row)] + [pl.BlockSpec((tm, A_OUT), row)] * 6
                 + [pl.BlockSpec((tm, LRU_WIDTH), row)] * 3
                 + [_const_spec((1, D_MODEL)), _const_spec((D_MODEL, N_BRANCH * D_MODEL)),
                    _const_spec((A_OUT, D_MODEL)), _const_spec((LRU_WIDTH, D_MODEL)),
                    _const_spec((SGU_WIDTH, D_MODEL)), _const_spec((SSD_INNER, D_MODEL)),
                    _const_spec((D_MODEL, D_MODEL))],
        out_specs=pl.BlockSpec((tm, D_MODEL), row),
        out_shape=jax.ShapeDtypeStruct((R, D_MODEL), F32),
        compiler_params=_cp("parallel"), name="merge",
    )(x2d, *att, ob, oc, od, lw["norm1_g"], lw["w_gates"], lw["w_pa"], lw["w_pb"], lw["w_pc"],
      lw["w_pd"], lw["w_o"])


def _route(logits):
    lane = lax.broadcasted_iota(jnp.int32, logits.shape, 1)
    big = jnp.int32(LANES)
    ninf = -jnp.inf
    gl = jnp.where(lane < N_EXPERT_GROUPS, logits, ninf)
    gm = jnp.max(gl, axis=-1, keepdims=True)
    gsel = jnp.min(jnp.where(gl == gm, lane, big), axis=-1, keepdims=True)
    pg = 1.0 / jnp.sum(jnp.exp(gl - gm), axis=-1, keepdims=True)
    lo = N_EXPERT_GROUPS + EXPERTS_PER_GROUP * gsel
    el = jnp.where(jnp.logical_and(lane >= lo, lane < lo + EXPERTS_PER_GROUP), logits, ninf)
    t1 = jnp.max(el, axis=-1, keepdims=True)
    i1 = jnp.min(jnp.where(el == t1, lane, big), axis=-1, keepdims=True)
    el2 = jnp.where(lane == i1, ninf, el)
    t2 = jnp.max(el2, axis=-1, keepdims=True)
    i2 = jnp.min(jnp.where(el2 == t2, lane, big), axis=-1, keepdims=True)
    e2 = jnp.exp(t2 - t1)
    den = 1.0 + e2
    w1 = (1.0 / den) * pg
    w2 = (e2 / den) * pg
    return jnp.where(lane == i1, w1, 0.0) + jnp.where(lane == i2, w2, 0.0)


def _moe_body(x_ref, g2_ref, wr_ref, br_ref, wg_ref, wu_ref, wd_ref, out_ref, h_ref, comb_ref, acc_ref):
    e = pl.program_id(1)

    @pl.when(e == 0)
    def _():
        h = _rms(x_ref[...], g2_ref[...])
        h_ref[...] = h.astype(h_ref.dtype)
        logits = jnp.dot(h, wr_ref[...], preferred_element_type=F32, precision=HIGHEST) + br_ref[...]
        comb_ref[...] = _route(logits)
        acc_ref[...] = jnp.zeros_like(acc_ref)

    h = h_ref[...]
    lane = lax.broadcasted_iota(jnp.int32, comb_ref.shape, 1)
    c = jnp.sum(jnp.where(lane == N_EXPERT_GROUPS + e, comb_ref[...], 0.0), axis=-1, keepdims=True)
    hg = _mm(h, wg_ref[...])
    hu = _mm(h, wu_ref[...])
    act = _silu(hg) * hu * c
    acc_ref[...] += _mm(act, wd_ref[...])

    @pl.when(e == N_EXPERTS - 1)
    def _():
        out_ref[...] = x_ref[...] + acc_ref[...]


def _moe(x2d, lw, tm):
    R = x2d.shape[0]
    row = lambda i, e: (i, 0)
    fixed = lambda shape: pl.BlockSpec(shape, lambda i, e: (0,) * len(shape), pipeline_mode=pl.Buffered(1))
    return pl.pallas_call(
        _moe_body, grid=(R // tm, N_EXPERTS),
        in_specs=[pl.BlockSpec((tm, D_MODEL), row), fixed((1, D_MODEL)), fixed((D_MODEL, LANES)),
                  fixed((1, LANES)),
                  pl.BlockSpec((None, D_MODEL, D_EXPERT), lambda i, e: (e, 0, 0)),
                  pl.BlockSpec((None, D_MODEL, D_EXPERT), lambda i, e: (e, 0, 0)),
                  pl.BlockSpec((None, D_EXPERT, D_MODEL), lambda i, e: (e, 0, 0))],
        out_specs=pl.BlockSpec((tm, D_MODEL), row),
        out_shape=jax.ShapeDtypeStruct((R, D_MODEL), F32),
        scratch_shapes=[pltpu.VMEM((tm, D_MODEL), lw["moe_w_gate"].dtype), pltpu.VMEM((tm, LANES), F32),
                        pltpu.VMEM((tm, D_MODEL), F32)],
        compiler_params=_cp("parallel", "arbitrary"), name="moe",
    )(x2d, lw["norm2_g"], lw["w_router"], lw["b_router"], lw["moe_w_gate"], lw["moe_w_up"],
      lw["moe_w_down"])


def _rope_tables(pos):
    half = HEAD_DIM // 2
    inv = ROPE_THETA ** (-jnp.arange(half, dtype=F32) / half)
    ang = pos.astype(F32)[:, None] * inv[None]
    c, s = jnp.cos(ang), jnp.sin(ang)
    z = jnp.zeros_like(s)
    reps = LANES // HEAD_DIM
    return (jnp.tile(jnp.concatenate([c, c], 1), (1, reps)),
            jnp.tile(jnp.concatenate([-s, z], 1), (1, reps)),
            jnp.tile(jnp.concatenate([z, s], 1), (1, reps)))


def _block_diag(w):
    n, k, _ = w.shape
    eye = jnp.eye(n, dtype=w.dtype)
    return (eye[:, None, :, None] * w[:, :, None, :]).reshape(n * k, n * k)


def _layer_params(p, l, wdt):
    row = lambda t: t[l].reshape(1, -1)
    w_in = p["w_in"][l]
    offs, o = [], 0
    for n in IN_SIZES:
        offs.append(o)
        o += n
    cut = lambda i: w_in[:, offs[i]:offs[i] + IN_SIZES[i]]
    lw = {}
    lw["norm1_g"] = row(p["norm1_g"])
    lw["norm2_g"] = row(p["norm2_g"])
    lw["wq"], lw["wk"], lw["wv"] = cut(0).astype(wdt), cut(1).astype(wdt), cut(2).astype(wdt)
    lw["w_xb"] = cut(3).astype(wdt)
    lw["w_uvc"] = cut(4).astype(wdt)
    lw["w_z"] = cut(5).astype(wdt)
    lw["w_xbc"] = cut(6).astype(wdt)
    lw["w_dt"] = jnp.pad(cut(7), ((0, 0), (0, LANES - SSD_HEADS))).astype(wdt)
    lw["w_gates"] = cut(8).astype(wdt)
    lw["w_rest"] = jnp.concatenate([lw["w_xb"], lw["w_uvc"], lw["w_z"], lw["w_xbc"], lw["w_dt"]], axis=1)
    reps = A_WIDTH // HEAD_DIM
    lw["q_norm_g"] = jnp.tile(p["q_norm_g"][l], reps).reshape(1, -1)
    lw["k_norm_g"] = jnp.tile(p["k_norm_g"][l], reps).reshape(1, -1)
    lw["gbd"] = _block_diag(jnp.ones((reps, HEAD_DIM, HEAD_DIM), wdt))
    lw["conv_b_w"] = p["conv_b_w"][l]
    lw["conv_b_b"] = row(p["conv_b_b"])
    lw["lru_wa_bd"] = _block_diag(p["lru_wa"][l]).astype(wdt)
    lw["lru_wx_bd"] = _block_diag(p["lru_wx"][l]).astype(wdt)
    lw["lru_ba"], lw["lru_bx"], lw["lru_lambda"] = row(p["lru_ba"]), row(p["lru_bx"]), row(p["lru_lambda"])
    lw["sgu_ln_g"], lw["sgu_ln_b"] = row(p["sgu_ln_g"]), row(p["sgu_ln_b"])
    lw["sgu_w"] = p["sgu_w"][l]
    gw = SGU_WIDTH // SGU_GROUPS
    lw["sgu_b_tab"] = jnp.repeat(p["sgu_b"][l].T, gw, axis=1)
    lw["sgu_w00_tab"] = jnp.repeat(p["sgu_w"][l][:, 0, 0], gw).reshape(1, -1)
    lw["sgu_b0_tab"] = jnp.repeat(p["sgu_b"][l][:, 0], gw).reshape(1, -1)
    lw["ssd_conv_w"] = p["ssd_conv_w"][l]
    lw["ssd_conv_b"] = row(p["ssd_conv_b"])
    padh = lambda t: jnp.pad(t[l], (0, LANES - SSD_HEADS)).reshape(1, -1)
    lw["ssd_dt_bias"], lw["ssd_a_log"] = padh(p["ssd_dt_bias"]), padh(p["ssd_a_log"])
    lw["ssd_d_tab"] = jnp.repeat(p["ssd_d"][l], HEAD_DIM).reshape(1, -1)
    lw["ssd_norm_g"] = row(p["ssd_norm_g"])
    for n in ("w_pa", "w_pb", "w_pc", "w_pd", "w_o", "moe_w_gate", "moe_w_up", "moe_w_down"):
        lw[n] = p[n][l].astype(wdt)
    we = jnp.transpose(p["router_exp_w"][l], (1, 0, 2)).reshape(D_MODEL, N_EXPERTS)
    wr = jnp.concatenate([p["router_group_w"][l], we], axis=1)
    lw["w_router"] = jnp.pad(wr, ((0, 0), (0, LANES - wr.shape[1])))
    br = jnp.concatenate([p["router_group_b"][l], p["router_exp_b"][l].reshape(-1)])
    lw["b_router"] = jnp.pad(br, (0, LANES - br.shape[0])).reshape(1, -1)
    return lw


def _kv_rows(k, v, gi):
    sl = slice(gi * A_OUT, (gi + 1) * A_OUT)
    shp = k.shape[:-1] + (A_HPG, HEAD_DIM)
    return jnp.stack([k[..., sl].reshape(shp), v[..., sl].reshape(shp)], axis=-3)


def _prompt_layer(x, lw, tabs):
    B, S, _ = x.shape
    x2d = x.reshape(B * S, D_MODEL)
    q, k, v = _qkv(x2d, lw, tabs, 512)
    r3 = lambda t: t.reshape(B, S, -1)
    att = _attn_prompt(r3(q), r3(k), r3(v))
    ob, tail_b = _lru_prompt(x, lw, 256)
    oc = _sgu_prompt(x2d, lw, 256)
    od, tail_d, fin = _ssd_prompt(x, lw, 256)
    x2d = _merge(x2d, att, ob.reshape(B * S, -1), oc, od.reshape(B * S, -1), lw, 256)
    x2d = _moe(x2d, lw, 512)
    k3, v3 = r3(k), r3(v)
    kvs = [_kv_rows(k3[:, S - min(w, S):], v3[:, S - min(w, S):], gi) for gi, (w, _) in enumerate(A_GROUPS)]
    nb = CONV_W - 1
    states = (tail_b[:, SUBLANES - nb:], ob[:, -1], tail_d[:, SUBLANES - nb:],
              fin.reshape(B, SSD_HEADS, HEAD_DIM, SSD_STATE))
    return x2d.reshape(B, S, D_MODEL), kvs, states


def _decode_layer(x2d, lw, tabs, caches, scb, hb, scd, ssm):
    Bd = x2d.shape[0]
    q, k, v = _qkv(x2d, lw, tabs, Bd)
    att = _attn_step(q, k, v, caches)
    proj = _proj(x2d, lw["norm1_g"], lw["w_rest"], 640)
    offs = (0, 768, 768 + 1536, 768 + 1536 + 768, 768 + 1536 + 768 + 1280)
    nb = CONV_W - 1
    ob, ncb, oc, vc, ncd, xs, xdt, ea, bc = _dec_branches(
        proj, offs, scb.reshape(Bd, nb * LRU_WIDTH), hb, scd.reshape(Bd, nb * SSD_CONV_CH), lw)
    z = proj[:, offs[2]:offs[2] + SSD_INNER]
    od, nssm = _dec_ssd(xs, xdt, ea, bc, z, ssm, lw)
    x2d = _merge(x2d, att, ob, oc, od, lw, Bd)
    x2d = _moe(x2d, lw, Bd)
    kvs = [_kv_rows(k, v, gi)[:, None] for gi in range(len(A_GROUPS))]
    states = (ncb.reshape(Bd, nb, LRU_WIDTH), ob, vc[:, None], ncd.reshape(Bd, nb, SSD_CONV_CH), nssm)
    return x2d, kvs, states


def kernel(x_prompt, x_sample, cache_kv_a1, cache_kv_a2, cache_kv_a3, state_conv_b, state_h_b, state_conv_d, state_ssm_d, norm1_g, w_in, q_norm_g, k_norm_g, conv_b_w, conv_b_b, lru_wa, lru_ba, lru_wx, lru_bx, lru_lambda, sgu_ln_g, sgu_ln_b, sgu_w, sgu_b, ssd_conv_w, ssd_conv_b, ssd_dt_bias, ssd_a_log, ssd_d, ssd_norm_g, w_pa, w_pb, w_pc, w_pd, w_o, norm2_g, router_group_w, router_group_b, router_exp_w, router_exp_b, moe_w_gate, moe_w_up, moe_w_down):
    p = dict(norm1_g=norm1_g, w_in=w_in, q_norm_g=q_norm_g, k_norm_g=k_norm_g, conv_b_w=conv_b_w,
             conv_b_b=conv_b_b, lru_wa=lru_wa, lru_ba=lru_ba, lru_wx=lru_wx, lru_bx=lru_bx,
             lru_lambda=lru_lambda, sgu_ln_g=sgu_ln_g, sgu_ln_b=sgu_ln_b, sgu_w=sgu_w, sgu_b=sgu_b,
             ssd_conv_w=ssd_conv_w, ssd_conv_b=ssd_conv_b, ssd_dt_bias=ssd_dt_bias, ssd_a_log=ssd_a_log,
             ssd_d=ssd_d, ssd_norm_g=ssd_norm_g, w_pa=w_pa, w_pb=w_pb, w_pc=w_pc, w_pd=w_pd, w_o=w_o,
             norm2_g=norm2_g, router_group_w=router_group_w, router_group_b=router_group_b,
             router_exp_w=router_exp_w, router_exp_b=router_exp_b, moe_w_gate=moe_w_gate,
             moe_w_up=moe_w_up, moe_w_down=moe_w_down)
    B, S, _ = x_prompt.shape
    Bd = x_sample.shape[0]
    tabs_p = _rope_tables(jnp.arange(S))
    tabs_s = tuple(jnp.broadcast_to(t, (Bd, LANES)) for t in _rope_tables(PAST_LEN + jnp.arange(1)))
    yp, ys = x_prompt, x_sample.reshape(Bd, D_MODEL)
    P = [[] for _ in range(7)]
    Sx = [[] for _ in range(8)]
    for l in range(DEPTH):
        lw = _layer_params(p, l, BF16)
        yp, kvs, st = _prompt_layer(yp, lw, tabs_p)
        for dst, val in zip(P, list(kvs) + list(st)):
            dst.append(val)
        ys, kvs, st = _decode_layer(ys, _layer_params(p, l, F32), tabs_s, (cache_kv_a1[l], cache_kv_a2[l], cache_kv_a3[l]),
                                    state_conv_b[l], state_h_b[l], state_conv_d[l], state_ssm_d[l])
        for dst, val in zip(Sx, list(kvs) + list(st)):
            dst.append(val)
    st = jnp.stack
    return (yp, ys.reshape(Bd, 1, D_MODEL)) + tuple(st(t) for t in P) + tuple(st(t) for t in Sx)
```

```python
import functools

import jax
import jax.numpy as jnp
from jax import lax
from jax.experimental import pallas as pl
from jax.experimental.pallas import tpu as pltpu

F32 = jnp.float32
BF16 = jnp.bfloat16
HIGHEST = lax.Precision.HIGHEST

D_MODEL = 1024
DEPTH = 4
PAST_LEN = 8192
EPS = 1e-6
HEAD_DIM = 64
A_HPG = 4
A_GROUPS = ((128, 1), (512, 4), (2048, 16))
A_WIDTH = 768
A_OUT = 256
BLK = 128
ROPE_THETA = 10000.0
LRU_WIDTH = 768
LRU_C = 8.0
CONV_W = 4
SGU_WIDTH = 768
SGU_GROUPS = 4
SGU_CHUNK = 128
SSD_INNER = 768
SSD_HEADS = 12
SSD_GROUPS = 2
SSD_STATE = 128
SSD_CHUNK = 128
SSD_CONV_CH = 1280
N_BRANCH = 4
N_EXPERT_GROUPS = 4
EXPERTS_PER_GROUP = 4
N_EXPERTS = 16
D_EXPERT = 512
IN_SIZES = (768, 768, 768, 768, 1536, 768, 1280, 12, 4096)
IN_OFFS = tuple(sum(IN_SIZES[:i]) for i in range(len(IN_SIZES)))

LANES = 128
SUBLANES = 8
VMEM_LIMIT = 56 * 1024 * 1024

TM_QKV = 512
T_SEQ = 256
TM_MERGE = 256
TM_MOE = 512


def _cp(*sem):
    return pltpu.CompilerParams(dimension_semantics=sem, vmem_limit_bytes=VMEM_LIMIT)


def _const_spec(shape):
    nd = len(shape)
    return pl.BlockSpec(shape, lambda *_: (0,) * nd, pipeline_mode=pl.Buffered(1))


def _lspec(a, l, block=None, idx=None):
    shape = tuple(a.shape[1:]) if block is None else tuple(block)
    tail = (0,) * len(shape) if idx is None else tuple(idx)
    return pl.BlockSpec((None,) + shape, lambda *_: (l,) + tail, pipeline_mode=pl.Buffered(1))


def _rms(x, g):
    return x * lax.rsqrt(jnp.mean(x * x, axis=-1, keepdims=True) + EPS) * g


def _mm(a, w):
    if w.dtype == F32:
        return jnp.dot(a.astype(F32), w, preferred_element_type=F32, precision=HIGHEST)
    return jnp.dot(a.astype(BF16), w, preferred_element_type=F32)


def _bdot(a, b):
    return jnp.dot(a.astype(BF16), b.astype(BF16), preferred_element_type=F32)


def _bdot_nt(a, b):
    return lax.dot_general(a.astype(BF16), b.astype(BF16), (((1,), (1,)), ((), ())),
                           preferred_element_type=F32)


def _bdot_tn(a, b):
    return lax.dot_general(a.astype(BF16), b.astype(BF16), (((0,), (0,)), ((), ())),
                           preferred_element_type=F32)


def _sigmoid(x):
    return jax.nn.sigmoid(x)


def _silu(x):
    return x * jax.nn.sigmoid(x)


def _neg_expm1(x):
    u = jnp.exp(x)
    safe = jnp.logical_and(u != 1.0, u > 0.0)
    lu = jnp.log(jnp.where(safe, u, 0.5))
    em1 = jnp.where(u == 1.0, x, jnp.where(u > 0.0, (u - 1.0) * x / lu, -1.0))
    return -em1


def _lanes6(t):
    return jnp.concatenate([t] * 6, axis=1)


def _qkv_body(x_ref, g1_ref, wq_ref, wk_ref, wv_ref, qg_ref, kg_ref, gbd_ref,
              cos_ref, sa_ref, sb_ref, *outs, split):
    h = _rms(x_ref[...], g1_ref[...]).astype(wq_ref.dtype)
    cos = _lanes6(cos_ref[...])
    sa = _lanes6(sa_ref[...])
    sb = _lanes6(sb_ref[...])
    gbd = gbd_ref[...]

    def normed_rotated(w_ref, hg_ref):
        t = _mm(h, w_ref[...])
        t2 = t * t
        if gbd.dtype == F32:
            ss = _mm(t2, gbd)
        else:
            hi = t2.astype(BF16)
            ss = _mm(hi, gbd) + _mm(t2 - hi.astype(F32), gbd)
        tn = t * lax.rsqrt(ss * (1.0 / HEAD_DIM) + EPS) * hg_ref[...]
        return (tn * cos + pltpu.roll(tn, A_WIDTH - HEAD_DIM // 2, 1) * sa
                + pltpu.roll(tn, HEAD_DIM // 2, 1) * sb)

    q = normed_rotated(wq_ref, qg_ref) * (HEAD_DIM ** -0.5)
    k = normed_rotated(wk_ref, kg_ref)
    v = _mm(h, wv_ref[...])
    if split:
        for gi in range(len(A_GROUPS)):
            sl = slice(gi * A_OUT, (gi + 1) * A_OUT)
            outs[gi][...] = q[:, sl].astype(BF16)
            outs[3 + gi][...] = k[:, sl].astype(BF16)
            outs[6 + gi][...] = v[:, sl].astype(BF16)
        outs[9][...] = k
        outs[10][...] = v
    else:
        outs[0][...] = q
        outs[1][...] = k
        outs[2][...] = v


def _qkv(x2d, W, l, tabs, tm, split):
    R = x2d.shape[0]
    cos, sa, sb = tabs
    npos = cos.shape[0] // tm
    row = lambda i: (i, 0)
    pos = lambda i: (i % npos, 0)
    wcol = lambda j: _lspec(W["w_in"], l, (D_MODEL, A_WIDTH), (0, j))
    full = pl.BlockSpec((tm, A_WIDTH), row)
    if split:
        out_specs = [pl.BlockSpec((tm, A_OUT), row)] * 9 + [full, full]
        out_shape = [jax.ShapeDtypeStruct((R, A_OUT), BF16)] * 9 + [jax.ShapeDtypeStruct((R, A_WIDTH), F32)] * 2
    else:
        out_specs = [full] * 3
        out_shape = [jax.ShapeDtypeStruct((R, A_WIDTH), F32)] * 3
    return pl.pallas_call(
        functools.partial(_qkv_body, split=split),
        grid=(R // tm,),
        in_specs=[pl.BlockSpec((tm, D_MODEL), row), _lspec(W["norm1_g"], l),
                  wcol(0), wcol(1), wcol(2),
                  _lspec(W["q_norm_g"], l), _lspec(W["k_norm_g"], l),
                  _const_spec((A_WIDTH, A_WIDTH)),
                  pl.BlockSpec((tm, LANES), pos), pl.BlockSpec((tm, LANES), pos),
                  pl.BlockSpec((tm, LANES), pos)],
        out_specs=out_specs, out_shape=out_shape,
        compiler_params=_cp("parallel"), name="qkv_proj",
    )(x2d, W["norm1_g"], W["w_in"], W["w_in"], W["w_in"], W["q_norm_g"], W["k_norm_g"],
      W["gbd"], cos, sa, sb)


def _attn_group(q_ref, kp_ref, kc_ref, vp_ref, vc_ref, o_ref, l_ref, mask, lane_head):
    q = q_ref[...]
    zero = jnp.zeros_like(q)
    qs = jnp.concatenate([jnp.where(lane_head == hh, q, zero) for hh in range(A_HPG)], axis=0)
    kk = jnp.concatenate([kp_ref[...], kc_ref[...]], axis=0)
    vv = jnp.concatenate([vp_ref[...], vc_ref[...]], axis=0)
    s = jnp.where(mask, _bdot_nt(qs, kk), -jnp.inf)
    m = jnp.max(s, axis=-1, keepdims=True)
    e = jnp.exp(s - m)
    den = jnp.sum(e, axis=-1, keepdims=True)
    o4 = _bdot(e * (1.0 / den), vv)
    lse = m + jnp.log(den)
    o = o4[(A_HPG - 1) * BLK:]
    lo = jnp.broadcast_to(lse[(A_HPG - 1) * BLK:], (BLK, A_OUT))
    for hh in range(A_HPG - 2, -1, -1):
        sel = lane_head == hh
        o = jnp.where(sel, o4[hh * BLK:(hh + 1) * BLK], o)
        lo = jnp.where(sel, jnp.broadcast_to(lse[hh * BLK:(hh + 1) * BLK], (BLK, A_OUT)), lo)
    o_ref[...] = o
    l_ref[...] = lo


def _attn_prompt_body(*refs):
    i = pl.program_id(1)
    ins, outs = refs[:15], refs[15:]
    rows = A_HPG * BLK
    qi = lax.broadcasted_iota(jnp.int32, (rows, 2 * BLK), 0) % BLK
    kj = lax.broadcasted_iota(jnp.int32, (rows, 2 * BLK), 1)
    mask_cur = jnp.logical_and(kj >= BLK, kj - BLK <= qi)
    mask_prev = jnp.logical_and(kj < BLK, kj >= qi)
    lane_head = lax.broadcasted_iota(jnp.int32, (BLK, A_OUT), 1) // HEAD_DIM
    for gi, (_, dil) in enumerate(A_GROUPS):
        q_ref, kp_ref, kc_ref, vp_ref, vc_ref = ins[5 * gi:5 * gi + 5]
        o_ref, l_ref = outs[2 * gi:2 * gi + 2]
        mask = jnp.logical_or(mask_cur, jnp.logical_and(mask_prev, (i // dil) > 0))
        _attn_group(q_ref, kp_ref, kc_ref, vp_ref, vc_ref, o_ref, l_ref, mask, lane_head)


def _attn_prompt(qkv, B, S):
    nblk = S // BLK
    args, in_specs, out_specs, out_shape = [], [], [], []
    for gi, (_, dil) in enumerate(A_GROUPS):
        assert S % (dil * BLK) == 0
        rows = S // dil
        cur = lambda b, i, dil=dil: (b, i // dil, i % dil)
        prev = lambda b, i, dil=dil: (b, jnp.maximum(i // dil - 1, 0), i % dil)
        blk = (None, BLK, A_OUT)
        view = lambda t: t.reshape(B, rows, dil * A_OUT)
        qv, kv, vv = view(qkv[gi]), view(qkv[3 + gi]), view(qkv[6 + gi])
        args += [qv, kv, kv, vv, vv]
        in_specs += [pl.BlockSpec(blk, cur), pl.BlockSpec(blk, prev), pl.BlockSpec(blk, cur),
                     pl.BlockSpec(blk, prev), pl.BlockSpec(blk, cur)]
        out_specs += [pl.BlockSpec(blk, cur)] * 2
        out_shape += [jax.ShapeDtypeStruct((B, rows, dil * A_OUT), F32)] * 2
    res = pl.pallas_call(
        _attn_prompt_body, grid=(B, nblk), in_specs=in_specs, out_specs=out_specs,
        out_shape=out_shape, compiler_params=_cp("parallel", "parallel"), name="attn_prompt",
    )(*args)
    return [r.reshape(B * S, A_OUT) for r in res]


def _attn_step_body(q_ref, k_ref, v_ref, c1_ref, c2_ref, c3_ref, *outs):
    lane = lax.broadcasted_iota(jnp.int32, (SUBLANES, A_OUT), 1)
    srow = lax.broadcasted_iota(jnp.int32, (SUBLANES, A_OUT), 0)
    head_mask = (lane // HEAD_DIM) == srow
    for gi, c_ref in enumerate((c1_ref, c2_ref, c3_ref)):
        o_ref, l_ref = outs[2 * gi:2 * gi + 2]
        sl = slice(gi * A_OUT, (gi + 1) * A_OUT)
        q = q_ref[:, sl]
        kn = k_ref[:, sl]
        vn = v_ref[:, sl]
        kc = c_ref[:, 0:A_OUT]
        vc = c_ref[:, A_OUT:2 * A_OUT]
        qm = jnp.where(head_mask, jnp.broadcast_to(q, (SUBLANES, A_OUT)), 0.0)
        s_c = lax.dot_general(qm, kc, (((1,), (1,)), ((), ())), preferred_element_type=F32,
                              precision=HIGHEST)
        s_n = jnp.sum(qm * kn, axis=-1, keepdims=True)
        m = jnp.maximum(jnp.max(s_c, axis=-1, keepdims=True), s_n)
        e_c = jnp.exp(s_c - m)
        e_n = jnp.exp(s_n - m)
        den = jnp.sum(e_c, axis=-1, keepdims=True) + e_n
        o8 = jnp.dot(e_c / den, vc, preferred_element_type=F32, precision=HIGHEST) + (e_n / den) * vn
        o_ref[...] = jnp.sum(jnp.where(head_mask, o8, 0.0), axis=0, keepdims=True)
        lse8 = jnp.broadcast_to(m + jnp.log(den), (SUBLANES, A_OUT))
        l_ref[...] = jnp.sum(jnp.where(head_mask, lse8, 0.0), axis=0, keepdims=True)


def _attn_step(q, k, v, caches, l):
    Bd = q.shape[0]
    row = lambda b: (b, 0, 0)
    args = [q.reshape(Bd, 1, A_WIDTH), k.reshape(Bd, 1, A_WIDTH), v.reshape(Bd, 1, A_WIDTH)]
    in_specs = [pl.BlockSpec((None, 1, A_WIDTH), row)] * 3
    for (window, dil), c in zip(A_GROUPS, caches):
        n_back = window // dil
        assert c.shape[2] == window and n_back == BLK
        args.append(c.reshape(c.shape[0], Bd, n_back, dil * 2 * A_OUT))
        in_specs.append(pl.BlockSpec((None, None, n_back, 2 * A_OUT), lambda b: (l, b, 0, 0)))
    res = pl.pallas_call(
        _attn_step_body, grid=(Bd,), in_specs=in_specs,
        out_specs=[pl.BlockSpec((None, 1, A_OUT), row)] * 6,
        out_shape=[jax.ShapeDtypeStruct((Bd, 1, A_OUT), F32)] * 6,
        compiler_params=_cp("parallel"), name="attn_step",
    )(*args)
    return [r.reshape(Bd, A_OUT) for r in res]


def _lru_gates(xc, wa_ref, ba_ref, wx_ref, bx_ref, lam_ref):
    r = _sigmoid(_mm(xc, wa_ref[...]) + ba_ref[...])
    i = _sigmoid(_mm(xc, wx_ref[...]) + bx_ref[...])
    log_a = -LRU_C * r * jax.nn.softplus(-lam_ref[...])
    a = jnp.exp(log_a)
    b = jnp.sqrt(_neg_expm1(2.0 * log_a)) * (i * xc)
    return a, b


def _scan_rows(a, b):
    T = a.shape[0]
    row = lax.broadcasted_iota(jnp.int32, a.shape, 0)
    k = 1
    while k < T:
        keep = row >= k
        a_s = jnp.where(keep, pltpu.roll(a, k, 0), 1.0)
        b_s = jnp.where(keep, pltpu.roll(b, k, 0), 0.0)
        b = a * b_s + b
        a = a * a_s
        k *= 2
    return a, b


def _lru_body(x_ref, g1_ref, w_ref, cw_ref, cb_ref, wa_ref, ba_ref, wx_ref, bx_ref, lam_ref,
              ob_ref, tail_ref, ext_ref, hc_ref, *, T):
    @pl.when(pl.program_id(1) == 0)
    def _():
        ext_ref[0:SUBLANES, :] = jnp.zeros((SUBLANES, LRU_WIDTH), F32)
        hc_ref[...] = jnp.zeros_like(hc_ref)

    h = _rms(x_ref[...], g1_ref[...])
    xb = _mm(h, w_ref[...])
    ext_ref[SUBLANES:SUBLANES + T, :] = xb
    xc = cb_ref[...]
    for kk in range(CONV_W - 1):
        xc = xc + ext_ref[pl.ds(SUBLANES - (CONV_W - 1) + kk, T), :] * cw_ref[kk:kk + 1, :]
    xc = xc + xb * cw_ref[CONV_W - 1:CONV_W, :]
    tail = xb[T - SUBLANES:T]
    ext_ref[0:SUBLANES, :] = tail
    tail_ref[...] = tail

    a, b = _lru_gates(xc, wa_ref, ba_ref, wx_ref, bx_ref, lam_ref)
    a_cum, hloc = _scan_rows(a, b)
    hfull = hloc + a_cum * hc_ref[0:1, :]
    ob_ref[...] = hfull
    hc_ref[...] = jnp.broadcast_to(hfull[T - 1:T], hc_ref.shape)


def _lru_prompt(x, W, l, T):
    B, S, _ = x.shape
    tile = lambda b, s: (b, s, 0)
    names = ("norm1_g", None, "conv_b_w", "conv_b_b", "lru_wa_bd", "lru_ba", "lru_wx_bd", "lru_bx", "lru_lambda")
    specs = [_lspec(W[n], l) if n else _lspec(W["w_in"], l, (D_MODEL, LRU_WIDTH), (0, IN_OFFS[3] // LRU_WIDTH))
             for n in names]
    return pl.pallas_call(
        functools.partial(_lru_body, T=T),
        grid=(B, S // T),
        in_specs=[pl.BlockSpec((None, T, D_MODEL), tile)] + specs,
        out_specs=[pl.BlockSpec((None, T, LRU_WIDTH), tile),
                   pl.BlockSpec((None, SUBLANES, LRU_WIDTH), lambda b, s: (b, 0, 0))],
        out_shape=[jax.ShapeDtypeStruct((B, S, LRU_WIDTH), F32),
                   jax.ShapeDtypeStruct((B, SUBLANES, LRU_WIDTH), F32)],
        scratch_shapes=[pltpu.VMEM((T + SUBLANES, LRU_WIDTH), F32),
                        pltpu.VMEM((SUBLANES, LRU_WIDTH), F32)],
        compiler_params=_cp("parallel", "arbitrary"), name="lru_prompt",
    )(x, *[W[n] if n else W["w_in"] for n in names])


def _gelu_ln(uv, lg_ref, lb_ref):
    uv = jax.nn.gelu(uv)
    u = uv[:, :SGU_WIDTH]
    v = uv[:, SGU_WIDTH:]
    mu = jnp.mean(v, axis=-1, keepdims=True)
    var = jnp.mean(jnp.square(v - mu), axis=-1, keepdims=True)
    v = (v - mu) * lax.rsqrt(var + EPS) * lg_ref[...] + lb_ref[...]
    return u, v


def _sgu_body(x_ref, g1_ref, w_ref, lg_ref, lb_ref, ws_ref, bs_ref, oc_ref, *, T):
    h = _rms(x_ref[...], g1_ref[...])
    u, v = _gelu_ln(_mm(h, w_ref[...]), lg_ref, lb_ref)
    qi = lax.broadcasted_iota(jnp.int32, (SGU_CHUNK, SGU_CHUNK), 0)
    kj = lax.broadcasted_iota(jnp.int32, (SGU_CHUNK, SGU_CHUNK), 1)
    tril = (kj <= qi).astype(F32)
    lane = lax.broadcasted_iota(jnp.int32, (SGU_CHUNK, SGU_WIDTH), 1)
    gw = SGU_WIDTH // SGU_GROUPS
    wms = [(ws_ref[g] * tril).astype(BF16) for g in range(SGU_GROUPS)]
    for c in range(T // SGU_CHUNK):
        rows = slice(c * SGU_CHUNK, (c + 1) * SGU_CHUNK)
        vc = v[rows].astype(BF16)
        mixed = jnp.dot(wms[SGU_GROUPS - 1], vc, preferred_element_type=F32)
        for g in range(SGU_GROUPS - 2, -1, -1):
            mixed = jnp.where(lane < (g + 1) * gw, jnp.dot(wms[g], vc, preferred_element_type=F32), mixed)
        oc_ref[rows, :] = u[rows] * (mixed + bs_ref[...])


def _sgu_prompt(x2d, W, l, T):
    R = x2d.shape[0]
    row = lambda i: (i, 0)
    return pl.pallas_call(
        functools.partial(_sgu_body, T=T),
        grid=(R // T,),
        in_specs=[pl.BlockSpec((T, D_MODEL), row), _lspec(W["norm1_g"], l),
                  _lspec(W["w_in"], l, (D_MODEL, 2 * SGU_WIDTH), (0, IN_OFFS[4] // (2 * SGU_WIDTH))),
                  _lspec(W["sgu_ln_g"], l), _lspec(W["sgu_ln_b"], l), _lspec(W["sgu_w"], l),
                  _lspec(W["sgu_b_tab"], l)],
        out_specs=pl.BlockSpec((T, SGU_WIDTH), row),
        out_shape=jax.ShapeDtypeStruct((R, SGU_WIDTH), F32),
        compiler_params=_cp("parallel"), name="sgu_prompt",
    )(x2d, W["norm1_g"], W["w_in"], W["sgu_ln_g"], W["sgu_ln_b"], W["sgu_w"], W["sgu_b_tab"])


def _ssd_dt_a(dtr, dtb_ref, alog_ref):
    lane = lax.broadcasted_iota(jnp.int32, (1, LANES), 1)
    dt = jax.nn.softplus(dtr + dtb_ref[...])
    A = jnp.where(lane < SSD_HEADS, -jnp.exp(alog_ref[...]), 0.0)
    return dt, A * dt


def _ssd_gate_norm(y, xs, z, dsk_ref, ng_ref):
    y = y + dsk_ref[...] * xs
    y = y * _silu(z)
    gw = SSD_INNER // SSD_GROUPS
    parts = []
    for g in range(SSD_GROUPS):
        yg = y[:, g * gw:(g + 1) * gw]
        parts.append(yg * lax.rsqrt(jnp.mean(yg * yg, axis=-1, keepdims=True) + EPS))
    return jnp.concatenate(parts, axis=1) * ng_ref[...]


def _ssd_chunk(xs, dt, a, Bm, Cm, st_ref, y_ref, row0):
    Q = SSD_CHUNK
    qi = lax.broadcasted_iota(jnp.int32, (Q, Q), 0)
    kj = lax.broadcasted_iota(jnp.int32, (Q, Q), 1)
    tril = kj <= qi
    cs = jnp.dot(tril.astype(F32), a, preferred_element_type=F32, precision=HIGHEST)
    cs_t = cs.T
    ecs = jnp.exp(cs)
    cs_last = cs[Q - 1:Q, :]
    to_end = jnp.exp(cs_last - cs)
    e_last = jnp.exp(cs_last)
    hpg = SSD_HEADS // SSD_GROUPS
    for g in range(SSD_GROUPS):
        Cg = Cm[:, g * SSD_STATE:(g + 1) * SSD_STATE].astype(BF16)
        Bg = Bm[:, g * SSD_STATE:(g + 1) * SSD_STATE].astype(BF16)
        G = _bdot_nt(Cg, Bg)
        for hh in range(hpg):
            hd = g * hpg + hh
            sl = slice(hd * HEAD_DIM, (hd + 1) * HEAD_DIM)
            Lh = jnp.exp(jnp.where(tril, cs[:, hd:hd + 1] - cs_t[hd:hd + 1, :], -jnp.inf))
            Xh = xs[:, sl] * dt[:, hd:hd + 1]
            Sp = st_ref[sl, :]
            y = _bdot(G * Lh, Xh) + ecs[:, hd:hd + 1] * _bdot_nt(Cg, Sp)
            y_ref[row0:row0 + Q, sl] = y
            st_ref[sl, :] = e_last[:, hd:hd + 1] * Sp + _bdot_tn(Xh * to_end[:, hd:hd + 1], Bg)


def _ssd_body(x_ref, g1_ref, wz_ref, wxbc_ref, wdt_ref, cw_ref, cb_ref, dtb_ref, alog_ref, dsk_ref,
              ng_ref, od_ref, tail_ref, fin_ref, ext_ref, st_ref, y_ref, *, T):
    @pl.when(pl.program_id(1) == 0)
    def _():
        ext_ref[0:SUBLANES, :] = jnp.zeros((SUBLANES, SSD_CONV_CH), F32)
        st_ref[...] = jnp.zeros_like(st_ref)

    h = _rms(x_ref[...], g1_ref[...]).astype(BF16)
    z = _mm(h, wz_ref[...])
    xbc = _mm(h, wxbc_ref[...])
    dtr = _mm(h, wdt_ref[...])
    ext_ref[SUBLANES:SUBLANES + T, :] = xbc
    xc = cb_ref[...]
    for kk in range(CONV_W - 1):
        xc = xc + ext_ref[pl.ds(SUBLANES - (CONV_W - 1) + kk, T), :] * cw_ref[kk:kk + 1, :]
    xc = xc + xbc * cw_ref[CONV_W - 1:CONV_W, :]
    tail = xbc[T - SUBLANES:T]
    ext_ref[0:SUBLANES, :] = tail
    tail_ref[...] = tail

    xc = _silu(xc)
    xs = xc[:, :SSD_INNER]
    gn = SSD_GROUPS * SSD_STATE
    Bm = xc[:, SSD_INNER:SSD_INNER + gn]
    Cm = xc[:, SSD_INNER + gn:]
    dt, a = _ssd_dt_a(dtr, dtb_ref, alog_ref)
    for c in range(T // SSD_CHUNK):
        rows = slice(c * SSD_CHUNK, (c + 1) * SSD_CHUNK)
        _ssd_chunk(xs[rows], dt[rows], a[rows], Bm[rows], Cm[rows], st_ref, y_ref, c * SSD_CHUNK)
    od_ref[...] = _ssd_gate_norm(y_ref[...], xs, z, dsk_ref, ng_ref)
    fin_ref[...] = st_ref[...]


def _ssd_prompt(x, W, l, T):
    B, S, _ = x.shape
    tile = lambda b, s: (b, s, 0)
    perb = lambda b, s: (b, 0, 0)
    wx = W["w_xbcdt"]
    return pl.pallas_call(
        functools.partial(_ssd_body, T=T),
        grid=(B, S // T),
        in_specs=[pl.BlockSpec((None, T, D_MODEL), tile), _lspec(W["norm1_g"], l),
                  _lspec(W["w_in"], l, (D_MODEL, SSD_INNER), (0, IN_OFFS[5] // SSD_INNER)),
                  _lspec(wx, l, (D_MODEL, SSD_CONV_CH), (0, 0)),
                  _lspec(wx, l, (D_MODEL, LANES), (0, SSD_CONV_CH // LANES)),
                  _lspec(W["ssd_conv_w"], l), _lspec(W["ssd_conv_b"], l), _lspec(W["ssd_dt_bias"], l),
                  _lspec(W["ssd_a_log"], l), _lspec(W["ssd_d_tab"], l), _lspec(W["ssd_norm_g"], l)],
        out_specs=[pl.BlockSpec((None, T, SSD_INNER), tile),
                   pl.BlockSpec((None, SUBLANES, SSD_CONV_CH), perb),
                   pl.BlockSpec((None, SSD_INNER, SSD_STATE), perb)],
        out_shape=[jax.ShapeDtypeStruct((B, S, SSD_INNER), F32),
                   jax.ShapeDtypeStruct((B, SUBLANES, SSD_CONV_CH), F32),
                   jax.ShapeDtypeStruct((B, SSD_INNER, SSD_STATE), F32)],
        scratch_shapes=[pltpu.VMEM((T + SUBLANES, SSD_CONV_CH), F32),
                        pltpu.VMEM((SSD_INNER, SSD_STATE), F32),
                        pltpu.VMEM((T, SSD_INNER), F32)],
        compiler_params=_cp("parallel", "arbitrary"), name="ssd_prompt",
    )(x, W["norm1_g"], W["w_in"], wx, wx, W["ssd_conv_w"], W["ssd_conv_b"],
      W["ssd_dt_bias"], W["ssd_a_log"], W["ssd_d_tab"], W["ssd_norm_g"])


def _proj_body(x_ref, g1_ref, w_ref, o_ref):
    o_ref[...] = _mm(_rms(x_ref[...], g1_ref[...]), w_ref[...])


def _proj(x2d, W, l, name, col0, ncols, tn):
    R = x2d.shape[0]
    assert col0 % tn == 0 and ncols % tn == 0
    j0 = col0 // tn
    return pl.pallas_call(
        _proj_body, grid=(ncols // tn,),
        in_specs=[_const_spec((R, D_MODEL)), _lspec(W["norm1_g"], l),
                  pl.BlockSpec((None, D_MODEL, tn), lambda j: (l, 0, j0 + j))],
        out_specs=pl.BlockSpec((R, tn), lambda j: (0, j)),
        out_shape=jax.ShapeDtypeStruct((R, ncols), F32),
        compiler_params=_cp("parallel"), name="proj_rest",
    )(x2d, W["norm1_g"], W[name])


def _conv_step(xnew, st_ref, cw_ref, cb_ref, C):
    out = cb_ref[...]
    for kk in range(CONV_W - 1):
        out = out + st_ref[:, kk * C:(kk + 1) * C] * cw_ref[kk:kk + 1, :]
    return out + xnew * cw_ref[CONV_W - 1:CONV_W, :]


def _dec_branches_body(proj_ref, dtr_ref, scb_ref, hb_ref, scd_ref,
                       cwb_ref, cbb_ref, wa_ref, ba_ref, wx_ref, bx_ref, lam_ref,
                       lg_ref, lb_ref, w00_ref, b0_ref,
                       cwd_ref, cbd_ref, dtb_ref, alog_ref,
                       ob_ref, ncb_ref, oc_ref, vc_ref, ncd_ref, xs_ref, xdt_ref, ea_ref, bc_ref,
                       *, offs):
    o_xb, o_uv, o_xbc = offs
    xb = proj_ref[:, o_xb:o_xb + LRU_WIDTH]
    xc = _conv_step(xb, scb_ref, cwb_ref, cbb_ref, LRU_WIDTH)
    a, b = _lru_gates(xc, wa_ref, ba_ref, wx_ref, bx_ref, lam_ref)
    ob_ref[...] = a * hb_ref[...] + b
    ncb_ref[:, 0:2 * LRU_WIDTH] = scb_ref[:, LRU_WIDTH:3 * LRU_WIDTH]
    ncb_ref[:, 2 * LRU_WIDTH:3 * LRU_WIDTH] = xb
    u, v = _gelu_ln(proj_ref[:, o_uv:o_uv + 2 * SGU_WIDTH], lg_ref, lb_ref)
    vc_ref[...] = v
    oc_ref[...] = u * (w00_ref[...] * v + b0_ref[...])
    xbc = proj_ref[:, o_xbc:o_xbc + SSD_CONV_CH]
    xcd = _silu(_conv_step(xbc, scd_ref, cwd_ref, cbd_ref, SSD_CONV_CH))
    ncd_ref[:, 0:2 * SSD_CONV_CH] = scd_ref[:, SSD_CONV_CH:3 * SSD_CONV_CH]
    ncd_ref[:, 2 * SSD_CONV_CH:3 * SSD_CONV_CH] = xbc
    xs = xcd[:, :SSD_INNER]
    dt, a_dt = _ssd_dt_a(dtr_ref[...], dtb_ref, alog_ref)
    hrow = lax.broadcasted_iota(jnp.int32, (LANES, SSD_INNER), 0)
    hlane = lax.broadcasted_iota(jnp.int32, (LANES, SSD_INNER), 1)
    expand = (hlane // HEAD_DIM == hrow).astype(F32)
    dt_e = jnp.dot(dt, expand, preferred_element_type=F32, precision=HIGHEST)
    a_e = jnp.dot(a_dt, expand, preferred_element_type=F32, precision=HIGHEST)
    xs_ref[...] = xs
    xdt_ref[...] = xs * dt_e
    ea_ref[...] = jnp.exp(a_e)
    bc_ref[...] = xcd[:, SSD_INNER:]


def _dec_branches(proj, dtr, offs, scb, hb, scd, W, l):
    Bd = proj.shape[0]
    f = lambda n: jax.ShapeDtypeStruct((Bd, n), F32)
    whole = lambda a: pl.BlockSpec(a.shape, lambda i: (0,) * a.ndim)
    names = ("conv_b_w", "conv_b_b", "lru_wa_bd", "lru_ba", "lru_wx_bd", "lru_bx", "lru_lambda",
             "sgu_ln_g", "sgu_ln_b", "sgu_w00_tab", "sgu_b0_tab",
             "ssd_conv_w", "ssd_conv_b", "ssd_dt_bias", "ssd_a_log")
    acts = (proj, dtr, scb, hb, scd)
    widths = (LRU_WIDTH, 3 * LRU_WIDTH, SGU_WIDTH, SGU_WIDTH, 3 * SSD_CONV_CH, SSD_INNER, SSD_INNER,
              SSD_INNER, 2 * SSD_GROUPS * SSD_STATE)
    return pl.pallas_call(
        functools.partial(_dec_branches_body, offs=offs),
        grid=(1,),
        in_specs=[whole(a) for a in acts] + [_lspec(W[n], l) for n in names],
        out_specs=[pl.BlockSpec((Bd, n), lambda i: (0, 0)) for n in widths],
        out_shape=[f(n) for n in widths],
        compiler_params=_cp("arbitrary"), name="dec_branches",
    )(*acts, *[W[n] for n in names])


def _dec_ssd_body(xs_ref, xdt_ref, ea_ref, bc_ref, z_ref, st_ref, dsk_ref, ng_ref, od_ref, ns_ref):
    gn = SSD_GROUPS * SSD_STATE
    half = SSD_INNER // SSD_GROUPS
    ridx = lax.broadcasted_iota(jnp.int32, (LANES, SSD_INNER), 0)
    rows = jnp.where(ridx == 0, jnp.broadcast_to(xdt_ref[...], (LANES, SSD_INNER)),
                     jnp.where(ridx == 1, jnp.broadcast_to(ea_ref[...], (LANES, SSD_INNER)), 0.0))
    cols = rows.T
    xdt_c = cols[:, 0:1]
    ea_c = cols[:, 1:2]
    bc = bc_ref[...]
    Bm, Cm = bc[:, :gn], bc[:, gn:]
    rowi = lax.broadcasted_iota(jnp.int32, (SSD_INNER, SSD_STATE), 0)
    b_full = jnp.where(rowi < half, jnp.broadcast_to(Bm[:, :SSD_STATE], (SSD_INNER, SSD_STATE)),
                       jnp.broadcast_to(Bm[:, SSD_STATE:], (SSD_INNER, SSD_STATE)))
    new = ea_c * st_ref[...] + xdt_c * b_full
    ns_ref[...] = new
    cidx = lax.broadcasted_iota(jnp.int32, (SUBLANES, SSD_STATE), 0)
    c8 = jnp.where(cidx == 0, jnp.broadcast_to(Cm[:, :SSD_STATE], (SUBLANES, SSD_STATE)),
                   jnp.where(cidx == 1, jnp.broadcast_to(Cm[:, SSD_STATE:], (SUBLANES, SSD_STATE)), 0.0))
    y8 = lax.dot_general(c8, new, (((1,), (1,)), ((), ())), preferred_element_type=F32,
                         precision=HIGHEST)
    lane = lax.broadcasted_iota(jnp.int32, (1, SSD_INNER), 1)
    y = jnp.where(lane < half, y8[0:1], y8[1:2])
    od_ref[...] = _ssd_gate_norm(y, xs_ref[...], z_ref[...], dsk_ref, ng_ref)


def _dec_ssd(xs, xdt, ea, bc, z, state_all, W, l):
    Bd = xs.shape[0]
    row = lambda b: (b, 0, 0)
    r3 = lambda t: t.reshape(Bd, 1, t.shape[-1])
    st = state_all.reshape(state_all.shape[0], Bd, SSD_INNER, SSD_STATE)
    od, ns = pl.pallas_call(
        _dec_ssd_body, grid=(Bd,),
        in_specs=[pl.BlockSpec((None, 1, SSD_INNER), row)] * 3
                 + [pl.BlockSpec((None, 1, 2 * SSD_GROUPS * SSD_STATE), row),
                    pl.BlockSpec((None, 1, SSD_INNER), row),
                    pl.BlockSpec((None, None, SSD_INNER, SSD_STATE), lambda b: (l, b, 0, 0)),
                    _lspec(W["ssd_d_tab"], l), _lspec(W["ssd_norm_g"], l)],
        out_specs=[pl.BlockSpec((None, 1, SSD_INNER), row),
                   pl.BlockSpec((None, SSD_INNER, SSD_STATE), row)],
        out_shape=[jax.ShapeDtypeStruct((Bd, 1, SSD_INNER), F32),
                   jax.ShapeDtypeStruct((Bd, SSD_INNER, SSD_STATE), F32)],
        compiler_params=_cp("parallel"), name="dec_ssd",
    )(r3(xs), r3(xdt), r3(ea), r3(bc), r3(z), st, W["ssd_d_tab"], W["ssd_norm_g"])
    return od.reshape(Bd, SSD_INNER), ns.reshape(state_all.shape[1:])


def _merge_body(x_ref, o1_ref, l1_ref, o2_ref, l2_ref, o3_ref, l3_ref, ob_ref, oc_ref, od_ref,
                g1_ref, wg_ref, wpa_ref, wpb_ref, wpc_ref, wpd_ref, wo_ref, out_ref):
    x = x_ref[...]
    h = _rms(x, g1_ref[...]).astype(wg_ref.dtype)
    l1, l2, l3 = l1_ref[...], l2_ref[...], l3_ref[...]
    m = jnp.maximum(jnp.maximum(l1, l2), l3)
    e1, e2, e3 = jnp.exp(l1 - m), jnp.exp(l2 - m), jnp.exp(l3 - m)
    den = e1 + e2 + e3
    oa = (e1 / den) * o1_ref[...] + (e2 / den) * o2_ref[...] + (e3 / den) * o3_ref[...]
    merged = None
    for bi, (o, w_ref) in enumerate(((oa, wpa_ref), (ob_ref[...], wpb_ref),
                                     (oc_ref[...], wpc_ref), (od_ref[...], wpd_ref))):
        gate = _sigmoid(_mm(h, wg_ref[:, bi * D_MODEL:(bi + 1) * D_MODEL]))
        term = gate * _mm(o, w_ref[...])
        merged = term if merged is None else merged + term
    out_ref[...] = x + _mm(merged, wo_ref[...])


def _merge(x2d, att, ob, oc, od, W, l, tm):
    R = x2d.shape[0]
    row = lambda i: (i, 0)
    names = ("norm1_g", "w_gates", "w_pa", "w_pb", "w_pc", "w_pd", "w_o")
    return pl.pallas_call(
        _merge_body, grid=(R // tm,),
        in_specs=[pl.BlockSpec((tm, D_MODEL), row)] + [pl.BlockSpec((tm, A_OUT), row)] * 6
                 + [pl.BlockSpec((tm, LRU_WIDTH), row)] * 3 + [_lspec(W[n], l) for n in names],
        out_specs=pl.BlockSpec((tm, D_MODEL), row),
        out_shape=jax.ShapeDtypeStruct((R, D_MODEL), F32),
        compiler_params=_cp("parallel"), name="merge",
    )(x2d, *att, ob, oc, od, *[W[n] for n in names])


def _route(logits):
    lane = lax.broadcasted_iota(jnp.int32, logits.shape, 1)
    big = jnp.int32(LANES)
    ninf = -jnp.inf
    gl = jnp.where(lane < N_EXPERT_GROUPS, logits, ninf)
    gm = jnp.max(gl, axis=-1, keepdims=True)
    gsel = jnp.min(jnp.where(gl == gm, lane, big), axis=-1, keepdims=True)
    pg = 1.0 / jnp.sum(jnp.exp(gl - gm), axis=-1, keepdims=True)
    lo = N_EXPERT_GROUPS + EXPERTS_PER_GROUP * gsel
    el = jnp.where(jnp.logical_and(lane >= lo, lane < lo + EXPERTS_PER_GROUP), logits, ninf)
    t1 = jnp.max(el, axis=-1, keepdims=True)
    i1 = jnp.min(jnp.where(el == t1, lane, big), axis=-1, keepdims=True)
    el2 = jnp.where(lane == i1, ninf, el)
    t2 = jnp.max(el2, axis=-1, keepdims=True)
    i2 = jnp.min(jnp.where(el2 == t2, lane, big), axis=-1, keepdims=True)
    e2 = jnp.exp(t2 - t1)
    den = 1.0 + e2
    w1 = (1.0 / den) * pg
    w2 = (e2 / den) * pg
    return jnp.where(lane == i1, w1, 0.0) + jnp.where(lane == i2, w2, 0.0)


def _moe_body(x_ref, g2_ref, wr_ref, br_ref, wg_ref, wu_ref, wd_ref, out_ref, h_ref, comb_ref, acc_ref):
    e = pl.program_id(1)

    @pl.when(e == 0)
    def _():
        h = _rms(x_ref[...], g2_ref[...])
        h_ref[...] = h.astype(h_ref.dtype)
        logits = jnp.dot(h, wr_ref[...], preferred_element_type=F32, precision=HIGHEST) + br_ref[...]
        comb_ref[...] = _route(logits)
        acc_ref[...] = jnp.zeros_like(acc_ref)

    h = h_ref[...]
    lane = lax.broadcasted_iota(jnp.int32, comb_ref.shape, 1)
    c = jnp.sum(jnp.where(lane == N_EXPERT_GROUPS + e, comb_ref[...], 0.0), axis=-1, keepdims=True)
    hg = _mm(h, wg_ref[...])
    hu = _mm(h, wu_ref[...])
    act = _silu(hg) * hu * c
    acc_ref[...] += _mm(act, wd_ref[...])

    @pl.when(e == N_EXPERTS - 1)
    def _():
        out_ref[...] = x_ref[...] + acc_ref[...]


def _moe(x2d, W, l, tm):
    R = x2d.shape[0]
    row = lambda i, e: (i, 0)
    expert = lambda r, c: pl.BlockSpec((None, None, r, c), lambda i, e: (l, e, 0, 0))
    return pl.pallas_call(
        _moe_body, grid=(R // tm, N_EXPERTS),
        in_specs=[pl.BlockSpec((tm, D_MODEL), row), _lspec(W["norm2_g"], l), _lspec(W["w_router"], l),
                  _lspec(W["b_router"], l),
                  expert(D_MODEL, D_EXPERT), expert(D_MODEL, D_EXPERT), expert(D_EXPERT, D_MODEL)],
        out_specs=pl.BlockSpec((tm, D_MODEL), row),
        out_shape=jax.ShapeDtypeStruct((R, D_MODEL), F32),
        scratch_shapes=[pltpu.VMEM((tm, D_MODEL), W["moe_w_gate"].dtype), pltpu.VMEM((tm, LANES), F32),
                        pltpu.VMEM((tm, D_MODEL), F32)],
        compiler_params=_cp("parallel", "arbitrary"), name="moe",
    )(x2d, W["norm2_g"], W["w_router"], W["b_router"], W["moe_w_gate"], W["moe_w_up"],
      W["moe_w_down"])


def _rope_tables(pos):
    half = HEAD_DIM // 2
    inv = ROPE_THETA ** (-jnp.arange(half, dtype=F32) / half)
    ang = pos.astype(F32)[:, None] * inv[None]
    c, s = jnp.cos(ang), jnp.sin(ang)
    z = jnp.zeros_like(s)
    reps = LANES // HEAD_DIM
    return (jnp.tile(jnp.concatenate([c, c], 1), (1, reps)),
            jnp.tile(jnp.concatenate([-s, z], 1), (1, reps)),
            jnp.tile(jnp.concatenate([z, s], 1), (1, reps)))


def _block_diag(w):
    L, n, k, _ = w.shape
    eye = jnp.eye(n, dtype=w.dtype)
    return (eye[None, :, None, :, None] * w[:, :, :, None, :]).reshape(L, n * k, n * k)


def _small_params(p):
    vec = lambda t: t.reshape(t.shape[0], 1, -1)
    W = {}
    for n in ("norm1_g", "norm2_g", "conv_b_b", "lru_ba", "lru_bx", "lru_lambda", "sgu_ln_g", "sgu_ln_b",
              "ssd_conv_b", "ssd_norm_g"):
        W[n] = vec(p[n])
    reps = A_WIDTH // HEAD_DIM
    W["q_norm_g"] = vec(jnp.tile(p["q_norm_g"], (1, reps)))
    W["k_norm_g"] = vec(jnp.tile(p["k_norm_g"], (1, reps)))
    W["conv_b_w"] = p["conv_b_w"]
    W["ssd_conv_w"] = p["ssd_conv_w"]
    W["sgu_w"] = p["sgu_w"]
    gw = SGU_WIDTH // SGU_GROUPS
    W["sgu_b_tab"] = jnp.repeat(jnp.swapaxes(p["sgu_b"], 1, 2), gw, axis=2)
    W["sgu_w00_tab"] = vec(jnp.repeat(p["sgu_w"][:, :, 0, 0], gw, axis=1))
    W["sgu_b0_tab"] = vec(jnp.repeat(p["sgu_b"][:, :, 0], gw, axis=1))
    padh = lambda t: vec(jnp.pad(t, ((0, 0), (0, LANES - SSD_HEADS))))
    W["ssd_dt_bias"], W["ssd_a_log"] = padh(p["ssd_dt_bias"]), padh(p["ssd_a_log"])
    W["ssd_d_tab"] = vec(jnp.repeat(p["ssd_d"], HEAD_DIM, axis=1))
    we = jnp.transpose(p["router_exp_w"], (0, 2, 1, 3)).reshape(-1, D_MODEL, N_EXPERTS)
    wr = jnp.concatenate([p["router_group_w"], we], axis=2)
    W["w_router"] = jnp.pad(wr, ((0, 0), (0, 0), (0, LANES - wr.shape[2])))
    br = jnp.concatenate([p["router_group_b"], p["router_exp_b"].reshape(-1, N_EXPERTS)], axis=1)
    W["b_router"] = vec(jnp.pad(br, ((0, 0), (0, LANES - br.shape[1]))))
    return W


def _matrix_params(p, wdt):
    W = {}
    w_in = p["w_in"].astype(wdt)
    W["w_in"] = w_in
    o_xbc, o_dt, o_g = IN_OFFS[6], IN_OFFS[7], IN_OFFS[8]
    W["w_xbcdt"] = jnp.concatenate(
        [w_in[:, :, o_xbc:o_dt], jnp.pad(w_in[:, :, o_dt:o_g], ((0, 0), (0, 0), (0, LANES - SSD_HEADS)))], axis=2)
    W["w_gates"] = w_in[:, :, o_g:]
    reps = A_WIDTH // HEAD_DIM
    W["gbd"] = _block_diag(jnp.ones((1, reps, HEAD_DIM, HEAD_DIM), wdt))[0]
    W["lru_wa_bd"] = _block_diag(p["lru_wa"]).astype(wdt)
    W["lru_wx_bd"] = _block_diag(p["lru_wx"]).astype(wdt)
    for n in ("w_pa", "w_pb", "w_pc", "w_pd", "w_o", "moe_w_gate", "moe_w_up", "moe_w_down"):
        W[n] = p[n].astype(wdt)
    return W


def _kv_rows(k, v, gi):
    sl = slice(gi * A_OUT, (gi + 1) * A_OUT)
    shp = k.shape[:-1] + (A_HPG, HEAD_DIM)
    return jnp.stack([k[..., sl].reshape(shp), v[..., sl].reshape(shp)], axis=-3)


def _prompt_layer(x, W, l, tabs):
    B, S, _ = x.shape
    x2d = x.reshape(B * S, D_MODEL)
    *qkv, k, v = _qkv(x2d, W, l, tabs, TM_QKV, split=True)
    att = _attn_prompt(qkv, B, S)
    ob, tail_b = _lru_prompt(x, W, l, T_SEQ)
    oc = _sgu_prompt(x2d, W, l, T_SEQ)
    od, tail_d, fin = _ssd_prompt(x, W, l, T_SEQ)
    x2d = _merge(x2d, att, ob.reshape(B * S, -1), oc, od.reshape(B * S, -1), W, l, TM_MERGE)
    x2d = _moe(x2d, W, l, TM_MOE)
    k3, v3 = k.reshape(B, S, -1), v.reshape(B, S, -1)
    kvs = [_kv_rows(k3[:, S - min(w, S):], v3[:, S - min(w, S):], gi) for gi, (w, _) in enumerate(A_GROUPS)]
    nb = CONV_W - 1
    states = (tail_b[:, SUBLANES - nb:], ob[:, -1], tail_d[:, SUBLANES - nb:],
              fin.reshape(B, SSD_HEADS, HEAD_DIM, SSD_STATE))
    return x2d.reshape(B, S, D_MODEL), kvs, states


def _decode_layer(x2d, W, l, tabs, caches, scb, hb, scd, ssm_all):
    Bd = x2d.shape[0]
    q, k, v = _qkv(x2d, W, l, tabs, Bd, split=False)
    att = _attn_step(q, k, v, caches, l)
    col0, ncols = IN_OFFS[3], IN_OFFS[7] - IN_OFFS[3]
    proj = _proj(x2d, W, l, "w_in", col0, ncols, 256)
    dtr = _proj(x2d, W, l, "w_xbcdt", SSD_CONV_CH, LANES, LANES)
    offs = (0, IN_OFFS[4] - col0, IN_OFFS[6] - col0)
    nb = CONV_W - 1
    ob, ncb, oc, vc, ncd, xs, xdt, ea, bc = _dec_branches(
        proj, dtr, offs, scb.reshape(Bd, nb * LRU_WIDTH), hb, scd.reshape(Bd, nb * SSD_CONV_CH), W, l)
    z = proj[:, IN_OFFS[5] - col0:IN_OFFS[5] - col0 + SSD_INNER]
    od, nssm = _dec_ssd(xs, xdt, ea, bc, z, ssm_all, W, l)
    x2d = _merge(x2d, att, ob, oc, od, W, l, Bd)
    x2d = _moe(x2d, W, l, Bd)
    kvs = [_kv_rows(k, v, gi)[:, None] for gi in range(len(A_GROUPS))]
    states = (ncb.reshape(Bd, nb, LRU_WIDTH), ob, vc[:, None], ncd.reshape(Bd, nb, SSD_CONV_CH), nssm)
    return x2d, kvs, states


def kernel(x_prompt, x_sample, cache_kv_a1, cache_kv_a2, cache_kv_a3, state_conv_b, state_h_b, state_conv_d, state_ssm_d, norm1_g, w_in, q_norm_g, k_norm_g, conv_b_w, conv_b_b, lru_wa, lru_ba, lru_wx, lru_bx, lru_lambda, sgu_ln_g, sgu_ln_b, sgu_w, sgu_b, ssd_conv_w, ssd_conv_b, ssd_dt_bias, ssd_a_log, ssd_d, ssd_norm_g, w_pa, w_pb, w_pc, w_pd, w_o, norm2_g, router_group_w, router_group_b, router_exp_w, router_exp_b, moe_w_gate, moe_w_up, moe_w_down):
    p = dict(norm1_g=norm1_g, w_in=w_in, q_norm_g=q_norm_g, k_norm_g=k_norm_g, conv_b_w=conv_b_w,
             conv_b_b=conv_b_b, lru_wa=lru_wa, lru_ba=lru_ba, lru_wx=lru_wx, lru_bx=lru_bx,
             lru_lambda=lru_lambda, sgu_ln_g=sgu_ln_g, sgu_ln_b=sgu_ln_b, sgu_w=sgu_w, sgu_b=sgu_b,
             ssd_conv_w=ssd_conv_w, ssd_conv_b=ssd_conv_b, ssd_dt_bias=ssd_dt_bias, ssd_a_log=ssd_a_log,
             ssd_d=ssd_d, ssd_norm_g=ssd_norm_g, w_pa=w_pa, w_pb=w_pb, w_pc=w_pc, w_pd=w_pd, w_o=w_o,
             norm2_g=norm2_g, router_group_w=router_group_w, router_group_b=router_group_b,
             router_exp_w=router_exp_w, router_exp_b=router_exp_b, moe_w_gate=moe_w_gate,
             moe_w_up=moe_w_up, moe_w_down=moe_w_down)
    B, S, _ = x_prompt.shape
    Bd = x_sample.shape[0]
    depth = w_in.shape[0]
    small = _small_params(p)
    Wp = dict(small, **_matrix_params(p, BF16))
    Wd = dict(small, **_matrix_params(p, F32))
    tabs_p = _rope_tables(jnp.arange(S))
    tabs_s = tuple(jnp.broadcast_to(t, (Bd, LANES)) for t in _rope_tables(PAST_LEN + jnp.arange(1)))
    caches = (cache_kv_a1, cache_kv_a2, cache_kv_a3)
    yp, ys = x_prompt, x_sample.reshape(Bd, D_MODEL)
    P = [[] for _ in range(7)]
    Sx = [[] for _ in range(8)]
    for l in range(depth):
        yp, kvs, st = _prompt_layer(yp, Wp, l, tabs_p)
        for dst, val in zip(P, list(kvs) + list(st)):
            dst.append(val)
        ys, kvs, st = _decode_layer(ys, Wd, l, tabs_s, caches, state_conv_b[l], state_h_b[l],
                                    state_conv_d[l], state_ssm_d)
        for dst, val in zip(Sx, list(kvs) + list(st)):
            dst.append(val)
    st = jnp.stack
    return (yp, ys.reshape(Bd, 1, D_MODEL)) + tuple(st(t) for t in P) + tuple(st(t) for t in Sx)
```

```python
import functools

import jax
import jax.numpy as jnp
from jax import lax
from jax.experimental import pallas as pl
from jax.experimental.pallas import tpu as pltpu

F32 = jnp.float32
BF16 = jnp.bfloat16
HIGHEST = lax.Precision.HIGHEST

D_MODEL = 1024
DEPTH = 4
PAST_LEN = 8192
EPS = 1e-6
HEAD_DIM = 64
A_HPG = 4
A_GROUPS = ((128, 1), (512, 4), (2048, 16))
A_WIDTH = 768
A_OUT = 256
BLK = 128
ROPE_THETA = 10000.0
LRU_WIDTH = 768
LRU_C = 8.0
CONV_W = 4
SGU_WIDTH = 768
SGU_GROUPS = 4
SGU_CHUNK = 128
SSD_INNER = 768
SSD_HEADS = 12
SSD_GROUPS = 2
SSD_STATE = 128
SSD_CHUNK = 128
SSD_CONV_CH = 1280
N_BRANCH = 4
N_EXPERT_GROUPS = 4
EXPERTS_PER_GROUP = 4
N_EXPERTS = 16
D_EXPERT = 512
IN_SIZES = (768, 768, 768, 768, 1536, 768, 1280, 12, 4096)
IN_OFFS = tuple(sum(IN_SIZES[:i]) for i in range(len(IN_SIZES)))

LANES = 128
SUBLANES = 8
VMEM_LIMIT = 56 * 1024 * 1024

TM_QKV = 512
T_SEQ = 256
TM_MERGE = 256
TM_MOE = 1024


def _cp(*sem):
    return pltpu.CompilerParams(dimension_semantics=sem, vmem_limit_bytes=VMEM_LIMIT)


def _const_spec(shape):
    nd = len(shape)
    return pl.BlockSpec(shape, lambda *_: (0,) * nd, pipeline_mode=pl.Buffered(1))


def _lspec(a, l, block=None, idx=None):
    shape = tuple(a.shape[1:]) if block is None else tuple(block)
    tail = (0,) * len(shape) if idx is None else tuple(idx)
    return pl.BlockSpec((None,) + shape, lambda *_: (l,) + tail, pipeline_mode=pl.Buffered(1))


def _rms(x, g):
    return x * lax.rsqrt(jnp.mean(x * x, axis=-1, keepdims=True) + EPS) * g


def _mm(a, w):
    if w.dtype == F32:
        return jnp.dot(a.astype(F32), w, preferred_element_type=F32, precision=HIGHEST)
    return jnp.dot(a.astype(BF16), w, preferred_element_type=F32)


def _bdot(a, b):
    return jnp.dot(a.astype(BF16), b.astype(BF16), preferred_element_type=F32)


def _bdot_nt(a, b):
    return lax.dot_general(a.astype(BF16), b.astype(BF16), (((1,), (1,)), ((), ())),
                           preferred_element_type=F32)


def _bdot_tn(a, b):
    return lax.dot_general(a.astype(BF16), b.astype(BF16), (((0,), (0,)), ((), ())),
                           preferred_element_type=F32)


def _sigmoid(x):
    return jax.nn.sigmoid(x)


def _silu(x):
    return x * jax.nn.sigmoid(x)


def _neg_expm1(x):
    u = jnp.exp(x)
    safe = jnp.logical_and(u != 1.0, u > 0.0)
    lu = jnp.log(jnp.where(safe, u, 0.5))
    em1 = jnp.where(u == 1.0, x, jnp.where(u > 0.0, (u - 1.0) * x / lu, -1.0))
    return -em1


def _lanes6(t):
    return jnp.concatenate([t] * 6, axis=1)


def _qkv_body(x_ref, g1_ref, wq_ref, wk_ref, wv_ref, qg_ref, kg_ref, gbd_ref,
              cos_ref, sa_ref, sb_ref, *outs, split):
    h = _rms(x_ref[...], g1_ref[...]).astype(wq_ref.dtype)
    cos = _lanes6(cos_ref[...])
    sa = _lanes6(sa_ref[...])
    sb = _lanes6(sb_ref[...])
    gbd = gbd_ref[...]

    def normed_rotated(w_ref, hg_ref):
        t = _mm(h, w_ref[...])
        t2 = t * t
        if gbd.dtype == F32:
            ss = _mm(t2, gbd)
        else:
            hi = t2.astype(BF16)
            ss = _mm(hi, gbd) + _mm(t2 - hi.astype(F32), gbd)
        tn = t * lax.rsqrt(ss * (1.0 / HEAD_DIM) + EPS) * hg_ref[...]
        return (tn * cos + pltpu.roll(tn, A_WIDTH - HEAD_DIM // 2, 1) * sa
                + pltpu.roll(tn, HEAD_DIM // 2, 1) * sb)

    q = normed_rotated(wq_ref, qg_ref) * (HEAD_DIM ** -0.5)
    k = normed_rotated(wk_ref, kg_ref)
    v = _mm(h, wv_ref[...])
    if split:
        for gi in range(len(A_GROUPS)):
            sl = slice(gi * A_OUT, (gi + 1) * A_OUT)
            outs[gi][...] = q[:, sl].astype(BF16)
            outs[3 + gi][...] = k[:, sl].astype(BF16)
            outs[6 + gi][...] = v[:, sl].astype(BF16)
        outs[9][...] = k
        outs[10][...] = v
    else:
        outs[0][...] = q
        outs[1][...] = k
        outs[2][...] = v


def _qkv(x2d, W, l, tabs, tm, split):
    R = x2d.shape[0]
    cos, sa, sb = tabs
    npos = cos.shape[0] // tm
    row = lambda i: (i, 0)
    pos = lambda i: (i % npos, 0)
    wcol = lambda j: _lspec(W["w_in"], l, (D_MODEL, A_WIDTH), (0, j))
    full = pl.BlockSpec((tm, A_WIDTH), row)
    if split:
        out_specs = [pl.BlockSpec((tm, A_OUT), row)] * 9 + [full, full]
        out_shape = [jax.ShapeDtypeStruct((R, A_OUT), BF16)] * 9 + [jax.ShapeDtypeStruct((R, A_WIDTH), F32)] * 2
    else:
        out_specs = [full] * 3
        out_shape = [jax.ShapeDtypeStruct((R, A_WIDTH), F32)] * 3
    return pl.pallas_call(
        functools.partial(_qkv_body, split=split),
        grid=(R // tm,),
        in_specs=[pl.BlockSpec((tm, D_MODEL), row), _lspec(W["norm1_g"], l),
                  wcol(0), wcol(1), wcol(2),
                  _lspec(W["q_norm_g"], l), _lspec(W["k_norm_g"], l),
                  _const_spec((A_WIDTH, A_WIDTH)),
                  pl.BlockSpec((tm, LANES), pos), pl.BlockSpec((tm, LANES), pos),
                  pl.BlockSpec((tm, LANES), pos)],
        out_specs=out_specs, out_shape=out_shape,
        compiler_params=_cp("parallel"), name="qkv_proj",
    )(x2d, W["norm1_g"], W["w_in"], W["w_in"], W["w_in"], W["q_norm_g"], W["k_norm_g"],
      W["gbd"], cos, sa, sb)


def _attn_group(q_ref, kp_ref, kc_ref, vp_ref, vc_ref, o_ref, l_ref, mask, lane_head):
    q = q_ref[...]
    zero = jnp.zeros_like(q)
    qs = jnp.concatenate([jnp.where(lane_head == hh, q, zero) for hh in range(A_HPG)], axis=0)
    kk = jnp.concatenate([kp_ref[...], kc_ref[...]], axis=0)
    vv = jnp.concatenate([vp_ref[...], vc_ref[...]], axis=0)
    s = jnp.where(mask, _bdot_nt(qs, kk), -jnp.inf)
    m = jnp.max(s, axis=-1, keepdims=True)
    e = jnp.exp(s - m)
    den = jnp.sum(e, axis=-1, keepdims=True)
    o4 = _bdot(e * (1.0 / den), vv)
    lse = m + jnp.log(den)
    o = o4[(A_HPG - 1) * BLK:]
    lo = jnp.broadcast_to(lse[(A_HPG - 1) * BLK:], (BLK, A_OUT))
    for hh in range(A_HPG - 2, -1, -1):
        sel = lane_head == hh
        o = jnp.where(sel, o4[hh * BLK:(hh + 1) * BLK], o)
        lo = jnp.where(sel, jnp.broadcast_to(lse[hh * BLK:(hh + 1) * BLK], (BLK, A_OUT)), lo)
    o_ref[...] = o
    l_ref[...] = lo


def _attn_prompt_body(*refs):
    i = pl.program_id(1)
    ins, outs = refs[:15], refs[15:]
    rows = A_HPG * BLK
    qi = lax.broadcasted_iota(jnp.int32, (rows, 2 * BLK), 0) % BLK
    kj = lax.broadcasted_iota(jnp.int32, (rows, 2 * BLK), 1)
    mask_cur = jnp.logical_and(kj >= BLK, kj - BLK <= qi)
    mask_prev = jnp.logical_and(kj < BLK, kj >= qi)
    lane_head = lax.broadcasted_iota(jnp.int32, (BLK, A_OUT), 1) // HEAD_DIM
    for gi, (_, dil) in enumerate(A_GROUPS):
        q_ref, kp_ref, kc_ref, vp_ref, vc_ref = ins[5 * gi:5 * gi + 5]
        o_ref, l_ref = outs[2 * gi:2 * gi + 2]
        mask = jnp.logical_or(mask_cur, jnp.logical_and(mask_prev, (i // dil) > 0))
        _attn_group(q_ref, kp_ref, kc_ref, vp_ref, vc_ref, o_ref, l_ref, mask, lane_head)


def _attn_prompt(qkv, B, S):
    nblk = S // BLK
    args, in_specs, out_specs, out_shape = [], [], [], []
    for gi, (_, dil) in enumerate(A_GROUPS):
        assert S % (dil * BLK) == 0
        rows = S // dil
        cur = lambda b, i, dil=dil: (b, i // dil, i % dil)
        prev = lambda b, i, dil=dil: (b, jnp.maximum(i // dil - 1, 0), i % dil)
        blk = (None, BLK, A_OUT)
        view = lambda t: t.reshape(B, rows, dil * A_OUT)
        qv, kv, vv = view(qkv[gi]), view(qkv[3 + gi]), view(qkv[6 + gi])
        args += [qv, kv, kv, vv, vv]
        in_specs += [pl.BlockSpec(blk, cur), pl.BlockSpec(blk, prev), pl.BlockSpec(blk, cur),
                     pl.BlockSpec(blk, prev), pl.BlockSpec(blk, cur)]
        out_specs += [pl.BlockSpec(blk, cur)] * 2
        out_shape += [jax.ShapeDtypeStruct((B, rows, dil * A_OUT), F32)] * 2
    res = pl.pallas_call(
        _attn_prompt_body, grid=(B, nblk), in_specs=in_specs, out_specs=out_specs,
        out_shape=out_shape, compiler_params=_cp("parallel", "parallel"), name="attn_prompt",
    )(*args)
    return [r.reshape(B * S, A_OUT) for r in res]


def _attn_step_body(q_ref, k_ref, v_ref, c1_ref, c2_ref, c3_ref, *outs):
    for gi, (c_ref, (_, dil)) in enumerate(zip((c1_ref, c2_ref, c3_ref), A_GROUPS)):
        o_ref, l_ref = outs[2 * gi:2 * gi + 2]
        W = c_ref.shape[-1]
        pos = lax.broadcasted_iota(jnp.int32, (SUBLANES, W), 1)
        valid = (pos % dil) == 0
        for hh in range(A_HPG):
            sl = slice(gi * A_OUT + hh * HEAD_DIM, gi * A_OUT + (hh + 1) * HEAD_DIM)
            q = q_ref[:, sl]
            q8 = jnp.broadcast_to(q, (SUBLANES, HEAD_DIM))
            s_c = jnp.dot(q8, c_ref[0, hh], preferred_element_type=F32, precision=HIGHEST)
            s_c = jnp.where(valid, s_c, -jnp.inf)
            s_n = jnp.sum(q * k_ref[:, sl], axis=-1, keepdims=True)
            m = jnp.maximum(jnp.max(s_c, axis=-1, keepdims=True), s_n)
            e_c = jnp.exp(s_c - m)
            e_n = jnp.exp(s_n - m)
            den = jnp.sum(e_c, axis=-1, keepdims=True) + e_n
            o8 = lax.dot_general(e_c / den, c_ref[1, hh], (((1,), (1,)), ((), ())),
                                 preferred_element_type=F32, precision=HIGHEST)
            o8 = o8 + (e_n / den) * v_ref[:, sl]
            osl = slice(hh * HEAD_DIM, (hh + 1) * HEAD_DIM)
            o_ref[:, osl] = o8[0:1]
            l_ref[:, osl] = jnp.broadcast_to((m + jnp.log(den))[0:1], (1, HEAD_DIM))


def _attn_step(q, k, v, caches, l):
    Bd = q.shape[0]
    row = lambda b: (b, 0, 0)
    args = [q.reshape(Bd, 1, A_WIDTH), k.reshape(Bd, 1, A_WIDTH), v.reshape(Bd, 1, A_WIDTH)]
    in_specs = [pl.BlockSpec((None, 1, A_WIDTH), row)] * 3
    for (window, dil), c in zip(A_GROUPS, caches):
        assert c.shape[2] == window and window % dil == 0
        args.append(jnp.transpose(c, (0, 1, 3, 4, 5, 2)))
        in_specs.append(pl.BlockSpec((None, None, 2, A_HPG, HEAD_DIM, window), lambda b: (l, b, 0, 0, 0, 0)))
    res = pl.pallas_call(
        _attn_step_body, grid=(Bd,), in_specs=in_specs,
        out_specs=[pl.BlockSpec((None, 1, A_OUT), row)] * 6,
        out_shape=[jax.ShapeDtypeStruct((Bd, 1, A_OUT), F32)] * 6,
        compiler_params=_cp("parallel"), name="attn_step",
    )(*args)
    return [r.reshape(Bd, A_OUT) for r in res]


def _lru_gates(xc, wa_ref, ba_ref, wx_ref, bx_ref, lam_ref):
    r = _sigmoid(_mm(xc, wa_ref[...]) + ba_ref[...])
    i = _sigmoid(_mm(xc, wx_ref[...]) + bx_ref[...])
    log_a = -LRU_C * r * jax.nn.softplus(-lam_ref[...])
    a = jnp.exp(log_a)
    b = jnp.sqrt(_neg_expm1(2.0 * log_a)) * (i * xc)
    return a, b


def _scan_rows(a, b):
    T = a.shape[0]
    row = lax.broadcasted_iota(jnp.int32, a.shape, 0)
    k = 1
    while k < T:
        keep = row >= k
        a_s = jnp.where(keep, pltpu.roll(a, k, 0), 1.0)
        b_s = jnp.where(keep, pltpu.roll(b, k, 0), 0.0)
        b = a * b_s + b
        a = a * a_s
        k *= 2
    return a, b


def _lru_body(x_ref, g1_ref, w_ref, cw_ref, cb_ref, wa_ref, ba_ref, wx_ref, bx_ref, lam_ref,
              ob_ref, tail_ref, ext_ref, hc_ref, *, T):
    @pl.when(pl.program_id(1) == 0)
    def _():
        ext_ref[0:SUBLANES, :] = jnp.zeros((SUBLANES, LRU_WIDTH), F32)
        hc_ref[...] = jnp.zeros_like(hc_ref)

    h = _rms(x_ref[...], g1_ref[...])
    xb = _mm(h, w_ref[...])
    ext_ref[SUBLANES:SUBLANES + T, :] = xb
    xc = cb_ref[...]
    for kk in range(CONV_W - 1):
        xc = xc + ext_ref[pl.ds(SUBLANES - (CONV_W - 1) + kk, T), :] * cw_ref[kk:kk + 1, :]
    xc = xc + xb * cw_ref[CONV_W - 1:CONV_W, :]
    tail = xb[T - SUBLANES:T]
    ext_ref[0:SUBLANES, :] = tail
    tail_ref[...] = tail

    a, b = _lru_gates(xc, wa_ref, ba_ref, wx_ref, bx_ref, lam_ref)
    a_cum, hloc = _scan_rows(a, b)
    hfull = hloc + a_cum * hc_ref[0:1, :]
    ob_ref[...] = hfull
    hc_ref[...] = jnp.broadcast_to(hfull[T - 1:T], hc_ref.shape)


def _lru_prompt(x, W, l, T):
    B, S, _ = x.shape
    tile = lambda b, s: (b, s, 0)
    names = ("norm1_g", None, "conv_b_w", "conv_b_b", "lru_wa_bd", "lru_ba", "lru_wx_bd", "lru_bx", "lru_lambda")
    specs = [_lspec(W[n], l) if n else _lspec(W["w_in"], l, (D_MODEL, LRU_WIDTH), (0, IN_OFFS[3] // LRU_WIDTH))
             for n in names]
    return pl.pallas_call(
        functools.partial(_lru_body, T=T),
        grid=(B, S // T),
        in_specs=[pl.BlockSpec((None, T, D_MODEL), tile)] + specs,
        out_specs=[pl.BlockSpec((None, T, LRU_WIDTH), tile),
                   pl.BlockSpec((None, SUBLANES, LRU_WIDTH), lambda b, s: (b, 0, 0))],
        out_shape=[jax.ShapeDtypeStruct((B, S, LRU_WIDTH), F32),
                   jax.ShapeDtypeStruct((B, SUBLANES, LRU_WIDTH), F32)],
        scratch_shapes=[pltpu.VMEM((T + SUBLANES, LRU_WIDTH), F32),
                        pltpu.VMEM((SUBLANES, LRU_WIDTH), F32)],
        compiler_params=_cp("parallel", "arbitrary"), name="lru_prompt",
    )(x, *[W[n] if n else W["w_in"] for n in names])


def _gelu_ln(uv, lg_ref, lb_ref):
    uv = jax.nn.gelu(uv)
    u = uv[:, :SGU_WIDTH]
    v = uv[:, SGU_WIDTH:]
    mu = jnp.mean(v, axis=-1, keepdims=True)
    var = jnp.mean(jnp.square(v - mu), axis=-1, keepdims=True)
    v = (v - mu) * lax.rsqrt(var + EPS) * lg_ref[...] + lb_ref[...]
    return u, v


def _sgu_body(x_ref, g1_ref, w_ref, lg_ref, lb_ref, ws_ref, bs_ref, oc_ref, *, T):
    h = _rms(x_ref[...], g1_ref[...])
    u, v = _gelu_ln(_mm(h, w_ref[...]), lg_ref, lb_ref)
    qi = lax.broadcasted_iota(jnp.int32, (SGU_CHUNK, SGU_CHUNK), 0)
    kj = lax.broadcasted_iota(jnp.int32, (SGU_CHUNK, SGU_CHUNK), 1)
    tril = (kj <= qi).astype(F32)
    lane = lax.broadcasted_iota(jnp.int32, (SGU_CHUNK, SGU_WIDTH), 1)
    gw = SGU_WIDTH // SGU_GROUPS
    wms = [(ws_ref[g] * tril).astype(BF16) for g in range(SGU_GROUPS)]
    for c in range(T // SGU_CHUNK):
        rows = slice(c * SGU_CHUNK, (c + 1) * SGU_CHUNK)
        vc = v[rows].astype(BF16)
        mixed = jnp.dot(wms[SGU_GROUPS - 1], vc, preferred_element_type=F32)
        for g in range(SGU_GROUPS - 2, -1, -1):
            mixed = jnp.where(lane < (g + 1) * gw, jnp.dot(wms[g], vc, preferred_element_type=F32), mixed)
        oc_ref[rows, :] = u[rows] * (mixed + bs_ref[...])


def _sgu_prompt(x2d, W, l, T):
    R = x2d.shape[0]
    row = lambda i: (i, 0)
    return pl.pallas_call(
        functools.partial(_sgu_body, T=T),
        grid=(R // T,),
        in_specs=[pl.BlockSpec((T, D_MODEL), row), _lspec(W["norm1_g"], l),
                  _lspec(W["w_in"], l, (D_MODEL, 2 * SGU_WIDTH), (0, IN_OFFS[4] // (2 * SGU_WIDTH))),
                  _lspec(W["sgu_ln_g"], l), _lspec(W["sgu_ln_b"], l), _lspec(W["sgu_w"], l),
                  _lspec(W["sgu_b_tab"], l)],
        out_specs=pl.BlockSpec((T, SGU_WIDTH), row),
        out_shape=jax.ShapeDtypeStruct((R, SGU_WIDTH), F32),
        compiler_params=_cp("parallel"), name="sgu_prompt",
    )(x2d, W["norm1_g"], W["w_in"], W["sgu_ln_g"], W["sgu_ln_b"], W["sgu_w"], W["sgu_b_tab"])


def _ssd_dt_a(dtr, dtb_ref, alog_ref):
    lane = lax.broadcasted_iota(jnp.int32, (1, LANES), 1)
    dt = jax.nn.softplus(dtr + dtb_ref[...])
    A = jnp.where(lane < SSD_HEADS, -jnp.exp(alog_ref[...]), 0.0)
    return dt, A * dt


def _ssd_gate_norm(y, xs, z, dsk_ref, ng_ref):
    y = y + dsk_ref[...] * xs
    y = y * _silu(z)
    gw = SSD_INNER // SSD_GROUPS
    parts = []
    for g in range(SSD_GROUPS):
        yg = y[:, g * gw:(g + 1) * gw]
        parts.append(yg * lax.rsqrt(jnp.mean(yg * yg, axis=-1, keepdims=True) + EPS))
    return jnp.concatenate(parts, axis=1) * ng_ref[...]


def _ssd_chunk(xs, dt, a, Bm, Cm, st_ref, y_ref, row0):
    Q = SSD_CHUNK
    qi = lax.broadcasted_iota(jnp.int32, (Q, Q), 0)
    kj = lax.broadcasted_iota(jnp.int32, (Q, Q), 1)
    tril = kj <= qi
    cs = jnp.dot(tril.astype(F32), a, preferred_element_type=F32, precision=HIGHEST)
    cs_t = cs.T
    ecs = jnp.exp(cs)
    cs_last = cs[Q - 1:Q, :]
    to_end = jnp.exp(cs_last - cs)
    e_last = jnp.exp(cs_last)
    hpg = SSD_HEADS // SSD_GROUPS
    for g in range(SSD_GROUPS):
        Cg = Cm[:, g * SSD_STATE:(g + 1) * SSD_STATE].astype(BF16)
        Bg = Bm[:, g * SSD_STATE:(g + 1) * SSD_STATE].astype(BF16)
        G = _bdot_nt(Cg, Bg)
        for hh in range(hpg):
            hd = g * hpg + hh
            sl = slice(hd * HEAD_DIM, (hd + 1) * HEAD_DIM)
            Lh = jnp.exp(jnp.where(tril, cs[:, hd:hd + 1] - cs_t[hd:hd + 1, :], -jnp.inf))
            Xh = xs[:, sl] * dt[:, hd:hd + 1]
            Sp = st_ref[sl, :]
            y = _bdot(G * Lh, Xh) + ecs[:, hd:hd + 1] * _bdot_nt(Cg, Sp)
            y_ref[row0:row0 + Q, sl] = y
            st_ref[sl, :] = e_last[:, hd:hd + 1] * Sp + _bdot_tn(Xh * to_end[:, hd:hd + 1], Bg)


def _ssd_body(x_ref, g1_ref, wz_ref, wxbc_ref, wdt_ref, cw_ref, cb_ref, dtb_ref, alog_ref, dsk_ref,
              ng_ref, od_ref, tail_ref, fin_ref, ext_ref, st_ref, y_ref, *, T):
    @pl.when(pl.program_id(1) == 0)
    def _():
        ext_ref[0:SUBLANES, :] = jnp.zeros((SUBLANES, SSD_CONV_CH), F32)
        st_ref[...] = jnp.zeros_like(st_ref)

    h = _rms(x_ref[...], g1_ref[...]).astype(BF16)
    z = _mm(h, wz_ref[...])
    xbc = _mm(h, wxbc_ref[...])
    dtr = _mm(h, wdt_ref[...])
    ext_ref[SUBLANES:SUBLANES + T, :] = xbc
    xc = cb_ref[...]
    for kk in range(CONV_W - 1):
        xc = xc + ext_ref[pl.ds(SUBLANES - (CONV_W - 1) + kk, T), :] * cw_ref[kk:kk + 1, :]
    xc = xc + xbc * cw_ref[CONV_W - 1:CONV_W, :]
    tail = xbc[T - SUBLANES:T]
    ext_ref[0:SUBLANES, :] = tail
    tail_ref[...] = tail

    xc = _silu(xc)
    xs = xc[:, :SSD_INNER]
    gn = SSD_GROUPS * SSD_STATE
    Bm = xc[:, SSD_INNER:SSD_INNER + gn]
    Cm = xc[:, SSD_INNER + gn:]
    dt, a = _ssd_dt_a(dtr, dtb_ref, alog_ref)
    for c in range(T // SSD_CHUNK):
        rows = slice(c * SSD_CHUNK, (c + 1) * SSD_CHUNK)
        _ssd_chunk(xs[rows], dt[rows], a[rows], Bm[rows], Cm[rows], st_ref, y_ref, c * SSD_CHUNK)
    od_ref[...] = _ssd_gate_norm(y_ref[...], xs, z, dsk_ref, ng_ref)
    fin_ref[...] = st_ref[...]


def _ssd_prompt(x, W, l, T):
    B, S, _ = x.shape
    tile = lambda b, s: (b, s, 0)
    perb = lambda b, s: (b, 0, 0)
    wx = W["w_xbcdt"]
    return pl.pallas_call(
        functools.partial(_ssd_body, T=T),
        grid=(B, S // T),
        in_specs=[pl.BlockSpec((None, T, D_MODEL), tile), _lspec(W["norm1_g"], l),
                  _lspec(W["w_in"], l, (D_MODEL, SSD_INNER), (0, IN_OFFS[5] // SSD_INNER)),
                  _lspec(wx, l, (D_MODEL, SSD_CONV_CH), (0, 0)),
                  _lspec(wx, l, (D_MODEL, LANES), (0, SSD_CONV_CH // LANES)),
                  _lspec(W["ssd_conv_w"], l), _lspec(W["ssd_conv_b"], l), _lspec(W["ssd_dt_bias"], l),
                  _lspec(W["ssd_a_log"], l), _lspec(W["ssd_d_tab"], l), _lspec(W["ssd_norm_g"], l)],
        out_specs=[pl.BlockSpec((None, T, SSD_INNER), tile),
                   pl.BlockSpec((None, SUBLANES, SSD_CONV_CH), perb),
                   pl.BlockSpec((None, SSD_INNER, SSD_STATE), perb)],
        out_shape=[jax.ShapeDtypeStruct((B, S, SSD_INNER), F32),
                   jax.ShapeDtypeStruct((B, SUBLANES, SSD_CONV_CH), F32),
                   jax.ShapeDtypeStruct((B, SSD_INNER, SSD_STATE), F32)],
        scratch_shapes=[pltpu.VMEM((T + SUBLANES, SSD_CONV_CH), F32),
                        pltpu.VMEM((SSD_INNER, SSD_STATE), F32),
                        pltpu.VMEM((T, SSD_INNER), F32)],
        compiler_params=_cp("parallel", "arbitrary"), name="ssd_prompt",
    )(x, W["norm1_g"], W["w_in"], wx, wx, W["ssd_conv_w"], W["ssd_conv_b"],
      W["ssd_dt_bias"], W["ssd_a_log"], W["ssd_d_tab"], W["ssd_norm_g"])


def _proj_body(x_ref, g1_ref, w_ref, o_ref):
    o_ref[...] = _mm(_rms(x_ref[...], g1_ref[...]), w_ref[...])


def _proj(x2d, W, l, name, col0, ncols, tn):
    R = x2d.shape[0]
    assert col0 % tn == 0 and ncols % tn == 0
    j0 = col0 // tn
    return pl.pallas_call(
        _proj_body, grid=(ncols // tn,),
        in_specs=[_const_spec((R, D_MODEL)), _lspec(W["norm1_g"], l),
                  pl.BlockSpec((None, D_MODEL, tn), lambda j: (l, 0, j0 + j))],
        out_specs=pl.BlockSpec((R, tn), lambda j: (0, j)),
        out_shape=jax.ShapeDtypeStruct((R, ncols), F32),
        compiler_params=_cp("parallel"), name="proj_rest",
    )(x2d, W["norm1_g"], W[name])


def _conv_step(xnew, st_ref, cw_ref, cb_ref, C):
    out = cb_ref[...]
    for kk in range(CONV_W - 1):
        out = out + st_ref[:, kk * C:(kk + 1) * C] * cw_ref[kk:kk + 1, :]
    return out + xnew * cw_ref[CONV_W - 1:CONV_W, :]


def _dec_branches_body(proj_ref, dtr_ref, scb_ref, hb_ref, scd_ref,
                       cwb_ref, cbb_ref, wa_ref, ba_ref, wx_ref, bx_ref, lam_ref,
                       lg_ref, lb_ref, w00_ref, b0_ref,
                       cwd_ref, cbd_ref, dtb_ref, alog_ref,
                       ob_ref, ncb_ref, oc_ref, vc_ref, ncd_ref, xs_ref, xdt_ref, ea_ref, bc_ref,
                       *, offs):
    o_xb, o_uv, o_xbc = offs
    xb = proj_ref[:, o_xb:o_xb + LRU_WIDTH]
    xc = _conv_step(xb, scb_ref, cwb_ref, cbb_ref, LRU_WIDTH)
    a, b = _lru_gates(xc, wa_ref, ba_ref, wx_ref, bx_ref, lam_ref)
    ob_ref[...] = a * hb_ref[...] + b
    ncb_ref[:, 0:2 * LRU_WIDTH] = scb_ref[:, LRU_WIDTH:3 * LRU_WIDTH]
    ncb_ref[:, 2 * LRU_WIDTH:3 * LRU_WIDTH] = xb
    u, v = _gelu_ln(proj_ref[:, o_uv:o_uv + 2 * SGU_WIDTH], lg_ref, lb_ref)
    vc_ref[...] = v
    oc_ref[...] = u * (w00_ref[...] * v + b0_ref[...])
    xbc = proj_ref[:, o_xbc:o_xbc + SSD_CONV_CH]
    xcd = _silu(_conv_step(xbc, scd_ref, cwd_ref, cbd_ref, SSD_CONV_CH))
    ncd_ref[:, 0:2 * SSD_CONV_CH] = scd_ref[:, SSD_CONV_CH:3 * SSD_CONV_CH]
    ncd_ref[:, 2 * SSD_CONV_CH:3 * SSD_CONV_CH] = xbc
    xs = xcd[:, :SSD_INNER]
    dt, a_dt = _ssd_dt_a(dtr_ref[...], dtb_ref, alog_ref)
    hrow = lax.broadcasted_iota(jnp.int32, (LANES, SSD_INNER), 0)
    hlane = lax.broadcasted_iota(jnp.int32, (LANES, SSD_INNER), 1)
    expand = (hlane // HEAD_DIM == hrow).astype(F32)
    dt_e = jnp.dot(dt, expand, preferred_element_type=F32, precision=HIGHEST)
    a_e = jnp.dot(a_dt, expand, preferred_element_type=F32, precision=HIGHEST)
    xs_ref[...] = xs
    xdt_ref[...] = xs * dt_e
    ea_ref[...] = jnp.exp(a_e)
    bc_ref[...] = xcd[:, SSD_INNER:]


def _dec_branches(proj, dtr, offs, scb, hb, scd, W, l):
    Bd = proj.shape[0]
    f = lambda n: jax.ShapeDtypeStruct((Bd, n), F32)
    whole = lambda a: pl.BlockSpec(a.shape, lambda i: (0,) * a.ndim)
    names = ("conv_b_w", "conv_b_b", "lru_wa_bd", "lru_ba", "lru_wx_bd", "lru_bx", "lru_lambda",
             "sgu_ln_g", "sgu_ln_b", "sgu_w00_tab", "sgu_b0_tab",
             "ssd_conv_w", "ssd_conv_b", "ssd_dt_bias", "ssd_a_log")
    acts = (proj, dtr, scb, hb, scd)
    widths = (LRU_WIDTH, 3 * LRU_WIDTH, SGU_WIDTH, SGU_WIDTH, 3 * SSD_CONV_CH, SSD_INNER, SSD_INNER,
              SSD_INNER, 2 * SSD_GROUPS * SSD_STATE)
    return pl.pallas_call(
        functools.partial(_dec_branches_body, offs=offs),
        grid=(1,),
        in_specs=[whole(a) for a in acts] + [_lspec(W[n], l) for n in names],
        out_specs=[pl.BlockSpec((Bd, n), lambda i: (0, 0)) for n in widths],
        out_shape=[f(n) for n in widths],
        compiler_params=_cp("arbitrary"), name="dec_branches",
    )(*acts, *[W[n] for n in names])


def _dec_ssd_body(xs_ref, xdt_ref, ea_ref, bc_ref, z_ref, st_ref, dsk_ref, ng_ref, od_ref, ns_ref):
    gn = SSD_GROUPS * SSD_STATE
    half = SSD_INNER // SSD_GROUPS
    ridx = lax.broadcasted_iota(jnp.int32, (LANES, SSD_INNER), 0)
    rows = jnp.where(ridx == 0, jnp.broadcast_to(xdt_ref[...], (LANES, SSD_INNER)),
                     jnp.where(ridx == 1, jnp.broadcast_to(ea_ref[...], (LANES, SSD_INNER)), 0.0))
    cols = rows.T
    xdt_c = cols[:, 0:1]
    ea_c = cols[:, 1:2]
    bc = bc_ref[...]
    Bm, Cm = bc[:, :gn], bc[:, gn:]
    rowi = lax.broadcasted_iota(jnp.int32, (SSD_INNER, SSD_STATE), 0)
    b_full = jnp.where(rowi < half, jnp.broadcast_to(Bm[:, :SSD_STATE], (SSD_INNER, SSD_STATE)),
                       jnp.broadcast_to(Bm[:, SSD_STATE:], (SSD_INNER, SSD_STATE)))
    new = ea_c * st_ref[...] + xdt_c * b_full
    ns_ref[...] = new
    cidx = lax.broadcasted_iota(jnp.int32, (SUBLANES, SSD_STATE), 0)
    c8 = jnp.where(cidx == 0, jnp.broadcast_to(Cm[:, :SSD_STATE], (SUBLANES, SSD_STATE)),
                   jnp.where(cidx == 1, jnp.broadcast_to(Cm[:, SSD_STATE:], (SUBLANES, SSD_STATE)), 0.0))
    y8 = lax.dot_general(c8, new, (((1,), (1,)), ((), ())), preferred_element_type=F32,
                         precision=HIGHEST)
    lane = lax.broadcasted_iota(jnp.int32, (1, SSD_INNER), 1)
    y = jnp.where(lane < half, y8[0:1], y8[1:2])
    od_ref[...] = _ssd_gate_norm(y, xs_ref[...], z_ref[...], dsk_ref, ng_ref)


def _dec_ssd(xs, xdt, ea, bc, z, state_all, W, l):
    Bd = xs.shape[0]
    row = lambda b: (b, 0, 0)
    r3 = lambda t: t.reshape(Bd, 1, t.shape[-1])
    st = state_all.reshape(state_all.shape[0], Bd, SSD_INNER, SSD_STATE)
    od, ns = pl.pallas_call(
        _dec_ssd_body, grid=(Bd,),
        in_specs=[pl.BlockSpec((None, 1, SSD_INNER), row)] * 3
                 + [pl.BlockSpec((None, 1, 2 * SSD_GROUPS * SSD_STATE), row),
                    pl.BlockSpec((None, 1, SSD_INNER), row),
                    pl.BlockSpec((None, None, SSD_INNER, SSD_STATE), lambda b: (l, b, 0, 0)),
                    _lspec(W["ssd_d_tab"], l), _lspec(W["ssd_norm_g"], l)],
        out_specs=[pl.BlockSpec((None, 1, SSD_INNER), row),
                   pl.BlockSpec((None, SSD_INNER, SSD_STATE), row)],
        out_shape=[jax.ShapeDtypeStruct((Bd, 1, SSD_INNER), F32),
                   jax.ShapeDtypeStruct((Bd, SSD_INNER, SSD_STATE), F32)],
        compiler_params=_cp("parallel"), name="dec_ssd",
    )(r3(xs), r3(xdt), r3(ea), r3(bc), r3(z), st, W["ssd_d_tab"], W["ssd_norm_g"])
    return od.reshape(Bd, SSD_INNER), ns.reshape(state_all.shape[1:])


def _merge_body(x_ref, o1_ref, l1_ref, o2_ref, l2_ref, o3_ref, l3_ref, ob_ref, oc_ref, od_ref,
                g1_ref, wg_ref, wpa_ref, wpb_ref, wpc_ref, wpd_ref, wo_ref, out_ref):
    x = x_ref[...]
    h = _rms(x, g1_ref[...]).astype(wg_ref.dtype)
    l1, l2, l3 = l1_ref[...], l2_ref[...], l3_ref[...]
    m = jnp.maximum(jnp.maximum(l1, l2), l3)
    e1, e2, e3 = jnp.exp(l1 - m), jnp.exp(l2 - m), jnp.exp(l3 - m)
    den = e1 + e2 + e3
    oa = (e1 / den) * o1_ref[...] + (e2 / den) * o2_ref[...] + (e3 / den) * o3_ref[...]
    merged = None
    for bi, (o, w_ref) in enumerate(((oa, wpa_ref), (ob_ref[...], wpb_ref),
                                     (oc_ref[...], wpc_ref), (od_ref[...], wpd_ref))):
        gate = _sigmoid(_mm(h, wg_ref[:, bi * D_MODEL:(bi + 1) * D_MODEL]))
        term = gate * _mm(o, w_ref[...])
        merged = term if merged is None else merged + term
    out_ref[...] = x + _mm(merged, wo_ref[...])


def _merge(x2d, att, ob, oc, od, W, l, tm):
    R = x2d.shape[0]
    row = lambda i: (i, 0)
    names = ("norm1_g", "w_gates", "w_pa", "w_pb", "w_pc", "w_pd", "w_o")
    return pl.pallas_call(
        _merge_body, grid=(R // tm,),
        in_specs=[pl.BlockSpec((tm, D_MODEL), row)] + [pl.BlockSpec((tm, A_OUT), row)] * 6
                 + [pl.BlockSpec((tm, LRU_WIDTH), row)] * 3 + [_lspec(W[n], l) for n in names],
        out_specs=pl.BlockSpec((tm, D_MODEL), row),
        out_shape=jax.ShapeDtypeStruct((R, D_MODEL), F32),
        compiler_params=_cp("parallel"), name="merge",
    )(x2d, *att, ob, oc, od, *[W[n] for n in names])


def _route(logits):
    lane = lax.broadcasted_iota(jnp.int32, logits.shape, 1)
    big = jnp.int32(LANES)
    ninf = -jnp.inf
    gl = jnp.where(lane < N_EXPERT_GROUPS, logits, ninf)
    gm = jnp.max(gl, axis=-1, keepdims=True)
    gsel = jnp.min(jnp.where(gl == gm, lane, big), axis=-1, keepdims=True)
    pg = 1.0 / jnp.sum(jnp.exp(gl - gm), axis=-1, keepdims=True)
    lo = N_EXPERT_GROUPS + EXPERTS_PER_GROUP * gsel
    el = jnp.where(jnp.logical_and(lane >= lo, lane < lo + EXPERTS_PER_GROUP), logits, ninf)
    t1 = jnp.max(el, axis=-1, keepdims=True)
    i1 = jnp.min(jnp.where(el == t1, lane, big), axis=-1, keepdims=True)
    el2 = jnp.where(lane == i1, ninf, el)
    t2 = jnp.max(el2, axis=-1, keepdims=True)
    i2 = jnp.min(jnp.where(el2 == t2, lane, big), axis=-1, keepdims=True)
    e2 = jnp.exp(t2 - t1)
    den = 1.0 + e2
    w1 = (1.0 / den) * pg
    w2 = (e2 / den) * pg
    return jnp.where(lane == i1, w1, 0.0) + jnp.where(lane == i2, w2, 0.0), gsel


def _moe_body(x_ref, g2_ref, wr_ref, br_ref, wg_ref, wu_ref, wd_ref, out_ref, h_ref, comb_ref, acc_ref):
    e = pl.program_id(1)

    @pl.when(e == 0)
    def _():
        h = _rms(x_ref[...], g2_ref[...])
        h_ref[...] = h.astype(h_ref.dtype)
        logits = jnp.dot(h, wr_ref[...], preferred_element_type=F32, precision=HIGHEST) + br_ref[...]
        comb_ref[...] = _route(logits)[0]
        acc_ref[...] = jnp.zeros_like(acc_ref)

    h = h_ref[...]
    lane = lax.broadcasted_iota(jnp.int32, comb_ref.shape, 1)
    c = jnp.sum(jnp.where(lane == N_EXPERT_GROUPS + e, comb_ref[...], 0.0), axis=-1, keepdims=True)
    hg = _mm(h, wg_ref[...])
    hu = _mm(h, wu_ref[...])
    act = _silu(hg) * hu * c
    acc_ref[...] += _mm(act, wd_ref[...])

    @pl.when(e == N_EXPERTS - 1)
    def _():
        out_ref[...] = x_ref[...] + acc_ref[...]


def _moe(x2d, W, l, tm):
    R = x2d.shape[0]
    row = lambda i, e: (i, 0)
    expert = lambda r, c: pl.BlockSpec((None, None, r, c), lambda i, e: (l, e, 0, 0))
    return pl.pallas_call(
        _moe_body, grid=(R // tm, N_EXPERTS),
        in_specs=[pl.BlockSpec((tm, D_MODEL), row), _lspec(W["norm2_g"], l), _lspec(W["w_router"], l),
                  _lspec(W["b_router"], l),
                  expert(D_MODEL, D_EXPERT), expert(D_MODEL, D_EXPERT), expert(D_EXPERT, D_MODEL)],
        out_specs=pl.BlockSpec((tm, D_MODEL), row),
        out_shape=jax.ShapeDtypeStruct((R, D_MODEL), F32),
        scratch_shapes=[pltpu.VMEM((tm, D_MODEL), W["moe_w_gate"].dtype), pltpu.VMEM((tm, LANES), F32),
                        pltpu.VMEM((tm, D_MODEL), F32)],
        compiler_params=_cp("parallel", "arbitrary"), name="moe",
    )(x2d, W["norm2_g"], W["w_router"], W["b_router"], W["moe_w_gate"], W["moe_w_up"],
      W["moe_w_down"])


GID_LANE = N_EXPERT_GROUPS + N_EXPERTS
SEG_ALIGN = 16
MOE_RB = TM_MOE // N_EXPERT_GROUPS + 64
MOE_TMP = TM_MOE + N_EXPERT_GROUPS * SEG_ALIGN + MOE_RB
PERM_BLK = 128
UNPERM_BLK = 256
assert MOE_TMP % PERM_BLK == 0 and TM_MOE % UNPERM_BLK == 0


def _route_body(x_ref, g2_ref, wr_ref, br_ref, comb_ref, cnt_ref):
    h = _rms(x_ref[...], g2_ref[...])
    logits = jnp.dot(h, wr_ref[...], preferred_element_type=F32, precision=HIGHEST) + br_ref[...]
    comb, gsel = _route(logits)
    lane = lax.broadcasted_iota(jnp.int32, comb.shape, 1)
    comb_ref[...] = jnp.where(lane == GID_LANE, gsel.astype(F32), comb)
    counts = jnp.sum((lane == gsel).astype(F32), axis=0, keepdims=True)
    cnt_ref[...] = jnp.broadcast_to(counts, cnt_ref.shape)


def _route_call(x2d, W, l, tm):
    R = x2d.shape[0]
    return pl.pallas_call(
        _route_body, grid=(R // tm,),
        in_specs=[pl.BlockSpec((tm, D_MODEL), lambda i: (i, 0)), _lspec(W["norm2_g"], l),
                  _lspec(W["w_router"], l), _lspec(W["b_router"], l)],
        out_specs=[pl.BlockSpec((tm, LANES), lambda i: (i, 0)),
                   pl.BlockSpec((None, SUBLANES, LANES), lambda i: (i, 0, 0))],
        out_shape=[jax.ShapeDtypeStruct((R, LANES), F32),
                   jax.ShapeDtypeStruct((R // tm, SUBLANES, LANES), F32)],
        compiler_params=_cp("parallel"), name="moe_route",
    )(x2d, W["norm2_g"], W["w_router"], W["b_router"])


def _moe_sorted_body(off_ref, nch_ref, x_ref, comb_ref, g2_ref, wg_ref, wu_ref, wd_ref, out_ref,
                     sh_ref, sc_ref, acc_ref, rcol_ref, lo_ref, *, TM, RB):
    i = pl.program_id(0)
    g = pl.program_id(1)
    TMP = sh_ref.shape[0]

    @pl.when(g == 0)
    def _():
        h2 = _rms(x_ref[...], g2_ref[...]).astype(BF16)
        comb = comb_ref[...]
        lane = lax.broadcasted_iota(jnp.int32, comb.shape, 1)
        gid = jnp.sum(jnp.where(lane == GID_LANE, comb, 0.0), axis=-1, keepdims=True)
        onehot = jnp.logical_and(lane < N_EXPERT_GROUPS, lane.astype(F32) == gid).astype(F32)
        trow = lax.broadcasted_iota(jnp.int32, onehot.shape, 0)
        run = onehot
        k = 1
        while k < TM:
            run = run + jnp.where(trow >= k, pltpu.roll(run, k, 0), 0.0)
            k *= 2
        prefix = run - onehot
        lane1 = lax.broadcasted_iota(jnp.int32, (1, LANES), 1)
        offv = jnp.zeros((1, LANES), F32)
        for gg in range(N_EXPERT_GROUPS):
            offv = jnp.where(lane1 == gg, off_ref[i * N_EXPERT_GROUPS + gg].astype(F32), offv)
        rank = jnp.sum(onehot * (offv + prefix), axis=-1, keepdims=True)
        rank_b = jnp.broadcast_to(rank, (TM, LANES))
        rcol_ref[...] = rank_b
        rank_row = rank_b.T[0:1]
        c1 = comb.astype(BF16)
        r1 = comb - c1.astype(F32)
        c2 = r1.astype(BF16)
        c3 = (r1 - c2.astype(F32)).astype(BF16)
        for blk in range(TMP // PERM_BLK):
            rows = slice(blk * PERM_BLK, (blk + 1) * PERM_BLK)
            srow = lax.broadcasted_iota(jnp.int32, (PERM_BLK, TM), 0) + blk * PERM_BLK
            perm = (srow.astype(F32) == rank_row).astype(BF16)
            sh_ref[rows, :] = jnp.dot(perm, h2, preferred_element_type=F32).astype(BF16)
            sc_ref[rows, :] = (jnp.dot(perm, c1, preferred_element_type=F32)
                               + jnp.dot(perm, c2, preferred_element_type=F32)
                               + jnp.dot(perm, c3, preferred_element_type=F32))
        acc_ref[...] = jnp.zeros_like(acc_ref)

    off = off_ref[i * N_EXPERT_GROUPS + g]
    lane_c = lax.broadcasted_iota(jnp.int32, (RB, LANES), 1)

    def chunk(j, carry):
        r0 = pl.multiple_of(off + j * RB, SEG_ALIGN)
        rows = sh_ref[pl.ds(r0, RB), :]
        cc = sc_ref[pl.ds(r0, RB), :]
        y = jnp.zeros((RB, D_MODEL), F32)
        for e in range(EXPERTS_PER_GROUP):
            ce = jnp.sum(jnp.where(lane_c == N_EXPERT_GROUPS + EXPERTS_PER_GROUP * g + e, cc, 0.0),
                         axis=-1, keepdims=True)
            hg = jnp.dot(rows, wg_ref[e], preferred_element_type=F32)
            hu = jnp.dot(rows, wu_ref[e], preferred_element_type=F32)
            y = y + _bdot(_silu(hg) * hu * ce, wd_ref[e])
        acc_ref[pl.ds(r0, RB), :] += y
        return carry

    lax.fori_loop(0, nch_ref[i * N_EXPERT_GROUPS + g], chunk, 0)

    @pl.when(g == N_EXPERT_GROUPS - 1)
    def _():
        for blk in range(TMP // PERM_BLK):
            rows = slice(blk * PERM_BLK, (blk + 1) * PERM_BLK)
            a = acc_ref[rows, :]
            hi = a.astype(BF16)
            sh_ref[rows, :] = hi
            lo_ref[rows, :] = (a - hi.astype(F32)).astype(BF16)
        scol = lax.broadcasted_iota(jnp.int32, (UNPERM_BLK, TMP), 1).astype(F32)
        for blk in range(TM // UNPERM_BLK):
            rows = slice(blk * UNPERM_BLK, (blk + 1) * UNPERM_BLK)
            unperm = (scol == rcol_ref[rows, 0:1]).astype(BF16)
            out_ref[rows, :] = x_ref[rows, :] + (jnp.dot(unperm, sh_ref[...], preferred_element_type=F32)
                                                 + jnp.dot(unperm, lo_ref[...], preferred_element_type=F32))


def _moe_sorted(x2d, W, l):
    R = x2d.shape[0]
    TM, RB, TMP = TM_MOE, MOE_RB, MOE_TMP
    nt = R // TM
    comb, counts = _route_call(x2d, W, l, TM)
    cnt = counts[:, 0, :N_EXPERT_GROUPS].astype(jnp.int32)
    padded = (cnt + SEG_ALIGN - 1) // SEG_ALIGN * SEG_ALIGN
    off = (jnp.cumsum(padded, axis=1) - padded).reshape(-1)
    nch = ((cnt + RB - 1) // RB).reshape(-1)
    row = lambda i, g, *_: (i, 0)
    const = lambda a: pl.BlockSpec((None,) + tuple(a.shape[1:]), lambda i, g, *_: (l,) + (0,) * (a.ndim - 1))
    grp = lambda r, c: pl.BlockSpec((None, None, EXPERTS_PER_GROUP, r, c), lambda i, g, *_: (l, g, 0, 0, 0))
    gview = lambda a: a.reshape(a.shape[0], N_EXPERT_GROUPS, EXPERTS_PER_GROUP, a.shape[2], a.shape[3])
    return pl.pallas_call(
        functools.partial(_moe_sorted_body, TM=TM, RB=RB),
        grid_spec=pltpu.PrefetchScalarGridSpec(
            num_scalar_prefetch=2, grid=(nt, N_EXPERT_GROUPS),
            in_specs=[pl.BlockSpec((TM, D_MODEL), row, pipeline_mode=pl.Buffered(1)),
                      pl.BlockSpec((TM, LANES), row), const(W["norm2_g"]),
                      grp(D_MODEL, D_EXPERT), grp(D_MODEL, D_EXPERT), grp(D_EXPERT, D_MODEL)],
            out_specs=pl.BlockSpec((TM, D_MODEL), row),
            scratch_shapes=[pltpu.VMEM((TMP, D_MODEL), BF16), pltpu.VMEM((TMP, LANES), F32),
                            pltpu.VMEM((TMP, D_MODEL), F32), pltpu.VMEM((TM, LANES), F32),
                            pltpu.VMEM((TMP, D_MODEL), BF16)]),
        out_shape=jax.ShapeDtypeStruct((R, D_MODEL), F32),
        compiler_params=_cp("parallel", "arbitrary"), name="moe_sorted",
    )(off, nch, x2d, comb, W["norm2_g"], gview(W["moe_w_gate"]), gview(W["moe_w_up"]), gview(W["moe_w_down"]))


def _rope_tables(pos):
    half = HEAD_DIM // 2
    inv = ROPE_THETA ** (-jnp.arange(half, dtype=F32) / half)
    ang = pos.astype(F32)[:, None] * inv[None]
    c, s = jnp.cos(ang), jnp.sin(ang)
    z = jnp.zeros_like(s)
    reps = LANES // HEAD_DIM
    return (jnp.tile(jnp.concatenate([c, c], 1), (1, reps)),
            jnp.tile(jnp.concatenate([-s, z], 1), (1, reps)),
            jnp.tile(jnp.concatenate([z, s], 1), (1, reps)))


def _block_diag(w):
    L, n, k, _ = w.shape
    eye = jnp.eye(n, dtype=w.dtype)
    return (eye[None, :, None, :, None] * w[:, :, :, None, :]).reshape(L, n * k, n * k)


def _small_params(p):
    vec = lambda t: t.reshape(t.shape[0], 1, -1)
    W = {}
    for n in ("norm1_g", "norm2_g", "conv_b_b", "lru_ba", "lru_bx", "lru_lambda", "sgu_ln_g", "sgu_ln_b",
              "ssd_conv_b", "ssd_norm_g"):
        W[n] = vec(p[n])
    reps = A_WIDTH // HEAD_DIM
    W["q_norm_g"] = vec(jnp.tile(p["q_norm_g"], (1, reps)))
    W["k_norm_g"] = vec(jnp.tile(p["k_norm_g"], (1, reps)))
    W["conv_b_w"] = p["conv_b_w"]
    W["ssd_conv_w"] = p["ssd_conv_w"]
    W["sgu_w"] = p["sgu_w"]
    gw = SGU_WIDTH // SGU_GROUPS
    W["sgu_b_tab"] = jnp.repeat(jnp.swapaxes(p["sgu_b"], 1, 2), gw, axis=2)
    W["sgu_w00_tab"] = vec(jnp.repeat(p["sgu_w"][:, :, 0, 0], gw, axis=1))
    W["sgu_b0_tab"] = vec(jnp.repeat(p["sgu_b"][:, :, 0], gw, axis=1))
    padh = lambda t: vec(jnp.pad(t, ((0, 0), (0, LANES - SSD_HEADS))))
    W["ssd_dt_bias"], W["ssd_a_log"] = padh(p["ssd_dt_bias"]), padh(p["ssd_a_log"])
    W["ssd_d_tab"] = vec(jnp.repeat(p["ssd_d"], HEAD_DIM, axis=1))
    we = jnp.transpose(p["router_exp_w"], (0, 2, 1, 3)).reshape(-1, D_MODEL, N_EXPERTS)
    wr = jnp.concatenate([p["router_group_w"], we], axis=2)
    W["w_router"] = jnp.pad(wr, ((0, 0), (0, 0), (0, LANES - wr.shape[2])))
    br = jnp.concatenate([p["router_group_b"], p["router_exp_b"].reshape(-1, N_EXPERTS)], axis=1)
    W["b_router"] = vec(jnp.pad(br, ((0, 0), (0, LANES - br.shape[1]))))
    return W


def _matrix_params(p, wdt):
    W = {}
    w_in = p["w_in"].astype(wdt)
    W["w_in"] = w_in
    o_xbc, o_dt, o_g = IN_OFFS[6], IN_OFFS[7], IN_OFFS[8]
    W["w_xbcdt"] = jnp.concatenate(
        [w_in[:, :, o_xbc:o_dt], jnp.pad(w_in[:, :, o_dt:o_g], ((0, 0), (0, 0), (0, LANES - SSD_HEADS)))], axis=2)
    W["w_gates"] = w_in[:, :, o_g:]
    reps = A_WIDTH // HEAD_DIM
    W["gbd"] = _block_diag(jnp.ones((1, reps, HEAD_DIM, HEAD_DIM), wdt))[0]
    W["lru_wa_bd"] = _block_diag(p["lru_wa"]).astype(wdt)
    W["lru_wx_bd"] = _block_diag(p["lru_wx"]).astype(wdt)
    for n in ("w_pa", "w_pb", "w_pc", "w_pd", "w_o", "moe_w_gate", "moe_w_up", "moe_w_down"):
        W[n] = p[n].astype(wdt)
    return W


def _kv_rows(k, v, gi):
    sl = slice(gi * A_OUT, (gi + 1) * A_OUT)
    shp = k.shape[:-1] + (A_HPG, HEAD_DIM)
    return jnp.stack([k[..., sl].reshape(shp), v[..., sl].reshape(shp)], axis=-3)


def _prompt_layer(x, W, l, tabs):
    B, S, _ = x.shape
    x2d = x.reshape(B * S, D_MODEL)
    *qkv, k, v = _qkv(x2d, W, l, tabs, TM_QKV, split=True)
    att = _attn_prompt(qkv, B, S)
    ob, tail_b = _lru_prompt(x, W, l, T_SEQ)
    oc = _sgu_prompt(x2d, W, l, T_SEQ)
    od, tail_d, fin = _ssd_prompt(x, W, l, T_SEQ)
    x2d = _merge(x2d, att, ob.reshape(B * S, -1), oc, od.reshape(B * S, -1), W, l, TM_MERGE)
    x2d = _moe_sorted(x2d, W, l)
    k3, v3 = k.reshape(B, S, -1), v.reshape(B, S, -1)
    kvs = [_kv_rows(k3[:, S - min(w, S):], v3[:, S - min(w, S):], gi) for gi, (w, _) in enumerate(A_GROUPS)]
    nb = CONV_W - 1
    states = (tail_b[:, SUBLANES - nb:], ob[:, -1], tail_d[:, SUBLANES - nb:],
              fin.reshape(B, SSD_HEADS, HEAD_DIM, SSD_STATE))
    return x2d.reshape(B, S, D_MODEL), kvs, states


def _decode_layer(x2d, W, l, tabs, caches, scb, hb, scd, ssm_all):
    Bd = x2d.shape[0]
    q, k, v = _qkv(x2d, W, l, tabs, Bd, split=False)
    att = _attn_step(q, k, v, caches, l)
    col0, ncols = IN_OFFS[3], IN_OFFS[7] - IN_OFFS[3]
    proj = _proj(x2d, W, l, "w_in", col0, ncols, 256)
    dtr = _proj(x2d, W, l, "w_xbcdt", SSD_CONV_CH, LANES, LANES)
    offs = (0, IN_OFFS[4] - col0, IN_OFFS[6] - col0)
    nb = CONV_W - 1
    ob, ncb, oc, vc, ncd, xs, xdt, ea, bc = _dec_branches(
        proj, dtr, offs, scb.reshape(Bd, nb * LRU_WIDTH), hb, scd.reshape(Bd, nb * SSD_CONV_CH), W, l)
    z = proj[:, IN_OFFS[5] - col0:IN_OFFS[5] - col0 + SSD_INNER]
    od, nssm = _dec_ssd(xs, xdt, ea, bc, z, ssm_all, W, l)
    x2d = _merge(x2d, att, ob, oc, od, W, l, Bd)
    x2d = _moe(x2d, W, l, Bd)
    kvs = [_kv_rows(k, v, gi)[:, None] for gi in range(len(A_GROUPS))]
    states = (ncb.reshape(Bd, nb, LRU_WIDTH), ob, vc[:, None], ncd.reshape(Bd, nb, SSD_CONV_CH), nssm)
    return x2d, kvs, states


def kernel(x_prompt, x_sample, cache_kv_a1, cache_kv_a2, cache_kv_a3, state_conv_b, state_h_b, state_conv_d, state_ssm_d, norm1_g, w_in, q_norm_g, k_norm_g, conv_b_w, conv_b_b, lru_wa, lru_ba, lru_wx, lru_bx, lru_lambda, sgu_ln_g, sgu_ln_b, sgu_w, sgu_b, ssd_conv_w, ssd_conv_b, ssd_dt_bias, ssd_a_log, ssd_d, ssd_norm_g, w_pa, w_pb, w_pc, w_pd, w_o, norm2_g, router_group_w, router_group_b, router_exp_w, router_exp_b, moe_w_gate, moe_w_up, moe_w_down):
    p = dict(norm1_g=norm1_g, w_in=w_in, q_norm_g=q_norm_g, k_norm_g=k_norm_g, conv_b_w=conv_b_w,
             conv_b_b=conv_b_b, lru_wa=lru_wa, lru_ba=lru_ba, lru_wx=lru_wx, lru_bx=lru_bx,
             lru_lambda=lru_lambda, sgu_ln_g=sgu_ln_g, sgu_ln_b=sgu_ln_b, sgu_w=sgu_w, sgu_b=sgu_b,
             ssd_conv_w=ssd_conv_w, ssd_conv_b=ssd_conv_b, ssd_dt_bias=ssd_dt_bias, ssd_a_log=ssd_a_log,
             ssd_d=ssd_d, ssd_norm_g=ssd_norm_g, w_pa=w_pa, w_pb=w_pb, w_pc=w_pc, w_pd=w_pd, w_o=w_o,
             norm2_g=norm2_g, router_group_w=router_group_w, router_group_b=router_group_b,
             router_exp_w=router_exp_w, router_exp_b=router_exp_b, moe_w_gate=moe_w_gate,
             moe_w_up=moe_w_up, moe_w_down=moe_w_down)
    B, S, _ = x_prompt.shape
    Bd = x_sample.shape[0]
    depth = w_in.shape[0]
    small = _small_params(p)
    Wp = dict(small, **_matrix_params(p, BF16))
    Wd = dict(small, **_matrix_params(p, F32))
    tabs_p = _rope_tables(jnp.arange(S))
    tabs_s = tuple(jnp.broadcast_to(t, (Bd, LANES)) for t in _rope_tables(PAST_LEN + jnp.arange(1)))
    caches = (cache_kv_a1, cache_kv_a2, cache_kv_a3)
    yp, ys = x_prompt, x_sample.reshape(Bd, D_MODEL)
    P = [[] for _ in range(7)]
    Sx = [[] for _ in range(8)]
    for l in range(depth):
        yp, kvs, st = _prompt_layer(yp, Wp, l, tabs_p)
        for dst, val in zip(P, list(kvs) + list(st)):
            dst.append(val)
        ys, kvs, st = _decode_layer(ys, Wd, l, tabs_s, caches, state_conv_b[l], state_h_b[l],
                                    state_conv_d[l], state_ssm_d)
        for dst, val in zip(Sx, list(kvs) + list(st)):
            dst.append(val)
    st = jnp.stack
    return (yp, ys.reshape(Bd, 1, D_MODEL)) + tuple(st(t) for t in P) + tuple(st(t) for t in Sx)
```

```python
import functools

import jax
import jax.numpy as jnp
from jax import lax
from jax.experimental import pallas as pl
from jax.experimental.pallas import tpu as pltpu

F32 = jnp.float32
BF16 = jnp.bfloat16
HIGHEST = lax.Precision.HIGHEST

D_MODEL = 1024
DEPTH = 4
PAST_LEN = 8192
EPS = 1e-6
HEAD_DIM = 64
A_HPG = 4
A_GROUPS = ((128, 1), (512, 4), (2048, 16))
A_WIDTH = 768
A_OUT = 256
BLK = 128
ROPE_THETA = 10000.0
LRU_WIDTH = 768
LRU_C = 8.0
CONV_W = 4
SGU_WIDTH = 768
SGU_GROUPS = 4
SGU_CHUNK = 128
SSD_INNER = 768
SSD_HEADS = 12
SSD_GROUPS = 2
SSD_STATE = 128
SSD_CHUNK = 128
SSD_CONV_CH = 1280
N_BRANCH = 4
N_EXPERT_GROUPS = 4
EXPERTS_PER_GROUP = 4
N_EXPERTS = 16
D_EXPERT = 512
IN_SIZES = (768, 768, 768, 768, 1536, 768, 1280, 12, 4096)
IN_OFFS = tuple(sum(IN_SIZES[:i]) for i in range(len(IN_SIZES)))

LANES = 128
SUBLANES = 8
VMEM_LIMIT = 56 * 1024 * 1024

TM_QKV = 512
T_SEQ = 256
T_SGU = 512
TM_MERGE = 512
TM_MOE = 1024


def _cp(*sem):
    return pltpu.CompilerParams(dimension_semantics=sem, vmem_limit_bytes=VMEM_LIMIT)


def _const_spec(shape):
    nd = len(shape)
    return pl.BlockSpec(shape, lambda *_: (0,) * nd, pipeline_mode=pl.Buffered(1))


def _lspec(a, l, block=None, idx=None):
    shape = tuple(a.shape[1:]) if block is None else tuple(block)
    tail = (0,) * len(shape) if idx is None else tuple(idx)
    return pl.BlockSpec((None,) + shape, lambda *_: (l,) + tail, pipeline_mode=pl.Buffered(1))


def _rms(x, g):
    return x * lax.rsqrt(jnp.mean(x * x, axis=-1, keepdims=True) + EPS) * g


def _mm(a, w):
    if w.dtype == F32:
        return jnp.dot(a.astype(F32), w, preferred_element_type=F32, precision=HIGHEST)
    return jnp.dot(a.astype(BF16), w, preferred_element_type=F32)


def _bdot(a, b):
    return jnp.dot(a.astype(BF16), b.astype(BF16), preferred_element_type=F32)


def _bdot_nt(a, b):
    return lax.dot_general(a.astype(BF16), b.astype(BF16), (((1,), (1,)), ((), ())),
                           preferred_element_type=F32)


def _bdot_tn(a, b):
    return lax.dot_general(a.astype(BF16), b.astype(BF16), (((0,), (0,)), ((), ())),
                           preferred_element_type=F32)


def _sigmoid(x):
    return jax.nn.sigmoid(x)


def _silu(x):
    return x * jax.nn.sigmoid(x)


def _neg_expm1(x):
    u = jnp.exp(x)
    safe = jnp.logical_and(u != 1.0, u > 0.0)
    lu = jnp.log(jnp.where(safe, u, 0.5))
    em1 = jnp.where(u == 1.0, x, jnp.where(u > 0.0, (u - 1.0) * x / lu, -1.0))
    return -em1


def _lanes6(t):
    return jnp.concatenate([t] * 6, axis=1)


def _store_strided_view(t, out_ref, slab_ref, dil):
    if dil == 1:
        out_ref[...] = t.astype(out_ref.dtype)
        return
    T = t.shape[0]
    halves = A_OUT // LANES
    for s in range(halves):
        slab_ref[s] = t[:, s * LANES:(s + 1) * LANES]
    for r in range(dil):
        for s in range(halves):
            c0 = r * A_OUT + s * LANES
            out_ref[:, c0:c0 + LANES] = slab_ref[s, pl.ds(r, T // dil, stride=dil), :].astype(out_ref.dtype)


def _load_strided_view(view_ref, slab_ref, dil):
    if dil == 1:
        return view_ref[...]
    n = view_ref.shape[0]
    halves = A_OUT // LANES
    for r in range(dil):
        for s in range(halves):
            c0 = r * A_OUT + s * LANES
            slab_ref[s, pl.ds(r, n, stride=dil), :] = view_ref[:, c0:c0 + LANES]
    return jnp.concatenate([slab_ref[s] for s in range(halves)], axis=1)


def _qkv_body(x_ref, g1_ref, wq_ref, wk_ref, wv_ref, qg_ref, kg_ref, gbd_ref,
              cos_ref, sa_ref, sb_ref, *outs, split):
    h = _rms(x_ref[...], g1_ref[...]).astype(wq_ref.dtype)
    cos = _lanes6(cos_ref[...])
    sa = _lanes6(sa_ref[...])
    sb = _lanes6(sb_ref[...])
    gbd = gbd_ref[...]

    def normed_rotated(w_ref, hg_ref):
        t = _mm(h, w_ref[...])
        ss = _mm(t * t, gbd)
        tn = t * lax.rsqrt(ss * (1.0 / HEAD_DIM) + EPS) * hg_ref[...]
        return (tn * cos + pltpu.roll(tn, A_WIDTH - HEAD_DIM // 2, 1) * sa
                + pltpu.roll(tn, HEAD_DIM // 2, 1) * sb)

    q = normed_rotated(wq_ref, qg_ref) * (HEAD_DIM ** -0.5)
    k = normed_rotated(wk_ref, kg_ref)
    v = _mm(h, wv_ref[...])
    if split:
        *outs, slab_ref = outs
        for gi, (_, dil) in enumerate(A_GROUPS):
            sl = slice(gi * A_OUT, (gi + 1) * A_OUT)
            for j, t in enumerate((q, k, v)):
                _store_strided_view(t[:, sl], outs[3 * j + gi], slab_ref, dil)
        outs[9][...] = k
        outs[10][...] = v
    else:
        outs[0][...] = q
        outs[1][...] = k
        outs[2][...] = v


def _qkv(x2d, W, l, tabs, tm, split):
    R = x2d.shape[0]
    cos, sa, sb = tabs
    npos = cos.shape[0] // tm
    row = lambda i: (i, 0)
    pos = lambda i: (i % npos, 0)
    wcol = lambda j: _lspec(W["w_in"], l, (D_MODEL, A_WIDTH), (0, j))
    full = pl.BlockSpec((tm, A_WIDTH), row)
    scratch = []
    if split:
        dils = [dil for _, dil in A_GROUPS] * 3
        out_specs = [pl.BlockSpec((tm // d, d * A_OUT), row) for d in dils] + [full, full]
        out_shape = ([jax.ShapeDtypeStruct((R // d, d * A_OUT), BF16) for d in dils]
                     + [jax.ShapeDtypeStruct((R, A_WIDTH), F32)] * 2)
        scratch = [pltpu.VMEM((A_OUT // LANES, tm, LANES), F32)]
    else:
        out_specs = [full] * 3
        out_shape = [jax.ShapeDtypeStruct((R, A_WIDTH), F32)] * 3
    return pl.pallas_call(
        functools.partial(_qkv_body, split=split),
        grid=(R // tm,), scratch_shapes=scratch,
        in_specs=[pl.BlockSpec((tm, D_MODEL), row), _lspec(W["norm1_g"], l),
                  wcol(0), wcol(1), wcol(2),
                  _lspec(W["q_norm_g"], l), _lspec(W["k_norm_g"], l),
                  _const_spec((A_WIDTH, A_WIDTH)),
                  pl.BlockSpec((tm, LANES), pos), pl.BlockSpec((tm, LANES), pos),
                  pl.BlockSpec((tm, LANES), pos)],
        out_specs=out_specs, out_shape=out_shape,
        compiler_params=_cp("parallel"), name="qkv_proj",
    )(x2d, W["norm1_g"], W["w_in"], W["w_in"], W["w_in"], W["q_norm_g"], W["k_norm_g"],
      W["gbd"], cos, sa, sb)


def _attn_group(q_ref, kp_ref, kc_ref, vp_ref, vc_ref, o_ref, l_ref, mask, lane_head):
    q = q_ref[...]
    zero = jnp.zeros_like(q)
    qs = jnp.concatenate([jnp.where(lane_head == hh, q, zero) for hh in range(A_HPG)], axis=0)
    kk = jnp.concatenate([kp_ref[...], kc_ref[...]], axis=0)
    vv = jnp.concatenate([vp_ref[...], vc_ref[...]], axis=0)
    s = jnp.where(mask, _bdot_nt(qs, kk), -jnp.inf)
    m = jnp.max(s, axis=-1, keepdims=True)
    e = jnp.exp(s - m)
    den = jnp.sum(e, axis=-1, keepdims=True)
    o4 = _bdot(e * (1.0 / den), vv)
    lse = m + jnp.log(den)
    o = o4[(A_HPG - 1) * BLK:]
    lo = jnp.broadcast_to(lse[(A_HPG - 1) * BLK:], (BLK, A_OUT))
    for hh in range(A_HPG - 2, -1, -1):
        sel = lane_head == hh
        o = jnp.where(sel, o4[hh * BLK:(hh + 1) * BLK], o)
        lo = jnp.where(sel, jnp.broadcast_to(lse[hh * BLK:(hh + 1) * BLK], (BLK, A_OUT)), lo)
    o_ref[...] = o
    l_ref[...] = lo


def _attn_prompt_body(*refs):
    i = pl.program_id(1)
    ins, outs = refs[:15], refs[15:]
    rows = A_HPG * BLK
    qi = lax.broadcasted_iota(jnp.int32, (rows, 2 * BLK), 0) % BLK
    kj = lax.broadcasted_iota(jnp.int32, (rows, 2 * BLK), 1)
    mask_cur = jnp.logical_and(kj >= BLK, kj - BLK <= qi)
    mask_prev = jnp.logical_and(kj < BLK, kj >= qi)
    lane_head = lax.broadcasted_iota(jnp.int32, (BLK, A_OUT), 1) // HEAD_DIM
    for gi, (_, dil) in enumerate(A_GROUPS):
        q_ref, kp_ref, kc_ref, vp_ref, vc_ref = ins[5 * gi:5 * gi + 5]
        o_ref, l_ref = outs[2 * gi:2 * gi + 2]
        mask = jnp.logical_or(mask_cur, jnp.logical_and(mask_prev, (i // dil) > 0))
        _attn_group(q_ref, kp_ref, kc_ref, vp_ref, vc_ref, o_ref, l_ref, mask, lane_head)


def _attn_prompt(qkv, B, S):
    nblk = S // BLK
    args, in_specs, out_specs, out_shape = [], [], [], []
    for gi, (_, dil) in enumerate(A_GROUPS):
        assert S % (dil * BLK) == 0
        rows = S // dil
        cur = lambda b, i, dil=dil: (b, i // dil, i % dil)
        prev = lambda b, i, dil=dil: (b, jnp.maximum(i // dil - 1, 0), i % dil)
        blk = (None, BLK, A_OUT)
        view = lambda t: t.reshape(B, rows, dil * A_OUT)
        qv, kv, vv = view(qkv[gi]), view(qkv[3 + gi]), view(qkv[6 + gi])
        args += [qv, kv, kv, vv, vv]
        in_specs += [pl.BlockSpec(blk, cur), pl.BlockSpec(blk, prev), pl.BlockSpec(blk, cur),
                     pl.BlockSpec(blk, prev), pl.BlockSpec(blk, cur)]
        out_specs += [pl.BlockSpec(blk, cur)] * 2
        out_shape += [jax.ShapeDtypeStruct((B, rows, dil * A_OUT), F32)] * 2
    res = pl.pallas_call(
        _attn_prompt_body, grid=(B, nblk), in_specs=in_specs, out_specs=out_specs,
        out_shape=out_shape, compiler_params=_cp("parallel", "parallel"), name="attn_prompt",
    )(*args)
    return [r.reshape(-1, r.shape[-1]) for r in res]


def _attn_step_body(q_ref, k_ref, v_ref, c1_ref, c2_ref, c3_ref, *outs):
    q_cols = jnp.broadcast_to(q_ref[...], (LANES, A_WIDTH)).T
    lane_head = lax.broadcasted_iota(jnp.int32, (1, A_OUT), 1) // HEAD_DIM
    for gi, (c_ref, (_, dil)) in enumerate(zip((c1_ref, c2_ref, c3_ref), A_GROUPS)):
        o_ref, l_ref = outs[2 * gi:2 * gi + 2]
        W = c_ref.shape[-1]
        pos = lax.broadcasted_iota(jnp.int32, (1, W), 1)
        valid = (pos % dil) == 0
        o_cols, w_new, lse = [], 0.0, 0.0
        for hh in range(A_HPG):
            sl = slice(gi * A_OUT + hh * HEAD_DIM, gi * A_OUT + (hh + 1) * HEAD_DIM)
            s_c = jnp.sum(c_ref[0, hh] * q_cols[sl, 0:1], axis=0, keepdims=True)
            s_c = jnp.where(valid, s_c, -jnp.inf)
            s_n = jnp.sum(q_ref[:, sl] * k_ref[:, sl], axis=-1, keepdims=True)
            m = jnp.maximum(jnp.max(s_c, axis=-1, keepdims=True), s_n)
            e_c = jnp.exp(s_c - m)
            e_n = jnp.exp(s_n - m)
            den = jnp.sum(e_c, axis=-1, keepdims=True) + e_n
            o_cols.append(jnp.sum(c_ref[1, hh] * (e_c / den), axis=-1, keepdims=True))
            w_new = jnp.where(lane_head == hh, e_n / den, w_new)
            lse = jnp.where(lane_head == hh, m + jnp.log(den), lse)
        o_col = jnp.concatenate(o_cols, axis=0)
        o_row = jnp.broadcast_to(o_col, (A_OUT, LANES)).T[0:1]
        o_ref[...] = o_row + w_new * v_ref[:, gi * A_OUT:(gi + 1) * A_OUT]
        l_ref[...] = lse


def _attn_step(q, k, v, caches, l):
    Bd = q.shape[0]
    row = lambda b: (b, 0, 0)
    args = [q.reshape(Bd, 1, A_WIDTH), k.reshape(Bd, 1, A_WIDTH), v.reshape(Bd, 1, A_WIDTH)]
    in_specs = [pl.BlockSpec((None, 1, A_WIDTH), row)] * 3
    for (window, dil), c in zip(A_GROUPS, caches):
        assert c.shape[2] == window and window % dil == 0
        args.append(jnp.transpose(c, (0, 1, 3, 4, 5, 2)))
        in_specs.append(pl.BlockSpec((None, None, 2, A_HPG, HEAD_DIM, window), lambda b: (l, b, 0, 0, 0, 0)))
    res = pl.pallas_call(
        _attn_step_body, grid=(Bd,), in_specs=in_specs,
        out_specs=[pl.BlockSpec((None, 1, A_OUT), row)] * 6,
        out_shape=[jax.ShapeDtypeStruct((Bd, 1, A_OUT), F32)] * 6,
        compiler_params=_cp("parallel"), name="attn_step",
    )(*args)
    return [r.reshape(Bd, A_OUT) for r in res]


def _lru_gates(xc, wa_ref, ba_ref, wx_ref, bx_ref, lam_ref):
    r = _sigmoid(_mm(xc, wa_ref[...]) + ba_ref[...])
    i = _sigmoid(_mm(xc, wx_ref[...]) + bx_ref[...])
    log_a = -LRU_C * r * jax.nn.softplus(-lam_ref[...])
    a = jnp.exp(log_a)
    b = jnp.sqrt(_neg_expm1(2.0 * log_a)) * (i * xc)
    return a, b


def _scan_rows(a, b, h0):
    T = a.shape[0]
    row = lax.broadcasted_iota(jnp.int32, a.shape, 0) % SUBLANES
    k = 1
    while k < SUBLANES:
        keep = row >= k
        a_s = jnp.where(keep, pltpu.roll(a, k, 0), 1.0)
        b_s = jnp.where(keep, pltpu.roll(b, k, 0), 0.0)
        b = a * b_s + b
        a = a * a_s
        k *= 2
    out, carry = [], h0
    for j in range(T // SUBLANES):
        rows = slice(j * SUBLANES, (j + 1) * SUBLANES)
        hj = b[rows] + a[rows] * carry
        out.append(hj)
        carry = hj[SUBLANES - 1:SUBLANES]
    return jnp.concatenate(out, axis=0)


def _lru_body(x_ref, g1_ref, w_ref, cw_ref, cb_ref, wa_ref, ba_ref, wx_ref, bx_ref, lam_ref,
              ob_ref, tail_ref, ext_ref, hc_ref, *, T):
    @pl.when(pl.program_id(1) == 0)
    def _():
        ext_ref[0:SUBLANES, :] = jnp.zeros((SUBLANES, LRU_WIDTH), F32)
        hc_ref[...] = jnp.zeros_like(hc_ref)

    h = _rms(x_ref[...], g1_ref[...])
    xb = _mm(h, w_ref[...])
    ext_ref[SUBLANES:SUBLANES + T, :] = xb
    xc = cb_ref[...]
    for kk in range(CONV_W - 1):
        xc = xc + ext_ref[pl.ds(SUBLANES - (CONV_W - 1) + kk, T), :] * cw_ref[kk:kk + 1, :]
    xc = xc + xb * cw_ref[CONV_W - 1:CONV_W, :]
    tail = xb[T - SUBLANES:T]
    ext_ref[0:SUBLANES, :] = tail
    tail_ref[...] = tail

    a, b = _lru_gates(xc, wa_ref, ba_ref, wx_ref, bx_ref, lam_ref)
    hfull = _scan_rows(a, b, hc_ref[0:1, :])
    ob_ref[...] = hfull
    hc_ref[...] = jnp.broadcast_to(hfull[T - 1:T], hc_ref.shape)


def _lru_prompt(x, W, l, T):
    B, S, _ = x.shape
    tile = lambda b, s: (b, s, 0)
    names = ("norm1_g", None, "conv_b_w", "conv_b_b", "lru_wa_bd", "lru_ba", "lru_wx_bd", "lru_bx", "lru_lambda")
    specs = [_lspec(W[n], l) if n else _lspec(W["w_in"], l, (D_MODEL, LRU_WIDTH), (0, IN_OFFS[3] // LRU_WIDTH))
             for n in names]
    return pl.pallas_call(
        functools.partial(_lru_body, T=T),
        grid=(B, S // T),
        in_specs=[pl.BlockSpec((None, T, D_MODEL), tile)] + specs,
        out_specs=[pl.BlockSpec((None, T, LRU_WIDTH), tile),
                   pl.BlockSpec((None, SUBLANES, LRU_WIDTH), lambda b, s: (b, 0, 0))],
        out_shape=[jax.ShapeDtypeStruct((B, S, LRU_WIDTH), F32),
                   jax.ShapeDtypeStruct((B, SUBLANES, LRU_WIDTH), F32)],
        scratch_shapes=[pltpu.VMEM((T + SUBLANES, LRU_WIDTH), F32),
                        pltpu.VMEM((SUBLANES, LRU_WIDTH), F32)],
        compiler_params=_cp("parallel", "arbitrary"), name="lru_prompt",
    )(x, *[W[n] if n else W["w_in"] for n in names])


def _gelu_ln(uv, lg_ref, lb_ref):
    uv = jax.nn.gelu(uv)
    u = uv[:, :SGU_WIDTH]
    v = uv[:, SGU_WIDTH:]
    mu = jnp.mean(v, axis=-1, keepdims=True)
    var = jnp.mean(jnp.square(v - mu), axis=-1, keepdims=True)
    v = (v - mu) * lax.rsqrt(var + EPS) * lg_ref[...] + lb_ref[...]
    return u, v


def _sgu_body(x_ref, g1_ref, w_ref, lg_ref, lb_ref, ws_ref, bs_ref, oc_ref, *, T):
    h = _rms(x_ref[...], g1_ref[...])
    u, v = _gelu_ln(_mm(h, w_ref[...]), lg_ref, lb_ref)
    qi = lax.broadcasted_iota(jnp.int32, (SGU_CHUNK, SGU_CHUNK), 0)
    kj = lax.broadcasted_iota(jnp.int32, (SGU_CHUNK, SGU_CHUNK), 1)
    tril = (kj <= qi).astype(F32)
    lane = lax.broadcasted_iota(jnp.int32, (SGU_CHUNK, SGU_WIDTH), 1)
    gw = SGU_WIDTH // SGU_GROUPS
    wms = [(ws_ref[g] * tril).astype(BF16) for g in range(SGU_GROUPS)]
    for c in range(T // SGU_CHUNK):
        rows = slice(c * SGU_CHUNK, (c + 1) * SGU_CHUNK)
        vc = v[rows].astype(BF16)
        mixed = jnp.dot(wms[SGU_GROUPS - 1], vc, preferred_element_type=F32)
        for g in range(SGU_GROUPS - 2, -1, -1):
            mixed = jnp.where(lane < (g + 1) * gw, jnp.dot(wms[g], vc, preferred_element_type=F32), mixed)
        oc_ref[rows, :] = u[rows] * (mixed + bs_ref[...])


def _sgu_prompt(x2d, W, l, T):
    R = x2d.shape[0]
    row = lambda i: (i, 0)
    return pl.pallas_call(
        functools.partial(_sgu_body, T=T),
        grid=(R // T,),
        in_specs=[pl.BlockSpec((T, D_MODEL), row), _lspec(W["norm1_g"], l),
                  _lspec(W["w_in"], l, (D_MODEL, 2 * SGU_WIDTH), (0, IN_OFFS[4] // (2 * SGU_WIDTH))),
                  _lspec(W["sgu_ln_g"], l), _lspec(W["sgu_ln_b"], l), _lspec(W["sgu_w"], l),
                  _lspec(W["sgu_b_tab"], l)],
        out_specs=pl.BlockSpec((T, SGU_WIDTH), row),
        out_shape=jax.ShapeDtypeStruct((R, SGU_WIDTH), F32),
        compiler_params=_cp("parallel"), name="sgu_prompt",
    )(x2d, W["norm1_g"], W["w_in"], W["sgu_ln_g"], W["sgu_ln_b"], W["sgu_w"], W["sgu_b_tab"])


def _ssd_dt_a(dtr, dtb_ref, alog_ref):
    lane = lax.broadcasted_iota(jnp.int32, (1, LANES), 1)
    dt = jax.nn.softplus(dtr + dtb_ref[...])
    A = jnp.where(lane < SSD_HEADS, -jnp.exp(alog_ref[...]), 0.0)
    return dt, A * dt


def _ssd_gate_norm(y, xs, z, dsk_ref, ng_ref):
    y = y + dsk_ref[...] * xs
    y = y * _silu(z)
    gw = SSD_INNER // SSD_GROUPS
    parts = []
    for g in range(SSD_GROUPS):
        yg = y[:, g * gw:(g + 1) * gw]
        parts.append(yg * lax.rsqrt(jnp.mean(yg * yg, axis=-1, keepdims=True) + EPS))
    return jnp.concatenate(parts, axis=1) * ng_ref[...]


def _ssd_chunk(xs, dt, a, Bm, Cm, st_ref, y_ref, row0):
    Q = SSD_CHUNK
    qi = lax.broadcasted_iota(jnp.int32, (Q, Q), 0)
    kj = lax.broadcasted_iota(jnp.int32, (Q, Q), 1)
    tril = kj <= qi
    cs = jnp.dot(tril.astype(F32), a, preferred_element_type=F32, precision=HIGHEST)
    cs_t = cs.T
    ecs = jnp.exp(cs)
    cs_last = cs[Q - 1:Q, :]
    to_end = jnp.exp(cs_last - cs)
    e_last = jnp.exp(cs_last)
    hpg = SSD_HEADS // SSD_GROUPS
    for g in range(SSD_GROUPS):
        Cg = Cm[:, g * SSD_STATE:(g + 1) * SSD_STATE].astype(BF16)
        Bg = Bm[:, g * SSD_STATE:(g + 1) * SSD_STATE].astype(BF16)
        G = _bdot_nt(Cg, Bg)
        for hh in range(hpg):
            hd = g * hpg + hh
            sl = slice(hd * HEAD_DIM, (hd + 1) * HEAD_DIM)
            Lh = jnp.exp(jnp.where(tril, cs[:, hd:hd + 1] - cs_t[hd:hd + 1, :], -jnp.inf))
            Xh = xs[:, sl] * dt[:, hd:hd + 1]
            Sp = st_ref[sl, :]
            y = _bdot(G * Lh, Xh) + ecs[:, hd:hd + 1] * _bdot_nt(Cg, Sp)
            y_ref[row0:row0 + Q, sl] = y
            st_ref[sl, :] = e_last[:, hd:hd + 1] * Sp + _bdot_tn(Xh * to_end[:, hd:hd + 1], Bg)


def _ssd_body(x_ref, g1_ref, wz_ref, wxbc_ref, wdt_ref, cw_ref, cb_ref, dtb_ref, alog_ref, dsk_ref,
              ng_ref, od_ref, tail_ref, fin_ref, ext_ref, st_ref, y_ref, *, T):
    @pl.when(pl.program_id(1) == 0)
    def _():
        ext_ref[0:SUBLANES, :] = jnp.zeros((SUBLANES, SSD_CONV_CH), F32)
        st_ref[...] = jnp.zeros_like(st_ref)

    h = _rms(x_ref[...], g1_ref[...]).astype(BF16)
    z = _mm(h, wz_ref[...])
    xbc = _mm(h, wxbc_ref[...])
    dtr = _mm(h, wdt_ref[...])
    ext_ref[SUBLANES:SUBLANES + T, :] = xbc
    xc = cb_ref[...]
    for kk in range(CONV_W - 1):
        xc = xc + ext_ref[pl.ds(SUBLANES - (CONV_W - 1) + kk, T), :] * cw_ref[kk:kk + 1, :]
    xc = xc + xbc * cw_ref[CONV_W - 1:CONV_W, :]
    tail = xbc[T - SUBLANES:T]
    ext_ref[0:SUBLANES, :] = tail
    tail_ref[...] = tail

    xc = _silu(xc)
    xs = xc[:, :SSD_INNER]
    gn = SSD_GROUPS * SSD_STATE
    Bm = xc[:, SSD_INNER:SSD_INNER + gn]
    Cm = xc[:, SSD_INNER + gn:]
    dt, a = _ssd_dt_a(dtr, dtb_ref, alog_ref)
    for c in range(T // SSD_CHUNK):
        rows = slice(c * SSD_CHUNK, (c + 1) * SSD_CHUNK)
        _ssd_chunk(xs[rows], dt[rows], a[rows], Bm[rows], Cm[rows], st_ref, y_ref, c * SSD_CHUNK)
    od_ref[...] = _ssd_gate_norm(y_ref[...], xs, z, dsk_ref, ng_ref)
    fin_ref[...] = st_ref[...]


def _ssd_prompt(x, W, l, T):
    B, S, _ = x.shape
    tile = lambda b, s: (b, s, 0)
    perb = lambda b, s: (b, 0, 0)
    wx = W["w_xbcdt"]
    return pl.pallas_call(
        functools.partial(_ssd_body, T=T),
        grid=(B, S // T),
        in_specs=[pl.BlockSpec((None, T, D_MODEL), tile), _lspec(W["norm1_g"], l),
                  _lspec(W["w_in"], l, (D_MODEL, SSD_INNER), (0, IN_OFFS[5] // SSD_INNER)),
                  _lspec(wx, l, (D_MODEL, SSD_CONV_CH), (0, 0)),
                  _lspec(wx, l, (D_MODEL, LANES), (0, SSD_CONV_CH // LANES)),
                  _lspec(W["ssd_conv_w"], l), _lspec(W["ssd_conv_b"], l), _lspec(W["ssd_dt_bias"], l),
                  _lspec(W["ssd_a_log"], l), _lspec(W["ssd_d_tab"], l), _lspec(W["ssd_norm_g"], l)],
        out_specs=[pl.BlockSpec((None, T, SSD_INNER), tile),
                   pl.BlockSpec((None, SUBLANES, SSD_CONV_CH), perb),
                   pl.BlockSpec((None, SSD_INNER, SSD_STATE), perb)],
        out_shape=[jax.ShapeDtypeStruct((B, S, SSD_INNER), F32),
                   jax.ShapeDtypeStruct((B, SUBLANES, SSD_CONV_CH), F32),
                   jax.ShapeDtypeStruct((B, SSD_INNER, SSD_STATE), F32)],
        scratch_shapes=[pltpu.VMEM((T + SUBLANES, SSD_CONV_CH), F32),
                        pltpu.VMEM((SSD_INNER, SSD_STATE), F32),
                        pltpu.VMEM((T, SSD_INNER), F32)],
        compiler_params=_cp("parallel", "arbitrary"), name="ssd_prompt",
    )(x, W["norm1_g"], W["w_in"], wx, wx, W["ssd_conv_w"], W["ssd_conv_b"],
      W["ssd_dt_bias"], W["ssd_a_log"], W["ssd_d_tab"], W["ssd_norm_g"])


def _proj_body(x_ref, g1_ref, w_ref, o_ref):
    o_ref[...] = _mm(_rms(x_ref[...], g1_ref[...]), w_ref[...])


def _proj(x2d, W, l, name, col0, ncols, tn):
    R = x2d.shape[0]
    assert col0 % tn == 0 and ncols % tn == 0
    j0 = col0 // tn
    return pl.pallas_call(
        _proj_body, grid=(ncols // tn,),
        in_specs=[_const_spec((R, D_MODEL)), _lspec(W["norm1_g"], l),
                  pl.BlockSpec((None, D_MODEL, tn), lambda j: (l, 0, j0 + j))],
        out_specs=pl.BlockSpec((R, tn), lambda j: (0, j)),
        out_shape=jax.ShapeDtypeStruct((R, ncols), F32),
        compiler_params=_cp("parallel"), name="proj_rest",
    )(x2d, W["norm1_g"], W[name])


def _conv_step(xnew, st_ref, cw_ref, cb_ref, C):
    out = cb_ref[...]
    for kk in range(CONV_W - 1):
        out = out + st_ref[:, kk * C:(kk + 1) * C] * cw_ref[kk:kk + 1, :]
    return out + xnew * cw_ref[CONV_W - 1:CONV_W, :]


def _dec_branches_body(proj_ref, dtr_ref, scb_ref, hb_ref, scd_ref,
                       cwb_ref, cbb_ref, wa_ref, ba_ref, wx_ref, bx_ref, lam_ref,
                       lg_ref, lb_ref, w00_ref, b0_ref,
                       cwd_ref, cbd_ref, dtb_ref, alog_ref,
                       ob_ref, ncb_ref, oc_ref, vc_ref, ncd_ref, xs_ref, xdt_ref, ea_ref, bc_ref,
                       *, offs):
    o_xb, o_uv, o_xbc = offs
    xb = proj_ref[:, o_xb:o_xb + LRU_WIDTH]
    xc = _conv_step(xb, scb_ref, cwb_ref, cbb_ref, LRU_WIDTH)
    a, b = _lru_gates(xc, wa_ref, ba_ref, wx_ref, bx_ref, lam_ref)
    ob_ref[...] = a * hb_ref[...] + b
    ncb_ref[:, 0:2 * LRU_WIDTH] = scb_ref[:, LRU_WIDTH:3 * LRU_WIDTH]
    ncb_ref[:, 2 * LRU_WIDTH:3 * LRU_WIDTH] = xb
    u, v = _gelu_ln(proj_ref[:, o_uv:o_uv + 2 * SGU_WIDTH], lg_ref, lb_ref)
    vc_ref[...] = v
    oc_ref[...] = u * (w00_ref[...] * v + b0_ref[...])
    xbc = proj_ref[:, o_xbc:o_xbc + SSD_CONV_CH]
    xcd = _silu(_conv_step(xbc, scd_ref, cwd_ref, cbd_ref, SSD_CONV_CH))
    ncd_ref[:, 0:2 * SSD_CONV_CH] = scd_ref[:, SSD_CONV_CH:3 * SSD_CONV_CH]
    ncd_ref[:, 2 * SSD_CONV_CH:3 * SSD_CONV_CH] = xbc
    xs = xcd[:, :SSD_INNER]
    dt, a_dt = _ssd_dt_a(dtr_ref[...], dtb_ref, alog_ref)
    hrow = lax.broadcasted_iota(jnp.int32, (LANES, SSD_INNER), 0)
    hlane = lax.broadcasted_iota(jnp.int32, (LANES, SSD_INNER), 1)
    expand = (hlane // HEAD_DIM == hrow).astype(F32)
    dt_e = jnp.dot(dt, expand, preferred_element_type=F32, precision=HIGHEST)
    a_e = jnp.dot(a_dt, expand, preferred_element_type=F32, precision=HIGHEST)
    xs_ref[...] = xs
    xdt_ref[...] = xs * dt_e
    ea_ref[...] = jnp.exp(a_e)
    bc_ref[...] = xcd[:, SSD_INNER:]


def _dec_branches(proj, dtr, offs, scb, hb, scd, W, l):
    Bd = proj.shape[0]
    f = lambda n: jax.ShapeDtypeStruct((Bd, n), F32)
    whole = lambda a: pl.BlockSpec(a.shape, lambda i: (0,) * a.ndim)
    names = ("conv_b_w", "conv_b_b", "lru_wa_bd", "lru_ba", "lru_wx_bd", "lru_bx", "lru_lambda",
             "sgu_ln_g", "sgu_ln_b", "sgu_w00_tab", "sgu_b0_tab",
             "ssd_conv_w", "ssd_conv_b", "ssd_dt_bias", "ssd_a_log")
    acts = (proj, dtr, scb, hb, scd)
    widths = (LRU_WIDTH, 3 * LRU_WIDTH, SGU_WIDTH, SGU_WIDTH, 3 * SSD_CONV_CH, SSD_INNER, SSD_INNER,
              SSD_INNER, 2 * SSD_GROUPS * SSD_STATE)
    return pl.pallas_call(
        functools.partial(_dec_branches_body, offs=offs),
        grid=(1,),
        in_specs=[whole(a) for a in acts] + [_lspec(W[n], l) for n in names],
        out_specs=[pl.BlockSpec((Bd, n), lambda i: (0, 0)) for n in widths],
        out_shape=[f(n) for n in widths],
        compiler_params=_cp("arbitrary"), name="dec_branches",
    )(*acts, *[W[n] for n in names])


def _dec_ssd_body(xs_ref, xdt_ref, ea_ref, bc_ref, z_ref, st_ref, dsk_ref, ng_ref, od_ref, ns_ref):
    gn = SSD_GROUPS * SSD_STATE
    half = SSD_INNER // SSD_GROUPS
    ridx = lax.broadcasted_iota(jnp.int32, (LANES, SSD_INNER), 0)
    rows = jnp.where(ridx == 0, jnp.broadcast_to(xdt_ref[...], (LANES, SSD_INNER)),
                     jnp.where(ridx == 1, jnp.broadcast_to(ea_ref[...], (LANES, SSD_INNER)), 0.0))
    cols = rows.T
    xdt_c = cols[:, 0:1]
    ea_c = cols[:, 1:2]
    bc = bc_ref[...]
    Bm, Cm = bc[:, :gn], bc[:, gn:]
    rowi = lax.broadcasted_iota(jnp.int32, (SSD_INNER, SSD_STATE), 0)
    b_full = jnp.where(rowi < half, jnp.broadcast_to(Bm[:, :SSD_STATE], (SSD_INNER, SSD_STATE)),
                       jnp.broadcast_to(Bm[:, SSD_STATE:], (SSD_INNER, SSD_STATE)))
    new = ea_c * st_ref[...] + xdt_c * b_full
    ns_ref[...] = new
    cidx = lax.broadcasted_iota(jnp.int32, (SUBLANES, SSD_STATE), 0)
    c8 = jnp.where(cidx == 0, jnp.broadcast_to(Cm[:, :SSD_STATE], (SUBLANES, SSD_STATE)),
                   jnp.where(cidx == 1, jnp.broadcast_to(Cm[:, SSD_STATE:], (SUBLANES, SSD_STATE)), 0.0))
    y8 = lax.dot_general(c8, new, (((1,), (1,)), ((), ())), preferred_element_type=F32,
                         precision=HIGHEST)
    lane = lax.broadcasted_iota(jnp.int32, (1, SSD_INNER), 1)
    y = jnp.where(lane < half, y8[0:1], y8[1:2])
    od_ref[...] = _ssd_gate_norm(y, xs_ref[...], z_ref[...], dsk_ref, ng_ref)


def _dec_ssd(xs, xdt, ea, bc, z, state_all, W, l):
    Bd = xs.shape[0]
    row = lambda b: (b, 0, 0)
    r3 = lambda t: t.reshape(Bd, 1, t.shape[-1])
    st = state_all.reshape(state_all.shape[0], Bd, SSD_INNER, SSD_STATE)
    od, ns = pl.pallas_call(
        _dec_ssd_body, grid=(Bd,),
        in_specs=[pl.BlockSpec((None, 1, SSD_INNER), row)] * 3
                 + [pl.BlockSpec((None, 1, 2 * SSD_GROUPS * SSD_STATE), row),
                    pl.BlockSpec((None, 1, SSD_INNER), row),
                    pl.BlockSpec((None, None, SSD_INNER, SSD_STATE), lambda b: (l, b, 0, 0)),
                    _lspec(W["ssd_d_tab"], l), _lspec(W["ssd_norm_g"], l)],
        out_specs=[pl.BlockSpec((None, 1, SSD_INNER), row),
                   pl.BlockSpec((None, SSD_INNER, SSD_STATE), row)],
        out_shape=[jax.ShapeDtypeStruct((Bd, 1, SSD_INNER), F32),
                   jax.ShapeDtypeStruct((Bd, SSD_INNER, SSD_STATE), F32)],
        compiler_params=_cp("parallel"), name="dec_ssd",
    )(r3(xs), r3(xdt), r3(ea), r3(bc), r3(z), st, W["ssd_d_tab"], W["ssd_norm_g"])
    return od.reshape(Bd, SSD_INNER), ns.reshape(state_all.shape[1:])


def _merge_body(x_ref, o1_ref, l1_ref, o2_ref, l2_ref, o3_ref, l3_ref, ob_ref, oc_ref, od_ref,
                g1_ref, wg_ref, wpa_ref, wpb_ref, wpc_ref, wpd_ref, wo_ref, out_ref, *slab, dils):
    x = x_ref[...]
    h = _rms(x, g1_ref[...]).astype(wg_ref.dtype)
    tok = lambda ref, d: _load_strided_view(ref, slab[0], d) if d > 1 else ref[...]
    o1, o2, o3 = (tok(r, d) for r, d in zip((o1_ref, o2_ref, o3_ref), dils))
    l1, l2, l3 = (tok(r, d) for r, d in zip((l1_ref, l2_ref, l3_ref), dils))
    m = jnp.maximum(jnp.maximum(l1, l2), l3)
    e1, e2, e3 = jnp.exp(l1 - m), jnp.exp(l2 - m), jnp.exp(l3 - m)
    den = e1 + e2 + e3
    oa = (e1 / den) * o1 + (e2 / den) * o2 + (e3 / den) * o3
    merged = None
    for bi, (o, w_ref) in enumerate(((oa, wpa_ref), (ob_ref[...], wpb_ref),
                                     (oc_ref[...], wpc_ref), (od_ref[...], wpd_ref))):
        gate = _sigmoid(_mm(h, wg_ref[:, bi * D_MODEL:(bi + 1) * D_MODEL]))
        term = gate * _mm(o, w_ref[...])
        merged = term if merged is None else merged + term
    out_ref[...] = x + _mm(merged, wo_ref[...])


def _merge(x2d, att, ob, oc, od, W, l, tm, dils):
    R = x2d.shape[0]
    row = lambda i: (i, 0)
    names = ("norm1_g", "w_gates", "w_pa", "w_pb", "w_pc", "w_pd", "w_o")
    att_specs = [pl.BlockSpec((tm // d, d * A_OUT), row) for d in dils for _ in range(2)]
    scratch = [pltpu.VMEM((A_OUT // LANES, tm, LANES), F32)] if max(dils) > 1 else []
    return pl.pallas_call(
        functools.partial(_merge_body, dils=dils), grid=(R // tm,),
        in_specs=[pl.BlockSpec((tm, D_MODEL), row)] + att_specs
                 + [pl.BlockSpec((tm, LRU_WIDTH), row)] * 3 + [_lspec(W[n], l) for n in names],
        out_specs=pl.BlockSpec((tm, D_MODEL), row),
        out_shape=jax.ShapeDtypeStruct((R, D_MODEL), F32), scratch_shapes=scratch,
        compiler_params=_cp("parallel"), name="merge",
    )(x2d, *att, ob, oc, od, *[W[n] for n in names])


def _route(logits):
    lane = lax.broadcasted_iota(jnp.int32, logits.shape, 1)
    big = jnp.int32(LANES)
    ninf = -jnp.inf
    gl = jnp.where(lane < N_EXPERT_GROUPS, logits, ninf)
    gm = jnp.max(gl, axis=-1, keepdims=True)
    gsel = jnp.min(jnp.where(gl == gm, lane, big), axis=-1, keepdims=True)
    pg = 1.0 / jnp.sum(jnp.exp(gl - gm), axis=-1, keepdims=True)
    lo = N_EXPERT_GROUPS + EXPERTS_PER_GROUP * gsel
    el = jnp.where(jnp.logical_and(lane >= lo, lane < lo + EXPERTS_PER_GROUP), logits, ninf)
    t1 = jnp.max(el, axis=-1, keepdims=True)
    i1 = jnp.min(jnp.where(el == t1, lane, big), axis=-1, keepdims=True)
    el2 = jnp.where(lane == i1, ninf, el)
    t2 = jnp.max(el2, axis=-1, keepdims=True)
    i2 = jnp.min(jnp.where(el2 == t2, lane, big), axis=-1, keepdims=True)
    e2 = jnp.exp(t2 - t1)
    den = 1.0 + e2
    w1 = (1.0 / den) * pg
    w2 = (e2 / den) * pg
    return jnp.where(lane == i1, w1, 0.0) + jnp.where(lane == i2, w2, 0.0), gsel


def _moe_body(x_ref, g2_ref, wr_ref, br_ref, wg_ref, wu_ref, wd_ref, out_ref, h_ref, comb_ref, acc_ref):
    e = pl.program_id(1)

    @pl.when(e == 0)
    def _():
        h = _rms(x_ref[...], g2_ref[...])
        h_ref[...] = h.astype(h_ref.dtype)
        logits = jnp.dot(h, wr_ref[...], preferred_element_type=F32, precision=HIGHEST) + br_ref[...]
        comb_ref[...] = _route(logits)[0]
        acc_ref[...] = jnp.zeros_like(acc_ref)

    h = h_ref[...]
    lane = lax.broadcasted_iota(jnp.int32, comb_ref.shape, 1)
    c = jnp.sum(jnp.where(lane == N_EXPERT_GROUPS + e, comb_ref[...], 0.0), axis=-1, keepdims=True)
    hg = _mm(h, wg_ref[...])
    hu = _mm(h, wu_ref[...])
    act = _silu(hg) * hu * c
    acc_ref[...] += _mm(act, wd_ref[...])

    @pl.when(e == N_EXPERTS - 1)
    def _():
        out_ref[...] = x_ref[...] + acc_ref[...]


def _moe(x2d, W, l, tm):
    R = x2d.shape[0]
    row = lambda i, e: (i, 0)
    expert = lambda r, c: pl.BlockSpec((None, None, r, c), lambda i, e: (l, e, 0, 0))
    return pl.pallas_call(
        _moe_body, grid=(R // tm, N_EXPERTS),
        in_specs=[pl.BlockSpec((tm, D_MODEL), row), _lspec(W["norm2_g"], l), _lspec(W["w_router"], l),
                  _lspec(W["b_router"], l),
                  expert(D_MODEL, D_EXPERT), expert(D_MODEL, D_EXPERT), expert(D_EXPERT, D_MODEL)],
        out_specs=pl.BlockSpec((tm, D_MODEL), row),
        out_shape=jax.ShapeDtypeStruct((R, D_MODEL), F32),
        scratch_shapes=[pltpu.VMEM((tm, D_MODEL), W["moe_w_gate"].dtype), pltpu.VMEM((tm, LANES), F32),
                        pltpu.VMEM((tm, D_MODEL), F32)],
        compiler_params=_cp("parallel", "arbitrary"), name="moe",
    )(x2d, W["norm2_g"], W["w_router"], W["b_router"], W["moe_w_gate"], W["moe_w_up"],
      W["moe_w_down"])


GID_LANE = N_EXPERT_GROUPS + N_EXPERTS
SEG_ALIGN = 16
MOE_RB = TM_MOE // N_EXPERT_GROUPS + 64
MOE_TMP = TM_MOE + N_EXPERT_GROUPS * SEG_ALIGN + MOE_RB
PERM_BLK = 128
UNPERM_BLK = 256
assert MOE_TMP % PERM_BLK == 0 and TM_MOE % UNPERM_BLK == 0


def _route_body(x_ref, g2_ref, wr_ref, br_ref, comb_ref, cnt_ref):
    h = _rms(x_ref[...], g2_ref[...])
    logits = jnp.dot(h, wr_ref[...], preferred_element_type=F32, precision=HIGHEST) + br_ref[...]
    comb, gsel = _route(logits)
    lane = lax.broadcasted_iota(jnp.int32, comb.shape, 1)
    comb_ref[...] = jnp.where(lane == GID_LANE, gsel.astype(F32), comb)
    counts = jnp.sum((lane == gsel).astype(F32), axis=0, keepdims=True)
    cnt_ref[...] = jnp.broadcast_to(counts, cnt_ref.shape)


def _route_call(x2d, W, l, tm):
    R = x2d.shape[0]
    return pl.pallas_call(
        _route_body, grid=(R // tm,),
        in_specs=[pl.BlockSpec((tm, D_MODEL), lambda i: (i, 0)), _lspec(W["norm2_g"], l),
                  _lspec(W["w_router"], l), _lspec(W["b_router"], l)],
        out_specs=[pl.BlockSpec((tm, LANES), lambda i: (i, 0)),
                   pl.BlockSpec((None, SUBLANES, LANES), lambda i: (i, 0, 0))],
        out_shape=[jax.ShapeDtypeStruct((R, LANES), F32),
                   jax.ShapeDtypeStruct((R // tm, SUBLANES, LANES), F32)],
        compiler_params=_cp("parallel"), name="moe_route",
    )(x2d, W["norm2_g"], W["w_router"], W["b_router"])


def _moe_sorted_body(off_ref, nch_ref, x_ref, comb_ref, g2_ref, wg_ref, wu_ref, wd_ref, out_ref,
                     sh_ref, sc_ref, acc_ref, rcol_ref, lo_ref, *, TM, RB):
    i = pl.program_id(0)
    g = pl.program_id(1)
    TMP = sh_ref.shape[0]

    @pl.when(g == 0)
    def _():
        h2 = _rms(x_ref[...], g2_ref[...]).astype(BF16)
        comb = comb_ref[...]
        lane = lax.broadcasted_iota(jnp.int32, comb.shape, 1)
        gid = jnp.sum(jnp.where(lane == GID_LANE, comb, 0.0), axis=-1, keepdims=True)
        onehot = jnp.logical_and(lane < N_EXPERT_GROUPS, lane.astype(F32) == gid).astype(F32)
        trow = lax.broadcasted_iota(jnp.int32, onehot.shape, 0)
        run = onehot
        k = 1
        while k < TM:
            run = run + jnp.where(trow >= k, pltpu.roll(run, k, 0), 0.0)
            k *= 2
        prefix = run - onehot
        lane1 = lax.broadcasted_iota(jnp.int32, (1, LANES), 1)
        offv = jnp.zeros((1, LANES), F32)
        for gg in range(N_EXPERT_GROUPS):
            offv = jnp.where(lane1 == gg, off_ref[i * N_EXPERT_GROUPS + gg].astype(F32), offv)
        rank = jnp.sum(onehot * (offv + prefix), axis=-1, keepdims=True)
        rank_b = jnp.broadcast_to(rank, (TM, LANES))
        rcol_ref[...] = rank_b
        rank_row = rank_b.T[0:1]
        c1 = comb.astype(BF16)
        r1 = comb - c1.astype(F32)
        c2 = r1.astype(BF16)
        c3 = (r1 - c2.astype(F32)).astype(BF16)
        for blk in range(TMP // PERM_BLK):
            rows = slice(blk * PERM_BLK, (blk + 1) * PERM_BLK)
            srow = lax.broadcasted_iota(jnp.int32, (PERM_BLK, TM), 0) + blk * PERM_BLK
            perm = (srow.astype(F32) == rank_row).astype(BF16)
            sh_ref[rows, :] = jnp.dot(perm, h2, preferred_element_type=F32).astype(BF16)
            sc_ref[rows, :] = (jnp.dot(perm, c1, preferred_element_type=F32)
                               + jnp.dot(perm, c2, preferred_element_type=F32)
                               + jnp.dot(perm, c3, preferred_element_type=F32))
        acc_ref[...] = jnp.zeros_like(acc_ref)

    off = off_ref[i * N_EXPERT_GROUPS + g]
    lane_c = lax.broadcasted_iota(jnp.int32, (RB, LANES), 1)

    def chunk(j, carry):
        r0 = pl.multiple_of(off + j * RB, SEG_ALIGN)
        rows = sh_ref[pl.ds(r0, RB), :]
        cc = sc_ref[pl.ds(r0, RB), :]
        y = jnp.zeros((RB, D_MODEL), F32)
        for e in range(EXPERTS_PER_GROUP):
            ce = jnp.sum(jnp.where(lane_c == N_EXPERT_GROUPS + EXPERTS_PER_GROUP * g + e, cc, 0.0),
                         axis=-1, keepdims=True)
            hg = jnp.dot(rows, wg_ref[e], preferred_element_type=F32)
            hu = jnp.dot(rows, wu_ref[e], preferred_element_type=F32)
            y = y + _bdot(_silu(hg) * hu * ce, wd_ref[e])
        acc_ref[pl.ds(r0, RB), :] += y
        return carry

    lax.fori_loop(0, nch_ref[i * N_EXPERT_GROUPS + g], chunk, 0)

    @pl.when(g == N_EXPERT_GROUPS - 1)
    def _():
        for blk in range(TMP // PERM_BLK):
            rows = slice(blk * PERM_BLK, (blk + 1) * PERM_BLK)
            a = acc_ref[rows, :]
            hi = a.astype(BF16)
            sh_ref[rows, :] = hi
            lo_ref[rows, :] = (a - hi.astype(F32)).astype(BF16)
        scol = lax.broadcasted_iota(jnp.int32, (UNPERM_BLK, TMP), 1).astype(F32)
        for blk in range(TM // UNPERM_BLK):
            rows = slice(blk * UNPERM_BLK, (blk + 1) * UNPERM_BLK)
            unperm = (scol == rcol_ref[rows, 0:1]).astype(BF16)
            out_ref[rows, :] = x_ref[rows, :] + (jnp.dot(unperm, sh_ref[...], preferred_element_type=F32)
                                                 + jnp.dot(unperm, lo_ref[...], preferred_element_type=F32))


def _moe_sorted(x2d, W, l):
    R = x2d.shape[0]
    TM, RB, TMP = TM_MOE, MOE_RB, MOE_TMP
    nt = R // TM
    comb, counts = _route_call(x2d, W, l, TM)
    cnt = counts[:, 0, :N_EXPERT_GROUPS].astype(jnp.int32)
    padded = (cnt + SEG_ALIGN - 1) // SEG_ALIGN * SEG_ALIGN
    off = (jnp.cumsum(padded, axis=1) - padded).reshape(-1)
    nch = ((cnt + RB - 1) // RB).reshape(-1)
    row = lambda i, g, *_: (i, 0)
    const = lambda a: pl.BlockSpec((None,) + tuple(a.shape[1:]), lambda i, g, *_: (l,) + (0,) * (a.ndim - 1))
    grp = lambda r, c: pl.BlockSpec((None, None, EXPERTS_PER_GROUP, r, c), lambda i, g, *_: (l, g, 0, 0, 0))
    gview = lambda a: a.reshape(a.shape[0], N_EXPERT_GROUPS, EXPERTS_PER_GROUP, a.shape[2], a.shape[3])
    return pl.pallas_call(
        functools.partial(_moe_sorted_body, TM=TM, RB=RB),
        grid_spec=pltpu.PrefetchScalarGridSpec(
            num_scalar_prefetch=2, grid=(nt, N_EXPERT_GROUPS),
            in_specs=[pl.BlockSpec((TM, D_MODEL), row, pipeline_mode=pl.Buffered(1)),
                      pl.BlockSpec((TM, LANES), row), const(W["norm2_g"]),
                      grp(D_MODEL, D_EXPERT), grp(D_MODEL, D_EXPERT), grp(D_EXPERT, D_MODEL)],
            out_specs=pl.BlockSpec((TM, D_MODEL), row),
            scratch_shapes=[pltpu.VMEM((TMP, D_MODEL), BF16), pltpu.VMEM((TMP, LANES), F32),
                            pltpu.VMEM((TMP, D_MODEL), F32), pltpu.VMEM((TM, LANES), F32),
                            pltpu.VMEM((TMP, D_MODEL), BF16)]),
        out_shape=jax.ShapeDtypeStruct((R, D_MODEL), F32),
        compiler_params=_cp("parallel", "arbitrary"), name="moe_sorted",
    )(off, nch, x2d, comb, W["norm2_g"], gview(W["moe_w_gate"]), gview(W["moe_w_up"]), gview(W["moe_w_down"]))


def _rope_tables(pos):
    half = HEAD_DIM // 2
    inv = ROPE_THETA ** (-jnp.arange(half, dtype=F32) / half)
    ang = pos.astype(F32)[:, None] * inv[None]
    c, s = jnp.cos(ang), jnp.sin(ang)
    z = jnp.zeros_like(s)
    reps = LANES // HEAD_DIM
    return (jnp.tile(jnp.concatenate([c, c], 1), (1, reps)),
            jnp.tile(jnp.concatenate([-s, z], 1), (1, reps)),
            jnp.tile(jnp.concatenate([z, s], 1), (1, reps)))


def _block_diag(w):
    L, n, k, _ = w.shape
    eye = jnp.eye(n, dtype=w.dtype)
    return (eye[None, :, None, :, None] * w[:, :, :, None, :]).reshape(L, n * k, n * k)


def _small_params(p):
    vec = lambda t: t.reshape(t.shape[0], 1, -1)
    W = {}
    for n in ("norm1_g", "norm2_g", "conv_b_b", "lru_ba", "lru_bx", "lru_lambda", "sgu_ln_g", "sgu_ln_b",
              "ssd_conv_b", "ssd_norm_g"):
        W[n] = vec(p[n])
    reps = A_WIDTH // HEAD_DIM
    W["q_norm_g"] = vec(jnp.tile(p["q_norm_g"], (1, reps)))
    W["k_norm_g"] = vec(jnp.tile(p["k_norm_g"], (1, reps)))
    W["conv_b_w"] = p["conv_b_w"]
    W["ssd_conv_w"] = p["ssd_conv_w"]
    W["sgu_w"] = p["sgu_w"]
    gw = SGU_WIDTH // SGU_GROUPS
    W["sgu_b_tab"] = jnp.repeat(jnp.swapaxes(p["sgu_b"], 1, 2), gw, axis=2)
    W["sgu_w00_tab"] = vec(jnp.repeat(p["sgu_w"][:, :, 0, 0], gw, axis=1))
    W["sgu_b0_tab"] = vec(jnp.repeat(p["sgu_b"][:, :, 0], gw, axis=1))
    padh = lambda t: vec(jnp.pad(t, ((0, 0), (0, LANES - SSD_HEADS))))
    W["ssd_dt_bias"], W["ssd_a_log"] = padh(p["ssd_dt_bias"]), padh(p["ssd_a_log"])
    W["ssd_d_tab"] = vec(jnp.repeat(p["ssd_d"], HEAD_DIM, axis=1))
    we = jnp.transpose(p["router_exp_w"], (0, 2, 1, 3)).reshape(-1, D_MODEL, N_EXPERTS)
    wr = jnp.concatenate([p["router_group_w"], we], axis=2)
    W["w_router"] = jnp.pad(wr, ((0, 0), (0, 0), (0, LANES - wr.shape[2])))
    br = jnp.concatenate([p["router_group_b"], p["router_exp_b"].reshape(-1, N_EXPERTS)], axis=1)
    W["b_router"] = vec(jnp.pad(br, ((0, 0), (0, LANES - br.shape[1]))))
    return W


def _matrix_params(p, wdt):
    W = {}
    w_in = p["w_in"].astype(wdt)
    W["w_in"] = w_in
    o_xbc, o_dt, o_g = IN_OFFS[6], IN_OFFS[7], IN_OFFS[8]
    W["w_xbcdt"] = jnp.concatenate(
        [w_in[:, :, o_xbc:o_dt], jnp.pad(w_in[:, :, o_dt:o_g], ((0, 0), (0, 0), (0, LANES - SSD_HEADS)))], axis=2)
    W["w_gates"] = w_in[:, :, o_g:]
    reps = A_WIDTH // HEAD_DIM
    W["gbd"] = _block_diag(jnp.ones((1, reps, HEAD_DIM, HEAD_DIM), wdt))[0]
    W["lru_wa_bd"] = _block_diag(p["lru_wa"]).astype(wdt)
    W["lru_wx_bd"] = _block_diag(p["lru_wx"]).astype(wdt)
    for n in ("w_pa", "w_pb", "w_pc", "w_pd", "w_o", "moe_w_gate", "moe_w_up", "moe_w_down"):
        W[n] = p[n].astype(wdt)
    return W


def _kv_rows(k, v, gi):
    sl = slice(gi * A_OUT, (gi + 1) * A_OUT)
    shp = k.shape[:-1] + (A_HPG, HEAD_DIM)
    return jnp.stack([k[..., sl].reshape(shp), v[..., sl].reshape(shp)], axis=-3)


def _prompt_layer(x, W, l, tabs):
    B, S, _ = x.shape
    x2d = x.reshape(B * S, D_MODEL)
    *qkv, k, v = _qkv(x2d, W, l, tabs, TM_QKV, split=True)
    att = _attn_prompt(qkv, B, S)
    ob, tail_b = _lru_prompt(x, W, l, T_SEQ)
    oc = _sgu_prompt(x2d, W, l, T_SGU)
    od, tail_d, fin = _ssd_prompt(x, W, l, T_SEQ)
    x2d = _merge(x2d, att, ob.reshape(B * S, -1), oc, od.reshape(B * S, -1), W, l, TM_MERGE,
                 tuple(dil for _, dil in A_GROUPS))
    x2d = _moe_sorted(x2d, W, l)
    k3, v3 = k.reshape(B, S, -1), v.reshape(B, S, -1)
    kvs = [_kv_rows(k3[:, S - min(w, S):], v3[:, S - min(w, S):], gi) for gi, (w, _) in enumerate(A_GROUPS)]
    nb = CONV_W - 1
    states = (tail_b[:, SUBLANES - nb:], ob[:, -1], tail_d[:, SUBLANES - nb:],
              fin.reshape(B, SSD_HEADS, HEAD_DIM, SSD_STATE))
    return x2d.reshape(B, S, D_MODEL), kvs, states


def _decode_layer(x2d, W, l, tabs, caches, scb, hb, scd, ssm_all):
    Bd = x2d.shape[0]
    q, k, v = _qkv(x2d, W, l, tabs, Bd, split=False)
    att = _attn_step(q, k, v, caches, l)
    col0, ncols = IN_OFFS[3], IN_OFFS[7] - IN_OFFS[3]
    proj = _proj(x2d, W, l, "w_in", col0, ncols, 256)
    dtr = _proj(x2d, W, l, "w_xbcdt", SSD_CONV_CH, LANES, LANES)
    offs = (0, IN_OFFS[4] - col0, IN_OFFS[6] - col0)
    nb = CONV_W - 1
    ob, ncb, oc, vc, ncd, xs, xdt, ea, bc = _dec_branches(
        proj, dtr, offs, scb.reshape(Bd, nb * LRU_WIDTH), hb, scd.reshape(Bd, nb * SSD_CONV_CH), W, l)
    z = proj[:, IN_OFFS[5] - col0:IN_OFFS[5] - col0 + SSD_INNER]
    od, nssm = _dec_ssd(xs, xdt, ea, bc, z, ssm_all, W, l)
    x2d = _merge(x2d, att, ob, oc, od, W, l, Bd, (1,) * len(A_GROUPS))
    x2d = _moe(x2d, W, l, Bd)
    kvs = [_kv_rows(k, v, gi)[:, None] for gi in range(len(A_GROUPS))]
    states = (ncb.reshape(Bd, nb, LRU_WIDTH), ob, vc[:, None], ncd.reshape(Bd, nb, SSD_CONV_CH), nssm)
    return x2d, kvs, states


def kernel(x_prompt, x_sample, cache_kv_a1, cache_kv_a2, cache_kv_a3, state_conv_b, state_h_b, state_conv_d, state_ssm_d, norm1_g, w_in, q_norm_g, k_norm_g, conv_b_w, conv_b_b, lru_wa, lru_ba, lru_wx, lru_bx, lru_lambda, sgu_ln_g, sgu_ln_b, sgu_w, sgu_b, ssd_conv_w, ssd_conv_b, ssd_dt_bias, ssd_a_log, ssd_d, ssd_norm_g, w_pa, w_pb, w_pc, w_pd, w_o, norm2_g, router_group_w, router_group_b, router_exp_w, router_exp_b, moe_w_gate, moe_w_up, moe_w_down):
    p = dict(norm1_g=norm1_g, w_in=w_in, q_norm_g=q_norm_g, k_norm_g=k_norm_g, conv_b_w=conv_b_w,
             conv_b_b=conv_b_b, lru_wa=lru_wa, lru_ba=lru_ba, lru_wx=lru_wx, lru_bx=lru_bx,
             lru_lambda=lru_lambda, sgu_ln_g=sgu_ln_g, sgu_ln_b=sgu_ln_b, sgu_w=sgu_w, sgu_b=sgu_b,
             ssd_conv_w=ssd_conv_w, ssd_conv_b=ssd_conv_b, ssd_dt_bias=ssd_dt_bias, ssd_a_log=ssd_a_log,
             ssd_d=ssd_d, ssd_norm_g=ssd_norm_g, w_pa=w_pa, w_pb=w_pb, w_pc=w_pc, w_pd=w_pd, w_o=w_o,
             norm2_g=norm2_g, router_group_w=router_group_w, router_group_b=router_group_b,
             router_exp_w=router_exp_w, router_exp_b=router_exp_b, moe_w_gate=moe_w_gate,
             moe_w_up=moe_w_up, moe_w_down=moe_w_down)
    B, S, _ = x_prompt.shape
    Bd = x_sample.shape[0]
    depth = w_in.shape[0]
    small = _small_params(p)
    Wp = dict(small, **_matrix_params(p, BF16))
    Wd = dict(small, **_matrix_params(p, F32))
    tabs_p = _rope_tables(jnp.arange(S))
    tabs_s = tuple(jnp.broadcast_to(t, (Bd, LANES)) for t in _rope_tables(PAST_LEN + jnp.arange(1)))
    caches = (cache_kv_a1, cache_kv_a2, cache_kv_a3)
    yp, ys = x_prompt, x_sample.reshape(Bd, D_MODEL)
    P = [[] for _ in range(7)]
    Sx = [[] for _ in range(8)]
    for l in range(depth):
        yp, kvs, st = _prompt_layer(yp, Wp, l, tabs_p)
        for dst, val in zip(P, list(kvs) + list(st)):
            dst.append(val)
        ys, kvs, st = _decode_layer(ys, Wd, l, tabs_s, caches, state_conv_b[l], state_h_b[l],
                                    state_conv_d[l], state_ssm_d)
        for dst, val in zip(Sx, list(kvs) + list(st)):
            dst.append(val)
    st = jnp.stack
    return (yp, ys.reshape(Bd, 1, D_MODEL)) + tuple(st(t) for t in P) + tuple(st(t) for t in Sx)
```

```python
import functools

import jax
import jax.numpy as jnp
from jax import lax
from jax.experimental import pallas as pl
from jax.experimental.pallas import tpu as pltpu

F32 = jnp.float32
BF16 = jnp.bfloat16
HIGHEST = lax.Precision.HIGHEST

D_MODEL = 1024
DEPTH = 4
PAST_LEN = 8192
EPS = 1e-6
HEAD_DIM = 64
A_HPG = 4
A_GROUPS = ((128, 1), (512, 4), (2048, 16))
A_WIDTH = 768
A_OUT = 256
BLK = 128
ROPE_THETA = 10000.0
LRU_WIDTH = 768
LRU_C = 8.0
CONV_W = 4
SGU_WIDTH = 768
SGU_GROUPS = 4
SGU_CHUNK = 128
SSD_INNER = 768
SSD_HEADS = 12
SSD_GROUPS = 2
SSD_STATE = 128
SSD_CHUNK = 128
SSD_CONV_CH = 1280
N_BRANCH = 4
N_EXPERT_GROUPS = 4
EXPERTS_PER_GROUP = 4
N_EXPERTS = 16
D_EXPERT = 512
IN_SIZES = (768, 768, 768, 768, 1536, 768, 1280, 12, 4096)
IN_OFFS = tuple(sum(IN_SIZES[:i]) for i in range(len(IN_SIZES)))

LANES = 128
SUBLANES = 8
VMEM_LIMIT = 56 * 1024 * 1024

TM_QKV = 512
T_SEQ = 256
T_SGU = 512
TM_MERGE = 512
TM_MOE = 1024


def _cp(*sem):
    return pltpu.CompilerParams(dimension_semantics=sem, vmem_limit_bytes=VMEM_LIMIT)


def _const_spec(shape):
    nd = len(shape)
    return pl.BlockSpec(shape, lambda *_: (0,) * nd, pipeline_mode=pl.Buffered(1))


def _lspec(a, l, block=None, idx=None):
    shape = tuple(a.shape[1:]) if block is None else tuple(block)
    tail = (0,) * len(shape) if idx is None else tuple(idx)
    return pl.BlockSpec((None,) + shape, lambda *_: (l,) + tail, pipeline_mode=pl.Buffered(1))


def _rms(x, g):
    return x * lax.rsqrt(jnp.mean(x * x, axis=-1, keepdims=True) + EPS) * g


def _mm(a, w):
    if w.dtype == F32:
        return jnp.dot(a.astype(F32), w, preferred_element_type=F32, precision=HIGHEST)
    return jnp.dot(a.astype(BF16), w, preferred_element_type=F32)


def _bdot(a, b):
    return jnp.dot(a.astype(BF16), b.astype(BF16), preferred_element_type=F32)


def _bdot_nt(a, b):
    return lax.dot_general(a.astype(BF16), b.astype(BF16), (((1,), (1,)), ((), ())),
                           preferred_element_type=F32)


def _bdot_tn(a, b):
    return lax.dot_general(a.astype(BF16), b.astype(BF16), (((0,), (0,)), ((), ())),
                           preferred_element_type=F32)


def _sigmoid(x):
    return jax.nn.sigmoid(x)


def _silu(x):
    return x * jax.nn.sigmoid(x)


def _neg_expm1(x):
    t = jnp.tanh(0.5 * x)
    return -2.0 * t / (1.0 - t)


def _lanes6(t):
    return jnp.concatenate([t] * 6, axis=1)


def _store_strided_view(t, out_ref, slab_ref, dil):
    if dil == 1:
        out_ref[...] = t.astype(out_ref.dtype)
        return
    T = t.shape[0]
    halves = A_OUT // LANES
    for s in range(halves):
        slab_ref[s] = t[:, s * LANES:(s + 1) * LANES]
    for r in range(dil):
        for s in range(halves):
            c0 = r * A_OUT + s * LANES
            out_ref[:, c0:c0 + LANES] = slab_ref[s, pl.ds(r, T // dil, stride=dil), :].astype(out_ref.dtype)


def _load_strided_view(view_ref, slab_ref, dil):
    if dil == 1:
        return view_ref[...]
    n = view_ref.shape[0]
    halves = A_OUT // LANES
    for r in range(dil):
        for s in range(halves):
            c0 = r * A_OUT + s * LANES
            slab_ref[s, pl.ds(r, n, stride=dil), :] = view_ref[:, c0:c0 + LANES]
    return jnp.concatenate([slab_ref[s] for s in range(halves)], axis=1)


def _qkv_body(x_ref, g1_ref, wq_ref, wk_ref, wv_ref, qg_ref, kg_ref, gbd_ref,
              cos_ref, sa_ref, sb_ref, *outs, split):
    h = _rms(x_ref[...], g1_ref[...]).astype(wq_ref.dtype)
    cos = _lanes6(cos_ref[...])
    sa = _lanes6(sa_ref[...])
    sb = _lanes6(sb_ref[...])
    gbd = gbd_ref[...]

    def normed_rotated(w_ref, hg_ref):
        t = _mm(h, w_ref[...])
        ss = _mm(t * t, gbd)
        tn = t * lax.rsqrt(ss * (1.0 / HEAD_DIM) + EPS) * hg_ref[...]
        return (tn * cos + pltpu.roll(tn, A_WIDTH - HEAD_DIM // 2, 1) * sa
                + pltpu.roll(tn, HEAD_DIM // 2, 1) * sb)

    q = normed_rotated(wq_ref, qg_ref) * (HEAD_DIM ** -0.5)
    k = normed_rotated(wk_ref, kg_ref)
    v = _mm(h, wv_ref[...])
    if split:
        *outs, slab_ref = outs
        for gi, (_, dil) in enumerate(A_GROUPS):
            sl = slice(gi * A_OUT, (gi + 1) * A_OUT)
            for j, t in enumerate((q, k, v)):
                _store_strided_view(t[:, sl], outs[3 * j + gi], slab_ref, dil)
        outs[9][...] = k
        outs[10][...] = v
    else:
        outs[0][...] = q
        outs[1][...] = k
        outs[2][...] = v


def _qkv(x2d, W, l, tabs, tm, split):
    R = x2d.shape[0]
    cos, sa, sb = tabs
    npos = cos.shape[0] // tm
    row = lambda i: (i, 0)
    pos = lambda i: (i % npos, 0)
    wcol = lambda j: _lspec(W["w_in"], l, (D_MODEL, A_WIDTH), (0, j))
    full = pl.BlockSpec((tm, A_WIDTH), row)
    scratch = []
    if split:
        dils = [dil for _, dil in A_GROUPS] * 3
        out_specs = [pl.BlockSpec((tm // d, d * A_OUT), row) for d in dils] + [full, full]
        out_shape = ([jax.ShapeDtypeStruct((R // d, d * A_OUT), BF16) for d in dils]
                     + [jax.ShapeDtypeStruct((R, A_WIDTH), F32)] * 2)
        scratch = [pltpu.VMEM((A_OUT // LANES, tm, LANES), F32)]
    else:
        out_specs = [full] * 3
        out_shape = [jax.ShapeDtypeStruct((R, A_WIDTH), F32)] * 3
    return pl.pallas_call(
        functools.partial(_qkv_body, split=split),
        grid=(R // tm,), scratch_shapes=scratch,
        in_specs=[pl.BlockSpec((tm, D_MODEL), row), _lspec(W["norm1_g"], l),
                  wcol(0), wcol(1), wcol(2),
                  _lspec(W["q_norm_g"], l), _lspec(W["k_norm_g"], l),
                  _const_spec((A_WIDTH, A_WIDTH)),
                  pl.BlockSpec((tm, LANES), pos), pl.BlockSpec((tm, LANES), pos),
                  pl.BlockSpec((tm, LANES), pos)],
        out_specs=out_specs, out_shape=out_shape,
        compiler_params=_cp("parallel"), name="qkv_proj",
    )(x2d, W["norm1_g"], W["w_in"], W["w_in"], W["w_in"], W["q_norm_g"], W["k_norm_g"],
      W["gbd"], cos, sa, sb)


def _attn_group(q_ref, kp_ref, kc_ref, vp_ref, vc_ref, o_ref, l_ref, mask, lane_head):
    q = q_ref[...]
    zero = jnp.zeros_like(q)
    qs = jnp.concatenate([jnp.where(lane_head == hh, q, zero) for hh in range(A_HPG)], axis=0)
    kk = jnp.concatenate([kp_ref[...], kc_ref[...]], axis=0)
    vv = jnp.concatenate([vp_ref[...], vc_ref[...]], axis=0)
    s = jnp.where(mask, _bdot_nt(qs, kk), -jnp.inf)
    m = jnp.max(s, axis=-1, keepdims=True)
    e = jnp.exp(s - m)
    den = jnp.sum(e, axis=-1, keepdims=True)
    o4 = _bdot(e * (1.0 / den), vv)
    lse = m + jnp.log(den)
    o = o4[(A_HPG - 1) * BLK:]
    lo = jnp.broadcast_to(lse[(A_HPG - 1) * BLK:], (BLK, A_OUT))
    for hh in range(A_HPG - 2, -1, -1):
        sel = lane_head == hh
        o = jnp.where(sel, o4[hh * BLK:(hh + 1) * BLK], o)
        lo = jnp.where(sel, jnp.broadcast_to(lse[hh * BLK:(hh + 1) * BLK], (BLK, A_OUT)), lo)
    o_ref[...] = o
    l_ref[...] = lo


def _attn_prompt_body(*refs):
    i = pl.program_id(1)
    ins, outs = refs[:15], refs[15:]
    rows = A_HPG * BLK
    qi = lax.broadcasted_iota(jnp.int32, (rows, 2 * BLK), 0) % BLK
    kj = lax.broadcasted_iota(jnp.int32, (rows, 2 * BLK), 1)
    mask_cur = jnp.logical_and(kj >= BLK, kj - BLK <= qi)
    mask_prev = jnp.logical_and(kj < BLK, kj >= qi)
    lane_head = lax.broadcasted_iota(jnp.int32, (BLK, A_OUT), 1) // HEAD_DIM
    for gi, (_, dil) in enumerate(A_GROUPS):
        q_ref, kp_ref, kc_ref, vp_ref, vc_ref = ins[5 * gi:5 * gi + 5]
        o_ref, l_ref = outs[2 * gi:2 * gi + 2]
        mask = jnp.logical_or(mask_cur, jnp.logical_and(mask_prev, (i // dil) > 0))
        _attn_group(q_ref, kp_ref, kc_ref, vp_ref, vc_ref, o_ref, l_ref, mask, lane_head)


def _attn_prompt(qkv, B, S):
    nblk = S // BLK
    args, in_specs, out_specs, out_shape = [], [], [], []
    for gi, (_, dil) in enumerate(A_GROUPS):
        assert S % (dil * BLK) == 0
        rows = S // dil
        cur = lambda b, i, dil=dil: (b, i // dil, i % dil)
        prev = lambda b, i, dil=dil: (b, jnp.maximum(i // dil - 1, 0), i % dil)
        blk = (None, BLK, A_OUT)
        view = lambda t: t.reshape(B, rows, dil * A_OUT)
        qv, kv, vv = view(qkv[gi]), view(qkv[3 + gi]), view(qkv[6 + gi])
        args += [qv, kv, kv, vv, vv]
        in_specs += [pl.BlockSpec(blk, cur), pl.BlockSpec(blk, prev), pl.BlockSpec(blk, cur),
                     pl.BlockSpec(blk, prev), pl.BlockSpec(blk, cur)]
        out_specs += [pl.BlockSpec(blk, cur)] * 2
        out_shape += [jax.ShapeDtypeStruct((B, rows, dil * A_OUT), F32)] * 2
    res = pl.pallas_call(
        _attn_prompt_body, grid=(B, nblk), in_specs=in_specs, out_specs=out_specs,
        out_shape=out_shape, compiler_params=_cp("parallel", "parallel"), name="attn_prompt",
    )(*args)
    return [r.reshape(-1, r.shape[-1]) for r in res]


def _attn_step_body(q_ref, k_ref, v_ref, c1_ref, c2_ref, c3_ref, *outs):
    q_cols = jnp.broadcast_to(q_ref[...], (LANES, A_WIDTH)).T
    lane_head = lax.broadcasted_iota(jnp.int32, (1, A_OUT), 1) // HEAD_DIM
    for gi, (c_ref, (_, dil)) in enumerate(zip((c1_ref, c2_ref, c3_ref), A_GROUPS)):
        o_ref, l_ref = outs[2 * gi:2 * gi + 2]
        W = c_ref.shape[-1]
        pos = lax.broadcasted_iota(jnp.int32, (1, W), 1)
        valid = (pos % dil) == 0
        o_cols, w_new, lse = [], 0.0, 0.0
        for hh in range(A_HPG):
            sl = slice(gi * A_OUT + hh * HEAD_DIM, gi * A_OUT + (hh + 1) * HEAD_DIM)
            s_c = jnp.sum(c_ref[0, hh] * q_cols[sl, 0:1], axis=0, keepdims=True)
            s_c = jnp.where(valid, s_c, -jnp.inf)
            s_n = jnp.sum(q_ref[:, sl] * k_ref[:, sl], axis=-1, keepdims=True)
            m = jnp.maximum(jnp.max(s_c, axis=-1, keepdims=True), s_n)
            e_c = jnp.exp(s_c - m)
            e_n = jnp.exp(s_n - m)
            den = jnp.sum(e_c, axis=-1, keepdims=True) + e_n
            o_cols.append(jnp.sum(c_ref[1, hh] * (e_c / den), axis=-1, keepdims=True))
            w_new = jnp.where(lane_head == hh, e_n / den, w_new)
            lse = jnp.where(lane_head == hh, m + jnp.log(den), lse)
        o_col = jnp.concatenate(o_cols, axis=0)
        o_row = jnp.broadcast_to(o_col, (A_OUT, LANES)).T[0:1]
        o_ref[...] = o_row + w_new * v_ref[:, gi * A_OUT:(gi + 1) * A_OUT]
        l_ref[...] = lse


def _attn_step(q, k, v, caches, l):
    Bd = q.shape[0]
    row = lambda b: (b, 0, 0)
    args = [q.reshape(Bd, 1, A_WIDTH), k.reshape(Bd, 1, A_WIDTH), v.reshape(Bd, 1, A_WIDTH)]
    in_specs = [pl.BlockSpec((None, 1, A_WIDTH), row)] * 3
    for (window, dil), c in zip(A_GROUPS, caches):
        assert c.shape[2] == window and window % dil == 0
        args.append(jnp.transpose(c, (0, 1, 3, 4, 5, 2)))
        in_specs.append(pl.BlockSpec((None, None, 2, A_HPG, HEAD_DIM, window), lambda b: (l, b, 0, 0, 0, 0)))
    res = pl.pallas_call(
        _attn_step_body, grid=(Bd,), in_specs=in_specs,
        out_specs=[pl.BlockSpec((None, 1, A_OUT), row)] * 6,
        out_shape=[jax.ShapeDtypeStruct((Bd, 1, A_OUT), F32)] * 6,
        compiler_params=_cp("parallel"), name="attn_step",
    )(*args)
    return [r.reshape(Bd, A_OUT) for r in res]


def _lru_gates(xc, wa_ref, ba_ref, wx_ref, bx_ref, lam_ref):
    r = _sigmoid(_mm(xc, wa_ref[...]) + ba_ref[...])
    i = _sigmoid(_mm(xc, wx_ref[...]) + bx_ref[...])
    log_a = -LRU_C * r * jax.nn.softplus(-lam_ref[...])
    a = jnp.exp(log_a)
    b = jnp.sqrt(_neg_expm1(2.0 * log_a)) * (i * xc)
    return a, b


def _scan_rows(a, b, h0):
    T = a.shape[0]
    row = lax.broadcasted_iota(jnp.int32, a.shape, 0) % SUBLANES
    k = 1
    while k < SUBLANES:
        keep = row >= k
        a_s = jnp.where(keep, pltpu.roll(a, k, 0), 1.0)
        b_s = jnp.where(keep, pltpu.roll(b, k, 0), 0.0)
        b = a * b_s + b
        a = a * a_s
        k *= 2
    out, carry = [], h0
    for j in range(T // SUBLANES):
        rows = slice(j * SUBLANES, (j + 1) * SUBLANES)
        hj = b[rows] + a[rows] * carry
        out.append(hj)
        carry = hj[SUBLANES - 1:SUBLANES]
    return jnp.concatenate(out, axis=0)


def _lru_body(x_ref, g1_ref, w_ref, cw_ref, cb_ref, wa_ref, ba_ref, wx_ref, bx_ref, lam_ref,
              ob_ref, tail_ref, ext_ref, hc_ref, *, T):
    @pl.when(pl.program_id(1) == 0)
    def _():
        ext_ref[0:SUBLANES, :] = jnp.zeros((SUBLANES, LRU_WIDTH), F32)
        hc_ref[...] = jnp.zeros_like(hc_ref)

    h = _rms(x_ref[...], g1_ref[...])
    xb = _mm(h, w_ref[...])
    ext_ref[SUBLANES:SUBLANES + T, :] = xb
    xc = cb_ref[...]
    for kk in range(CONV_W - 1):
        xc = xc + ext_ref[pl.ds(SUBLANES - (CONV_W - 1) + kk, T), :] * cw_ref[kk:kk + 1, :]
    xc = xc + xb * cw_ref[CONV_W - 1:CONV_W, :]
    tail = xb[T - SUBLANES:T]
    ext_ref[0:SUBLANES, :] = tail
    tail_ref[...] = tail

    a, b = _lru_gates(xc, wa_ref, ba_ref, wx_ref, bx_ref, lam_ref)
    hfull = _scan_rows(a, b, hc_ref[0:1, :])
    ob_ref[...] = hfull
    hc_ref[...] = jnp.broadcast_to(hfull[T - 1:T], hc_ref.shape)


def _lru_prompt(x, W, l, T):
    B, S, _ = x.shape
    tile = lambda b, s: (b, s, 0)
    names = ("norm1_g", None, "conv_b_w", "conv_b_b", "lru_wa_bd", "lru_ba", "lru_wx_bd", "lru_bx", "lru_lambda")
    specs = [_lspec(W[n], l) if n else _lspec(W["w_in"], l, (D_MODEL, LRU_WIDTH), (0, IN_OFFS[3] // LRU_WIDTH))
             for n in names]
    return pl.pallas_call(
        functools.partial(_lru_body, T=T),
        grid=(B, S // T),
        in_specs=[pl.BlockSpec((None, T, D_MODEL), tile)] + specs,
        out_specs=[pl.BlockSpec((None, T, LRU_WIDTH), tile),
                   pl.BlockSpec((None, SUBLANES, LRU_WIDTH), lambda b, s: (b, 0, 0))],
        out_shape=[jax.ShapeDtypeStruct((B, S, LRU_WIDTH), F32),
                   jax.ShapeDtypeStruct((B, SUBLANES, LRU_WIDTH), F32)],
        scratch_shapes=[pltpu.VMEM((T + SUBLANES, LRU_WIDTH), F32),
                        pltpu.VMEM((SUBLANES, LRU_WIDTH), F32)],
        compiler_params=_cp("parallel", "arbitrary"), name="lru_prompt",
    )(x, *[W[n] if n else W["w_in"] for n in names])


def _gelu_ln(uv, lg_ref, lb_ref):
    uv = jax.nn.gelu(uv)
    u = uv[:, :SGU_WIDTH]
    v = uv[:, SGU_WIDTH:]
    mu = jnp.mean(v, axis=-1, keepdims=True)
    var = jnp.mean(jnp.square(v - mu), axis=-1, keepdims=True)
    v = (v - mu) * lax.rsqrt(var + EPS) * lg_ref[...] + lb_ref[...]
    return u, v


def _sgu_body(x_ref, g1_ref, w_ref, lg_ref, lb_ref, ws_ref, bs_ref, oc_ref, *, T):
    h = _rms(x_ref[...], g1_ref[...])
    u, v = _gelu_ln(_mm(h, w_ref[...]), lg_ref, lb_ref)
    qi = lax.broadcasted_iota(jnp.int32, (SGU_CHUNK, SGU_CHUNK), 0)
    kj = lax.broadcasted_iota(jnp.int32, (SGU_CHUNK, SGU_CHUNK), 1)
    tril = (kj <= qi).astype(F32)
    lane = lax.broadcasted_iota(jnp.int32, (SGU_CHUNK, SGU_WIDTH), 1)
    gw = SGU_WIDTH // SGU_GROUPS
    wms = [(ws_ref[g] * tril).astype(BF16) for g in range(SGU_GROUPS)]
    for c in range(T // SGU_CHUNK):
        rows = slice(c * SGU_CHUNK, (c + 1) * SGU_CHUNK)
        vc = v[rows].astype(BF16)
        mixed = jnp.dot(wms[SGU_GROUPS - 1], vc, preferred_element_type=F32)
        for g in range(SGU_GROUPS - 2, -1, -1):
            mixed = jnp.where(lane < (g + 1) * gw, jnp.dot(wms[g], vc, preferred_element_type=F32), mixed)
        oc_ref[rows, :] = u[rows] * (mixed + bs_ref[...])


def _sgu_prompt(x2d, W, l, T):
    R = x2d.shape[0]
    row = lambda i: (i, 0)
    return pl.pallas_call(
        functools.partial(_sgu_body, T=T),
        grid=(R // T,),
        in_specs=[pl.BlockSpec((T, D_MODEL), row), _lspec(W["norm1_g"], l),
                  _lspec(W["w_in"], l, (D_MODEL, 2 * SGU_WIDTH), (0, IN_OFFS[4] // (2 * SGU_WIDTH))),
                  _lspec(W["sgu_ln_g"], l), _lspec(W["sgu_ln_b"], l), _lspec(W["sgu_w"], l),
                  _lspec(W["sgu_b_tab"], l)],
        out_specs=pl.BlockSpec((T, SGU_WIDTH), row),
        out_shape=jax.ShapeDtypeStruct((R, SGU_WIDTH), F32),
        compiler_params=_cp("parallel"), name="sgu_prompt",
    )(x2d, W["norm1_g"], W["w_in"], W["sgu_ln_g"], W["sgu_ln_b"], W["sgu_w"], W["sgu_b_tab"])


def _ssd_dt_a(dtr, dtb_ref, alog_ref):
    lane = lax.broadcasted_iota(jnp.int32, (1, LANES), 1)
    dt = jax.nn.softplus(dtr + dtb_ref[...])
    A = jnp.where(lane < SSD_HEADS, -jnp.exp(alog_ref[...]), 0.0)
    return dt, A * dt


def _ssd_gate_norm(y, xs, z, dsk_ref, ng_ref):
    y = y + dsk_ref[...] * xs
    y = y * _silu(z)
    gw = SSD_INNER // SSD_GROUPS
    parts = []
    for g in range(SSD_GROUPS):
        yg = y[:, g * gw:(g + 1) * gw]
        parts.append(yg * lax.rsqrt(jnp.mean(yg * yg, axis=-1, keepdims=True) + EPS))
    return jnp.concatenate(parts, axis=1) * ng_ref[...]


def _ssd_chunk(xs, dt, a, Bm, Cm, st_ref, y_ref, row0):
    Q = SSD_CHUNK
    qi = lax.broadcasted_iota(jnp.int32, (Q, Q), 0)
    kj = lax.broadcasted_iota(jnp.int32, (Q, Q), 1)
    tril = kj <= qi
    cs = jnp.dot(tril.astype(F32), a, preferred_element_type=F32, precision=HIGHEST)
    cs_t = cs.T
    ecs = jnp.exp(cs)
    cs_last = cs[Q - 1:Q, :]
    to_end = jnp.exp(cs_last - cs)
    e_last = jnp.exp(cs_last)
    hpg = SSD_HEADS // SSD_GROUPS
    for g in range(SSD_GROUPS):
        Cg = Cm[:, g * SSD_STATE:(g + 1) * SSD_STATE].astype(BF16)
        Bg = Bm[:, g * SSD_STATE:(g + 1) * SSD_STATE].astype(BF16)
        G = _bdot_nt(Cg, Bg)
        for hh in range(hpg):
            hd = g * hpg + hh
            sl = slice(hd * HEAD_DIM, (hd + 1) * HEAD_DIM)
            Lh = jnp.exp(jnp.where(tril, cs[:, hd:hd + 1] - cs_t[hd:hd + 1, :], -jnp.inf))
            Xh = xs[:, sl] * dt[:, hd:hd + 1]
            Sp = st_ref[sl, :]
            y = _bdot(G * Lh, Xh) + ecs[:, hd:hd + 1] * _bdot_nt(Cg, Sp)
            y_ref[row0:row0 + Q, sl] = y
            st_ref[sl, :] = e_last[:, hd:hd + 1] * Sp + _bdot_tn(Xh * to_end[:, hd:hd + 1], Bg)


def _ssd_body(x_ref, g1_ref, wz_ref, wxbc_ref, wdt_ref, cw_ref, cb_ref, dtb_ref, alog_ref, dsk_ref,
              ng_ref, od_ref, tail_ref, fin_ref, ext_ref, st_ref, y_ref, *, T):
    @pl.when(pl.program_id(1) == 0)
    def _():
        ext_ref[0:SUBLANES, :] = jnp.zeros((SUBLANES, SSD_CONV_CH), F32)
        st_ref[...] = jnp.zeros_like(st_ref)

    h = _rms(x_ref[...], g1_ref[...]).astype(BF16)
    z = _mm(h, wz_ref[...])
    xbc = _mm(h, wxbc_ref[...])
    dtr = _mm(h, wdt_ref[...])
    ext_ref[SUBLANES:SUBLANES + T, :] = xbc
    xc = cb_ref[...]
    for kk in range(CONV_W - 1):
        xc = xc + ext_ref[pl.ds(SUBLANES - (CONV_W - 1) + kk, T), :] * cw_ref[kk:kk + 1, :]
    xc = xc + xbc * cw_ref[CONV_W - 1:CONV_W, :]
    tail = xbc[T - SUBLANES:T]
    ext_ref[0:SUBLANES, :] = tail
    tail_ref[...] = tail

    xc = _silu(xc)
    xs = xc[:, :SSD_INNER]
    gn = SSD_GROUPS * SSD_STATE
    Bm = xc[:, SSD_INNER:SSD_INNER + gn]
    Cm = xc[:, SSD_INNER + gn:]
    dt, a = _ssd_dt_a(dtr, dtb_ref, alog_ref)
    for c in range(T // SSD_CHUNK):
        rows = slice(c * SSD_CHUNK, (c + 1) * SSD_CHUNK)
        _ssd_chunk(xs[rows], dt[rows], a[rows], Bm[rows], Cm[rows], st_ref, y_ref, c * SSD_CHUNK)
    od_ref[...] = _ssd_gate_norm(y_ref[...], xs, z, dsk_ref, ng_ref)
    fin_ref[...] = st_ref[...]


def _ssd_prompt(x, W, l, T):
    B, S, _ = x.shape
    tile = lambda b, s: (b, s, 0)
    perb = lambda b, s: (b, 0, 0)
    wx = W["w_xbcdt"]
    return pl.pallas_call(
        functools.partial(_ssd_body, T=T),
        grid=(B, S // T),
        in_specs=[pl.BlockSpec((None, T, D_MODEL), tile), _lspec(W["norm1_g"], l),
                  _lspec(W["w_in"], l, (D_MODEL, SSD_INNER), (0, IN_OFFS[5] // SSD_INNER)),
                  _lspec(wx, l, (D_MODEL, SSD_CONV_CH), (0, 0)),
                  _lspec(wx, l, (D_MODEL, LANES), (0, SSD_CONV_CH // LANES)),
                  _lspec(W["ssd_conv_w"], l), _lspec(W["ssd_conv_b"], l), _lspec(W["ssd_dt_bias"], l),
                  _lspec(W["ssd_a_log"], l), _lspec(W["ssd_d_tab"], l), _lspec(W["ssd_norm_g"], l)],
        out_specs=[pl.BlockSpec((None, T, SSD_INNER), tile),
                   pl.BlockSpec((None, SUBLANES, SSD_CONV_CH), perb),
                   pl.BlockSpec((None, SSD_INNER, SSD_STATE), perb)],
        out_shape=[jax.ShapeDtypeStruct((B, S, SSD_INNER), F32),
                   jax.ShapeDtypeStruct((B, SUBLANES, SSD_CONV_CH), F32),
                   jax.ShapeDtypeStruct((B, SSD_INNER, SSD_STATE), F32)],
        scratch_shapes=[pltpu.VMEM((T + SUBLANES, SSD_CONV_CH), F32),
                        pltpu.VMEM((SSD_INNER, SSD_STATE), F32),
                        pltpu.VMEM((T, SSD_INNER), F32)],
        compiler_params=_cp("parallel", "arbitrary"), name="ssd_prompt",
    )(x, W["norm1_g"], W["w_in"], wx, wx, W["ssd_conv_w"], W["ssd_conv_b"],
      W["ssd_dt_bias"], W["ssd_a_log"], W["ssd_d_tab"], W["ssd_norm_g"])


def _proj_body(x_ref, g1_ref, w_ref, o_ref):
    o_ref[...] = _mm(_rms(x_ref[...], g1_ref[...]), w_ref[...])


def _proj(x2d, W, l, name, col0, ncols, tn):
    R = x2d.shape[0]
    assert col0 % tn == 0
    j0 = col0 // tn
    nblk = pl.cdiv(ncols, tn)
    return pl.pallas_call(
        _proj_body, grid=(nblk,),
        in_specs=[_const_spec((R, D_MODEL)), _lspec(W["norm1_g"], l),
                  pl.BlockSpec((None, D_MODEL, tn), lambda j: (l, 0, j0 + j))],
        out_specs=pl.BlockSpec((R, tn), lambda j: (0, j)),
        out_shape=jax.ShapeDtypeStruct((R, nblk * tn), F32),
        compiler_params=_cp("parallel"), name="proj_rest",
    )(x2d, W["norm1_g"], W[name])


def _conv_step(xnew, st_ref, cw_ref, cb_ref, C):
    out = cb_ref[...]
    for kk in range(CONV_W - 1):
        out = out + st_ref[:, kk * C:(kk + 1) * C] * cw_ref[kk:kk + 1, :]
    return out + xnew * cw_ref[CONV_W - 1:CONV_W, :]


def _dec_branches_body(proj_ref, dtr_ref, scb_ref, hb_ref, scd_ref,
                       cwb_ref, cbb_ref, wa_ref, ba_ref, wx_ref, bx_ref, lam_ref,
                       lg_ref, lb_ref, w00_ref, b0_ref,
                       cwd_ref, cbd_ref, dtb_ref, alog_ref,
                       ob_ref, ncb_ref, oc_ref, vc_ref, ncd_ref, xs_ref, xdt_ref, ea_ref, bc_ref,
                       *, offs):
    o_xb, o_uv, o_xbc = offs
    xb = proj_ref[:, o_xb:o_xb + LRU_WIDTH]
    xc = _conv_step(xb, scb_ref, cwb_ref, cbb_ref, LRU_WIDTH)
    a, b = _lru_gates(xc, wa_ref, ba_ref, wx_ref, bx_ref, lam_ref)
    ob_ref[...] = a * hb_ref[...] + b
    ncb_ref[:, 0:2 * LRU_WIDTH] = scb_ref[:, LRU_WIDTH:3 * LRU_WIDTH]
    ncb_ref[:, 2 * LRU_WIDTH:3 * LRU_WIDTH] = xb
    u, v = _gelu_ln(proj_ref[:, o_uv:o_uv + 2 * SGU_WIDTH], lg_ref, lb_ref)
    vc_ref[...] = v
    oc_ref[...] = u * (w00_ref[...] * v + b0_ref[...])
    xbc = proj_ref[:, o_xbc:o_xbc + SSD_CONV_CH]
    xcd = _silu(_conv_step(xbc, scd_ref, cwd_ref, cbd_ref, SSD_CONV_CH))
    ncd_ref[:, 0:2 * SSD_CONV_CH] = scd_ref[:, SSD_CONV_CH:3 * SSD_CONV_CH]
    ncd_ref[:, 2 * SSD_CONV_CH:3 * SSD_CONV_CH] = xbc
    xs = xcd[:, :SSD_INNER]
    dt, a_dt = _ssd_dt_a(dtr_ref[...], dtb_ref, alog_ref)
    hrow = lax.broadcasted_iota(jnp.int32, (LANES, SSD_INNER), 0)
    hlane = lax.broadcasted_iota(jnp.int32, (LANES, SSD_INNER), 1)
    expand = (hlane // HEAD_DIM == hrow).astype(F32)
    dt_e = jnp.dot(dt, expand, preferred_element_type=F32, precision=HIGHEST)
    a_e = jnp.dot(a_dt, expand, preferred_element_type=F32, precision=HIGHEST)
    xs_ref[...] = xs
    xdt_ref[...] = xs * dt_e
    ea_ref[...] = jnp.exp(a_e)
    bc_ref[...] = xcd[:, SSD_INNER:]


def _dec_branches(proj, dtr, offs, scb, hb, scd, W, l):
    Bd = proj.shape[0]
    f = lambda n: jax.ShapeDtypeStruct((Bd, n), F32)
    whole = lambda a: pl.BlockSpec(a.shape, lambda i: (0,) * a.ndim)
    names = ("conv_b_w", "conv_b_b", "lru_wa_bd", "lru_ba", "lru_wx_bd", "lru_bx", "lru_lambda",
             "sgu_ln_g", "sgu_ln_b", "sgu_w00_tab", "sgu_b0_tab",
             "ssd_conv_w", "ssd_conv_b", "ssd_dt_bias", "ssd_a_log")
    acts = (proj, dtr, scb, hb, scd)
    widths = (LRU_WIDTH, 3 * LRU_WIDTH, SGU_WIDTH, SGU_WIDTH, 3 * SSD_CONV_CH, SSD_INNER, SSD_INNER,
              SSD_INNER, 2 * SSD_GROUPS * SSD_STATE)
    return pl.pallas_call(
        functools.partial(_dec_branches_body, offs=offs),
        grid=(1,),
        in_specs=[whole(a) for a in acts] + [_lspec(W[n], l) for n in names],
        out_specs=[pl.BlockSpec((Bd, n), lambda i: (0, 0)) for n in widths],
        out_shape=[f(n) for n in widths],
        compiler_params=_cp("arbitrary"), name="dec_branches",
    )(*acts, *[W[n] for n in names])


def _dec_ssd_body(xs_ref, xdt_ref, ea_ref, bc_ref, z_ref, st_ref, dsk_ref, ng_ref, od_ref, ns_ref):
    gn = SSD_GROUPS * SSD_STATE
    half = SSD_INNER // SSD_GROUPS
    ridx = lax.broadcasted_iota(jnp.int32, (LANES, SSD_INNER), 0)
    rows = jnp.where(ridx == 0, jnp.broadcast_to(xdt_ref[...], (LANES, SSD_INNER)),
                     jnp.where(ridx == 1, jnp.broadcast_to(ea_ref[...], (LANES, SSD_INNER)), 0.0))
    cols = rows.T
    xdt_c = cols[:, 0:1]
    ea_c = cols[:, 1:2]
    bc = bc_ref[...]
    Bm, Cm = bc[:, :gn], bc[:, gn:]
    rowi = lax.broadcasted_iota(jnp.int32, (SSD_INNER, SSD_STATE), 0)
    b_full = jnp.where(rowi < half, jnp.broadcast_to(Bm[:, :SSD_STATE], (SSD_INNER, SSD_STATE)),
                       jnp.broadcast_to(Bm[:, SSD_STATE:], (SSD_INNER, SSD_STATE)))
    new = ea_c * st_ref[...] + xdt_c * b_full
    ns_ref[...] = new
    cidx = lax.broadcasted_iota(jnp.int32, (SUBLANES, SSD_STATE), 0)
    c8 = jnp.where(cidx == 0, jnp.broadcast_to(Cm[:, :SSD_STATE], (SUBLANES, SSD_STATE)),
                   jnp.where(cidx == 1, jnp.broadcast_to(Cm[:, SSD_STATE:], (SUBLANES, SSD_STATE)), 0.0))
    y8 = lax.dot_general(c8, new, (((1,), (1,)), ((), ())), preferred_element_type=F32,
                         precision=HIGHEST)
    lane = lax.broadcasted_iota(jnp.int32, (1, SSD_INNER), 1)
    y = jnp.where(lane < half, y8[0:1], y8[1:2])
    od_ref[...] = _ssd_gate_norm(y, xs_ref[...], z_ref[...], dsk_ref, ng_ref)


def _dec_ssd(xs, xdt, ea, bc, z, state_all, W, l):
    Bd = xs.shape[0]
    row = lambda b: (b, 0, 0)
    r3 = lambda t: t.reshape(Bd, 1, t.shape[-1])
    st = state_all.reshape(state_all.shape[0], Bd, SSD_INNER, SSD_STATE)
    od, ns = pl.pallas_call(
        _dec_ssd_body, grid=(Bd,),
        in_specs=[pl.BlockSpec((None, 1, SSD_INNER), row)] * 3
                 + [pl.BlockSpec((None, 1, 2 * SSD_GROUPS * SSD_STATE), row),
                    pl.BlockSpec((None, 1, SSD_INNER), row),
                    pl.BlockSpec((None, None, SSD_INNER, SSD_STATE), lambda b: (l, b, 0, 0)),
                    _lspec(W["ssd_d_tab"], l), _lspec(W["ssd_norm_g"], l)],
        out_specs=[pl.BlockSpec((None, 1, SSD_INNER), row),
                   pl.BlockSpec((None, SSD_INNER, SSD_STATE), row)],
        out_shape=[jax.ShapeDtypeStruct((Bd, 1, SSD_INNER), F32),
                   jax.ShapeDtypeStruct((Bd, SSD_INNER, SSD_STATE), F32)],
        compiler_params=_cp("parallel"), name="dec_ssd",
    )(r3(xs), r3(xdt), r3(ea), r3(bc), r3(z), st, W["ssd_d_tab"], W["ssd_norm_g"])
    return od.reshape(Bd, SSD_INNER), ns.reshape(state_all.shape[1:])


def _merge_body(x_ref, o1_ref, l1_ref, o2_ref, l2_ref, o3_ref, l3_ref, ob_ref, oc_ref, od_ref,
                g1_ref, wg_ref, wpa_ref, wpb_ref, wpc_ref, wpd_ref, wo_ref, out_ref, *slab, dils, pre_gates):
    x = x_ref[...]
    h = None if pre_gates else _rms(x, g1_ref[...]).astype(wg_ref.dtype)
    tok = lambda ref, d: _load_strided_view(ref, slab[0], d) if d > 1 else ref[...]
    o1, o2, o3 = (tok(r, d) for r, d in zip((o1_ref, o2_ref, o3_ref), dils))
    l1, l2, l3 = (tok(r, d) for r, d in zip((l1_ref, l2_ref, l3_ref), dils))
    m = jnp.maximum(jnp.maximum(l1, l2), l3)
    e1, e2, e3 = jnp.exp(l1 - m), jnp.exp(l2 - m), jnp.exp(l3 - m)
    den = e1 + e2 + e3
    oa = (e1 / den) * o1 + (e2 / den) * o2 + (e3 / den) * o3
    merged = None
    for bi, (o, w_ref) in enumerate(((oa, wpa_ref), (ob_ref[...], wpb_ref),
                                     (oc_ref[...], wpc_ref), (od_ref[...], wpd_ref))):
        cols = slice(bi * D_MODEL, (bi + 1) * D_MODEL)
        gate = _sigmoid(wg_ref[:, cols] if pre_gates else _mm(h, wg_ref[:, cols]))
        term = gate * _mm(o, w_ref[...])
        merged = term if merged is None else merged + term
    out_ref[...] = x + _mm(merged, wo_ref[...])


def _merge(x2d, att, ob, oc, od, W, l, tm, dils, gates=None):
    R = x2d.shape[0]
    row = lambda i: (i, 0)
    names = ("norm1_g", "w_gates", "w_pa", "w_pb", "w_pc", "w_pd", "w_o")
    params = [W[n] for n in names]
    pspecs = [_lspec(a, l) for a in params]
    if gates is not None:
        params[1], pspecs[1] = gates, pl.BlockSpec((tm, N_BRANCH * D_MODEL), row)
    att_specs = [pl.BlockSpec((tm // d, d * A_OUT), row) for d in dils for _ in range(2)]
    scratch = [pltpu.VMEM((A_OUT // LANES, tm, LANES), F32)] if max(dils) > 1 else []
    return pl.pallas_call(
        functools.partial(_merge_body, dils=dils, pre_gates=gates is not None), grid=(R // tm,),
        in_specs=[pl.BlockSpec((tm, D_MODEL), row)] + att_specs
                 + [pl.BlockSpec((tm, LRU_WIDTH), row)] * 3 + pspecs,
        out_specs=pl.BlockSpec((tm, D_MODEL), row),
        out_shape=jax.ShapeDtypeStruct((R, D_MODEL), F32), scratch_shapes=scratch,
        compiler_params=_cp("parallel"), name="merge",
    )(x2d, *att, ob, oc, od, *params)


def _route(logits):
    lane = lax.broadcasted_iota(jnp.int32, logits.shape, 1)
    big = jnp.int32(LANES)
    ninf = -jnp.inf
    gl = jnp.where(lane < N_EXPERT_GROUPS, logits, ninf)
    gm = jnp.max(gl, axis=-1, keepdims=True)
    gsel = jnp.min(jnp.where(gl == gm, lane, big), axis=-1, keepdims=True)
    pg = 1.0 / jnp.sum(jnp.exp(gl - gm), axis=-1, keepdims=True)
    lo = N_EXPERT_GROUPS + EXPERTS_PER_GROUP * gsel
    el = jnp.where(jnp.logical_and(lane >= lo, lane < lo + EXPERTS_PER_GROUP), logits, ninf)
    t1 = jnp.max(el, axis=-1, keepdims=True)
    i1 = jnp.min(jnp.where(el == t1, lane, big), axis=-1, keepdims=True)
    el2 = jnp.where(lane == i1, ninf, el)
    t2 = jnp.max(el2, axis=-1, keepdims=True)
    i2 = jnp.min(jnp.where(el2 == t2, lane, big), axis=-1, keepdims=True)
    e2 = jnp.exp(t2 - t1)
    den = 1.0 + e2
    w1 = (1.0 / den) * pg
    w2 = (e2 / den) * pg
    return jnp.where(lane == i1, w1, 0.0) + jnp.where(lane == i2, w2, 0.0), gsel


def _moe_body(x_ref, g2_ref, wr_ref, br_ref, wg_ref, wu_ref, wd_ref, out_ref, h_ref, comb_ref, acc_ref):
    e = pl.program_id(1)

    @pl.when(e == 0)
    def _():
        h = _rms(x_ref[...], g2_ref[...])
        h_ref[...] = h.astype(h_ref.dtype)
        logits = jnp.dot(h, wr_ref[...], preferred_element_type=F32, precision=HIGHEST) + br_ref[...]
        comb_ref[...] = _route(logits)[0]
        acc_ref[...] = jnp.zeros_like(acc_ref)

    h = h_ref[...]
    lane = lax.broadcasted_iota(jnp.int32, comb_ref.shape, 1)
    c = jnp.sum(jnp.where(lane == N_EXPERT_GROUPS + e, comb_ref[...], 0.0), axis=-1, keepdims=True)
    hg = _mm(h, wg_ref[...])
    hu = _mm(h, wu_ref[...])
    act = _silu(hg) * hu * c
    acc_ref[...] += _mm(act, wd_ref[...])

    @pl.when(e == N_EXPERTS - 1)
    def _():
        out_ref[...] = x_ref[...] + acc_ref[...]


def _moe(x2d, W, l, tm):
    R = x2d.shape[0]
    row = lambda i, e: (i, 0)
    expert = lambda r, c: pl.BlockSpec((None, None, r, c), lambda i, e: (l, e, 0, 0))
    return pl.pallas_call(
        _moe_body, grid=(R // tm, N_EXPERTS),
        in_specs=[pl.BlockSpec((tm, D_MODEL), row), _lspec(W["norm2_g"], l), _lspec(W["w_router"], l),
                  _lspec(W["b_router"], l),
                  expert(D_MODEL, D_EXPERT), expert(D_MODEL, D_EXPERT), expert(D_EXPERT, D_MODEL)],
        out_specs=pl.BlockSpec((tm, D_MODEL), row),
        out_shape=jax.ShapeDtypeStruct((R, D_MODEL), F32),
        scratch_shapes=[pltpu.VMEM((tm, D_MODEL), W["moe_w_gate"].dtype), pltpu.VMEM((tm, LANES), F32),
                        pltpu.VMEM((tm, D_MODEL), F32)],
        compiler_params=_cp("parallel", "arbitrary"), name="moe",
    )(x2d, W["norm2_g"], W["w_router"], W["b_router"], W["moe_w_gate"], W["moe_w_up"],
      W["moe_w_down"])


GID_LANE = N_EXPERT_GROUPS + N_EXPERTS
SEG_ALIGN = 16
MOE_RB = TM_MOE // N_EXPERT_GROUPS + 32
PERM_BLK = 128
UNPERM_BLK = 256
MOE_USED = -(-(TM_MOE + N_EXPERT_GROUPS * SEG_ALIGN) // PERM_BLK) * PERM_BLK
MOE_TMP = MOE_USED + MOE_RB
assert MOE_RB % SEG_ALIGN == 0 and TM_MOE % UNPERM_BLK == 0


def _route_body(x_ref, g2_ref, wr_ref, br_ref, comb_ref, cnt_ref):
    h = _rms(x_ref[...], g2_ref[...])
    logits = jnp.dot(h, wr_ref[...], preferred_element_type=F32, precision=HIGHEST) + br_ref[...]
    comb, gsel = _route(logits)
    lane = lax.broadcasted_iota(jnp.int32, comb.shape, 1)
    comb_ref[...] = jnp.where(lane == GID_LANE, gsel.astype(F32), comb)
    counts = jnp.sum((lane == gsel).astype(F32), axis=0, keepdims=True)
    cnt_ref[...] = jnp.broadcast_to(counts, cnt_ref.shape)


def _route_call(x2d, W, l, tm):
    R = x2d.shape[0]
    return pl.pallas_call(
        _route_body, grid=(R // tm,),
        in_specs=[pl.BlockSpec((tm, D_MODEL), lambda i: (i, 0)), _lspec(W["norm2_g"], l),
                  _lspec(W["w_router"], l), _lspec(W["b_router"], l)],
        out_specs=[pl.BlockSpec((tm, LANES), lambda i: (i, 0)),
                   pl.BlockSpec((None, SUBLANES, LANES), lambda i: (i, 0, 0))],
        out_shape=[jax.ShapeDtypeStruct((R, LANES), F32),
                   jax.ShapeDtypeStruct((R // tm, SUBLANES, LANES), F32)],
        compiler_params=_cp("parallel"), name="moe_route",
    )(x2d, W["norm2_g"], W["w_router"], W["b_router"])


def _moe_sorted_body(off_ref, nch_ref, x_ref, comb_ref, g2_ref, wg_ref, wu_ref, wd_ref, out_ref,
                     sh_ref, sc_ref, acc_ref, rcol_ref, lo_ref, *, TM, RB):
    i = pl.program_id(0)
    g = pl.program_id(1)
    TMP = sh_ref.shape[0]

    @pl.when(g == 0)
    def _():
        h2 = _rms(x_ref[...], g2_ref[...]).astype(BF16)
        comb = comb_ref[...]
        lane = lax.broadcasted_iota(jnp.int32, comb.shape, 1)
        gid = jnp.sum(jnp.where(lane == GID_LANE, comb, 0.0), axis=-1, keepdims=True)
        onehot = jnp.logical_and(lane < N_EXPERT_GROUPS, lane.astype(F32) == gid).astype(F32)
        trow = lax.broadcasted_iota(jnp.int32, onehot.shape, 0)
        run = onehot
        k = 1
        while k < TM:
            run = run + jnp.where(trow >= k, pltpu.roll(run, k, 0), 0.0)
            k *= 2
        prefix = run - onehot
        lane1 = lax.broadcasted_iota(jnp.int32, (1, LANES), 1)
        offv = jnp.zeros((1, LANES), F32)
        for gg in range(N_EXPERT_GROUPS):
            offv = jnp.where(lane1 == gg, off_ref[i * N_EXPERT_GROUPS + gg].astype(F32), offv)
        rank = jnp.sum(onehot * (offv + prefix), axis=-1, keepdims=True)
        rank_b = jnp.broadcast_to(rank, (TM, LANES))
        rcol_ref[...] = rank_b
        rank_row = rank_b.T[0:1]
        c1 = comb.astype(BF16)
        c2 = (comb - c1.astype(F32)).astype(BF16)
        payload = jnp.concatenate([h2, c1, c2], axis=1)
        for blk in range(MOE_USED // PERM_BLK):
            rows = slice(blk * PERM_BLK, (blk + 1) * PERM_BLK)
            srow = lax.broadcasted_iota(jnp.int32, (PERM_BLK, TM), 0) + blk * PERM_BLK
            perm = (srow.astype(F32) == rank_row).astype(BF16)
            moved = jnp.dot(perm, payload, preferred_element_type=F32)
            sh_ref[rows, :] = moved[:, :D_MODEL].astype(BF16)
            sc_ref[rows, :] = moved[:, D_MODEL:D_MODEL + LANES] + moved[:, D_MODEL + LANES:]
        sh_ref[MOE_USED:, :] = jnp.zeros((TMP - MOE_USED, D_MODEL), BF16)
        sc_ref[MOE_USED:, :] = jnp.zeros((TMP - MOE_USED, LANES), F32)
        acc_ref[...] = jnp.zeros_like(acc_ref)

    off = off_ref[i * N_EXPERT_GROUPS + g]
    lane_c = lax.broadcasted_iota(jnp.int32, (RB, LANES), 1)

    def chunk(j, carry):
        r0 = pl.multiple_of(off + j * RB, SEG_ALIGN)
        rows = sh_ref[pl.ds(r0, RB), :]
        cc = sc_ref[pl.ds(r0, RB), :]
        y = jnp.zeros((RB, D_MODEL), F32)
        for e in range(EXPERTS_PER_GROUP):
            ce = jnp.sum(jnp.where(lane_c == N_EXPERT_GROUPS + EXPERTS_PER_GROUP * g + e, cc, 0.0),
                         axis=-1, keepdims=True)
            hg = jnp.dot(rows, wg_ref[e], preferred_element_type=F32)
            hu = jnp.dot(rows, wu_ref[e], preferred_element_type=F32)
            y = y + _bdot(_silu(hg) * hu * ce, wd_ref[e])
        acc_ref[pl.ds(r0, RB), :] += y
        return carry

    lax.fori_loop(0, nch_ref[i * N_EXPERT_GROUPS + g], chunk, 0)

    @pl.when(g == N_EXPERT_GROUPS - 1)
    def _():
        for blk in range(MOE_USED // PERM_BLK):
            rows = slice(blk * PERM_BLK, (blk + 1) * PERM_BLK)
            a = acc_ref[rows, :]
            hi = a.astype(BF16)
            sh_ref[rows, :] = hi
            lo_ref[rows, :] = (a - hi.astype(F32)).astype(BF16)
        scol = lax.broadcasted_iota(jnp.int32, (UNPERM_BLK, MOE_USED), 1).astype(F32)
        for blk in range(TM // UNPERM_BLK):
            rows = slice(blk * UNPERM_BLK, (blk + 1) * UNPERM_BLK)
            unperm = (scol == rcol_ref[rows, 0:1]).astype(BF16)
            out_ref[rows, :] = x_ref[rows, :] + (
                jnp.dot(unperm, sh_ref[0:MOE_USED, :], preferred_element_type=F32)
                + jnp.dot(unperm, lo_ref[0:MOE_USED, :], preferred_element_type=F32))


def _moe_sorted(x2d, W, l):
    R = x2d.shape[0]
    TM, RB, TMP = TM_MOE, MOE_RB, MOE_TMP
    nt = R // TM
    comb, counts = _route_call(x2d, W, l, TM)
    cnt = counts[:, 0, :N_EXPERT_GROUPS].astype(jnp.int32)
    padded = (cnt + SEG_ALIGN - 1) // SEG_ALIGN * SEG_ALIGN
    off = (jnp.cumsum(padded, axis=1) - padded).reshape(-1)
    nch = ((cnt + RB - 1) // RB).reshape(-1)
    row = lambda i, g, *_: (i, 0)
    const = lambda a: pl.BlockSpec((None,) + tuple(a.shape[1:]), lambda i, g, *_: (l,) + (0,) * (a.ndim - 1))
    grp = lambda r, c: pl.BlockSpec((None, None, EXPERTS_PER_GROUP, r, c), lambda i, g, *_: (l, g, 0, 0, 0))
    gview = lambda a: a.reshape(a.shape[0], N_EXPERT_GROUPS, EXPERTS_PER_GROUP, a.shape[2], a.shape[3])
    return pl.pallas_call(
        functools.partial(_moe_sorted_body, TM=TM, RB=RB),
        grid_spec=pltpu.PrefetchScalarGridSpec(
            num_scalar_prefetch=2, grid=(nt, N_EXPERT_GROUPS),
            in_specs=[pl.BlockSpec((TM, D_MODEL), row, pipeline_mode=pl.Buffered(1)),
                      pl.BlockSpec((TM, LANES), row), const(W["norm2_g"]),
                      grp(D_MODEL, D_EXPERT), grp(D_MODEL, D_EXPERT), grp(D_EXPERT, D_MODEL)],
            out_specs=pl.BlockSpec((TM, D_MODEL), row),
            scratch_shapes=[pltpu.VMEM((TMP, D_MODEL), BF16), pltpu.VMEM((TMP, LANES), F32),
                            pltpu.VMEM((TMP, D_MODEL), F32), pltpu.VMEM((TM, LANES), F32),
                            pltpu.VMEM((TMP, D_MODEL), BF16)]),
        out_shape=jax.ShapeDtypeStruct((R, D_MODEL), F32),
        compiler_params=_cp("parallel", "arbitrary"), name="moe_sorted",
    )(off, nch, x2d, comb, W["norm2_g"], gview(W["moe_w_gate"]), gview(W["moe_w_up"]), gview(W["moe_w_down"]))


def _rope_tables(pos):
    half = HEAD_DIM // 2
    inv = ROPE_THETA ** (-jnp.arange(half, dtype=F32) / half)
    ang = pos.astype(F32)[:, None] * inv[None]
    c, s = jnp.cos(ang), jnp.sin(ang)
    z = jnp.zeros_like(s)
    reps = LANES // HEAD_DIM
    return (jnp.tile(jnp.concatenate([c, c], 1), (1, reps)),
            jnp.tile(jnp.concatenate([-s, z], 1), (1, reps)),
            jnp.tile(jnp.concatenate([z, s], 1), (1, reps)))


def _block_diag(w):
    L, n, k, _ = w.shape
    eye = jnp.eye(n, dtype=w.dtype)
    return (eye[None, :, None, :, None] * w[:, :, :, None, :]).reshape(L, n * k, n * k)


def _small_params(p):
    vec = lambda t: t.reshape(t.shape[0], 1, -1)
    W = {}
    for n in ("norm1_g", "norm2_g", "conv_b_b", "lru_ba", "lru_bx", "lru_lambda", "sgu_ln_g", "sgu_ln_b",
              "ssd_conv_b", "ssd_norm_g"):
        W[n] = vec(p[n])
    reps = A_WIDTH // HEAD_DIM
    W["q_norm_g"] = vec(jnp.tile(p["q_norm_g"], (1, reps)))
    W["k_norm_g"] = vec(jnp.tile(p["k_norm_g"], (1, reps)))
    W["conv_b_w"] = p["conv_b_w"]
    W["ssd_conv_w"] = p["ssd_conv_w"]
    W["sgu_w"] = p["sgu_w"]
    gw = SGU_WIDTH // SGU_GROUPS
    W["sgu_b_tab"] = jnp.repeat(jnp.swapaxes(p["sgu_b"], 1, 2), gw, axis=2)
    W["sgu_w00_tab"] = vec(jnp.repeat(p["sgu_w"][:, :, 0, 0], gw, axis=1))
    W["sgu_b0_tab"] = vec(jnp.repeat(p["sgu_b"][:, :, 0], gw, axis=1))
    padh = lambda t: vec(jnp.pad(t, ((0, 0), (0, LANES - SSD_HEADS))))
    W["ssd_dt_bias"], W["ssd_a_log"] = padh(p["ssd_dt_bias"]), padh(p["ssd_a_log"])
    W["ssd_d_tab"] = vec(jnp.repeat(p["ssd_d"], HEAD_DIM, axis=1))
    we = jnp.transpose(p["router_exp_w"], (0, 2, 1, 3)).reshape(-1, D_MODEL, N_EXPERTS)
    wr = jnp.concatenate([p["router_group_w"], we], axis=2)
    W["w_router"] = jnp.pad(wr, ((0, 0), (0, 0), (0, LANES - wr.shape[2])))
    br = jnp.concatenate([p["router_group_b"], p["router_exp_b"].reshape(-1, N_EXPERTS)], axis=1)
    W["b_router"] = vec(jnp.pad(br, ((0, 0), (0, LANES - br.shape[1]))))
    return W


def _matrix_params(p, wdt):
    W = {}
    w_in = p["w_in"].astype(wdt)
    W["w_in"] = w_in
    o_xbc, o_dt, o_g = IN_OFFS[6], IN_OFFS[7], IN_OFFS[8]
    W["w_xbcdt"] = jnp.concatenate(
        [w_in[:, :, o_xbc:o_dt], jnp.pad(w_in[:, :, o_dt:o_g], ((0, 0), (0, 0), (0, LANES - SSD_HEADS)))], axis=2)
    W["w_gates"] = w_in[:, :, o_g:]
    reps = A_WIDTH // HEAD_DIM
    W["gbd"] = _block_diag(jnp.ones((1, reps, HEAD_DIM, HEAD_DIM), wdt))[0]
    W["lru_wa_bd"] = _block_diag(p["lru_wa"]).astype(wdt)
    W["lru_wx_bd"] = _block_diag(p["lru_wx"]).astype(wdt)
    for n in ("w_pa", "w_pb", "w_pc", "w_pd", "w_o", "moe_w_gate", "moe_w_up", "moe_w_down"):
        W[n] = p[n].astype(wdt)
    return W


def _kv_rows(k, v, gi):
    sl = slice(gi * A_OUT, (gi + 1) * A_OUT)
    shp = k.shape[:-1] + (A_HPG, HEAD_DIM)
    return jnp.stack([k[..., sl].reshape(shp), v[..., sl].reshape(shp)], axis=-3)


def _prompt_layer(x, W, l, tabs):
    B, S, _ = x.shape
    x2d = x.reshape(B * S, D_MODEL)
    *qkv, k, v = _qkv(x2d, W, l, tabs, TM_QKV, split=True)
    att = _attn_prompt(qkv, B, S)
    ob, tail_b = _lru_prompt(x, W, l, T_SEQ)
    oc = _sgu_prompt(x2d, W, l, T_SGU)
    od, tail_d, fin = _ssd_prompt(x, W, l, T_SEQ)
    x2d = _merge(x2d, att, ob.reshape(B * S, -1), oc, od.reshape(B * S, -1), W, l, TM_MERGE,
                 tuple(dil for _, dil in A_GROUPS))
    x2d = _moe_sorted(x2d, W, l)
    k3, v3 = k.reshape(B, S, -1), v.reshape(B, S, -1)
    kvs = [_kv_rows(k3[:, S - min(w, S):], v3[:, S - min(w, S):], gi) for gi, (w, _) in enumerate(A_GROUPS)]
    nb = CONV_W - 1
    states = (tail_b[:, SUBLANES - nb:], ob[:, -1], tail_d[:, SUBLANES - nb:],
              fin.reshape(B, SSD_HEADS, HEAD_DIM, SSD_STATE))
    return x2d.reshape(B, S, D_MODEL), kvs, states


def _decode_layer(x2d, W, l, tabs, caches, scb, hb, scd, ssm_all):
    Bd = x2d.shape[0]
    q, k, v = _qkv(x2d, W, l, tabs, Bd, split=False)
    att = _attn_step(q, k, v, caches, l)
    col0 = IN_OFFS[3]
    proj = _proj(x2d, W, l, "w_in", col0, sum(IN_SIZES) - col0, LANES)
    dtr = proj[:, IN_OFFS[7] - col0:IN_OFFS[7] - col0 + LANES]
    gates = proj[:, IN_OFFS[8] - col0:IN_OFFS[8] - col0 + N_BRANCH * D_MODEL]
    offs = (0, IN_OFFS[4] - col0, IN_OFFS[6] - col0)
    nb = CONV_W - 1
    ob, ncb, oc, vc, ncd, xs, xdt, ea, bc = _dec_branches(
        proj, dtr, offs, scb.reshape(Bd, nb * LRU_WIDTH), hb, scd.reshape(Bd, nb * SSD_CONV_CH), W, l)
    z = proj[:, IN_OFFS[5] - col0:IN_OFFS[5] - col0 + SSD_INNER]
    od, nssm = _dec_ssd(xs, xdt, ea, bc, z, ssm_all, W, l)
    x2d = _merge(x2d, att, ob, oc, od, W, l, Bd, (1,) * len(A_GROUPS), gates=gates)
    x2d = _moe(x2d, W, l, Bd)
    kvs = [_kv_rows(k, v, gi)[:, None] for gi in range(len(A_GROUPS))]
    states = (ncb.reshape(Bd, nb, LRU_WIDTH), ob, vc[:, None], ncd.reshape(Bd, nb, SSD_CONV_CH), nssm)
    return x2d, kvs, states


def kernel(x_prompt, x_sample, cache_kv_a1, cache_kv_a2, cache_kv_a3, state_conv_b, state_h_b, state_conv_d, state_ssm_d, norm1_g, w_in, q_norm_g, k_norm_g, conv_b_w, conv_b_b, lru_wa, lru_ba, lru_wx, lru_bx, lru_lambda, sgu_ln_g, sgu_ln_b, sgu_w, sgu_b, ssd_conv_w, ssd_conv_b, ssd_dt_bias, ssd_a_log, ssd_d, ssd_norm_g, w_pa, w_pb, w_pc, w_pd, w_o, norm2_g, router_group_w, router_group_b, router_exp_w, router_exp_b, moe_w_gate, moe_w_up, moe_w_down):
    p = dict(norm1_g=norm1_g, w_in=w_in, q_norm_g=q_norm_g, k_norm_g=k_norm_g, conv_b_w=conv_b_w,
             conv_b_b=conv_b_b, lru_wa=lru_wa, lru_ba=lru_ba, lru_wx=lru_wx, lru_bx=lru_bx,
             lru_lambda=lru_lambda, sgu_ln_g=sgu_ln_g, sgu_ln_b=sgu_ln_b, sgu_w=sgu_w, sgu_b=sgu_b,
             ssd_conv_w=ssd_conv_w, ssd_conv_b=ssd_conv_b, ssd_dt_bias=ssd_dt_bias, ssd_a_log=ssd_a_log,
             ssd_d=ssd_d, ssd_norm_g=ssd_norm_g, w_pa=w_pa, w_pb=w_pb, w_pc=w_pc, w_pd=w_pd, w_o=w_o,
             norm2_g=norm2_g, router_group_w=router_group_w, router_group_b=router_group_b,
             router_exp_w=router_exp_w, router_exp_b=router_exp_b, moe_w_gate=moe_w_gate,
             moe_w_up=moe_w_up, moe_w_down=moe_w_down)
    B, S, _ = x_prompt.shape
    Bd = x_sample.shape[0]
    depth = w_in.shape[0]
    small = _small_params(p)
    Wp = dict(small, **_matrix_params(p, BF16))
    Wd = dict(small, **_matrix_params(p, F32))
    tabs_p = _rope_tables(jnp.arange(S))
    tabs_s = tuple(jnp.broadcast_to(t, (Bd, LANES)) for t in _rope_tables(PAST_LEN + jnp.arange(1)))
    caches = (cache_kv_a1, cache_kv_a2, cache_kv_a3)
    yp, ys = x_prompt, x_sample.reshape(Bd, D_MODEL)
    P = [[] for _ in range(7)]
    Sx = [[] for _ in range(8)]
    for l in range(depth):
        yp, kvs, st = _prompt_layer(yp, Wp, l, tabs_p)
        for dst, val in zip(P, list(kvs) + list(st)):
            dst.append(val)
        ys, kvs, st = _decode_layer(ys, Wd, l, tabs_s, caches, state_conv_b[l], state_h_b[l],
                                    state_conv_d[l], state_ssm_d)
        for dst, val in zip(Sx, list(kvs) + list(st)):
            dst.append(val)
    st = jnp.stack
    return (yp, ys.reshape(Bd, 1, D_MODEL)) + tuple(st(t) for t in P) + tuple(st(t) for t in Sx)
```

```python
import functools

import jax
import jax.numpy as jnp
from jax import lax
from jax.experimental import pallas as pl
from jax.experimental.pallas import tpu as pltpu

F32 = jnp.float32
BF16 = jnp.bfloat16
HIGHEST = lax.Precision.HIGHEST

D_MODEL = 1024
DEPTH = 4
PAST_LEN = 8192
EPS = 1e-6
HEAD_DIM = 64
A_HPG = 4
A_GROUPS = ((128, 1), (512, 4), (2048, 16))
A_WIDTH = 768
A_OUT = 256
BLK = 128
ROPE_THETA = 10000.0
LRU_WIDTH = 768
LRU_C = 8.0
CONV_W = 4
SGU_WIDTH = 768
SGU_GROUPS = 4
SGU_CHUNK = 128
SSD_INNER = 768
SSD_HEADS = 12
SSD_GROUPS = 2
SSD_STATE = 128
SSD_CHUNK = 128
SSD_CONV_CH = 1280
N_BRANCH = 4
N_EXPERT_GROUPS = 4
EXPERTS_PER_GROUP = 4
N_EXPERTS = 16
D_EXPERT = 512
IN_SIZES = (768, 768, 768, 768, 1536, 768, 1280, 12, 4096)
IN_OFFS = tuple(sum(IN_SIZES[:i]) for i in range(len(IN_SIZES)))

LANES = 128
SUBLANES = 8
VMEM_LIMIT = 56 * 1024 * 1024

TM_QKV = 512
T_SEQ = 256
T_SGU = 512
TM_MERGE = 512
TM_MOE = 1024


def _cp(*sem):
    return pltpu.CompilerParams(dimension_semantics=sem, vmem_limit_bytes=VMEM_LIMIT)


def _const_spec(shape):
    nd = len(shape)
    return pl.BlockSpec(shape, lambda *_: (0,) * nd, pipeline_mode=pl.Buffered(1))


def _lspec(a, l, block=None, idx=None):
    shape = tuple(a.shape[1:]) if block is None else tuple(block)
    tail = (0,) * len(shape) if idx is None else tuple(idx)
    return pl.BlockSpec((None,) + shape, lambda *_: (l,) + tail, pipeline_mode=pl.Buffered(1))


def _rms(x, g):
    return x * lax.rsqrt(jnp.mean(x * x, axis=-1, keepdims=True) + EPS) * g


def _dot3(a, w):
    a1 = a.astype(BF16)
    a2 = (a - a1.astype(F32)).astype(BF16)
    w1 = w.astype(BF16)
    w2 = (w - w1.astype(F32)).astype(BF16)
    d = lambda x, y: jnp.dot(x, y, preferred_element_type=F32)
    return d(a1, w1) + (d(a1, w2) + d(a2, w1))


def _mm(a, w):
    if w.dtype == F32:
        return _dot3(a.astype(F32), w)
    return jnp.dot(a.astype(BF16), w, preferred_element_type=F32)


def _bdot(a, b):
    return jnp.dot(a.astype(BF16), b.astype(BF16), preferred_element_type=F32)


def _bdot_nt(a, b):
    return lax.dot_general(a.astype(BF16), b.astype(BF16), (((1,), (1,)), ((), ())),
                           preferred_element_type=F32)


def _bdot_tn(a, b):
    return lax.dot_general(a.astype(BF16), b.astype(BF16), (((0,), (0,)), ((), ())),
                           preferred_element_type=F32)


def _sigmoid(x):
    return jax.nn.sigmoid(x)


def _silu(x):
    return x * jax.nn.sigmoid(x)


def _neg_expm1(x):
    t = jnp.tanh(0.5 * x)
    return -2.0 * t / (1.0 - t)


def _lanes6(t):
    return jnp.concatenate([t] * 6, axis=1)


def _store_strided_view(t, out_ref, slab_ref, dil):
    if dil == 1:
        out_ref[...] = t.astype(out_ref.dtype)
        return
    T = t.shape[0]
    halves = A_OUT // LANES
    for s in range(halves):
        slab_ref[s] = t[:, s * LANES:(s + 1) * LANES]
    for r in range(dil):
        for s in range(halves):
            c0 = r * A_OUT + s * LANES
            out_ref[:, c0:c0 + LANES] = slab_ref[s, pl.ds(r, T // dil, stride=dil), :].astype(out_ref.dtype)


def _load_strided_view(view_ref, slab_ref, dil):
    if dil == 1:
        return view_ref[...]
    n = view_ref.shape[0]
    halves = A_OUT // LANES
    for r in range(dil):
        for s in range(halves):
            c0 = r * A_OUT + s * LANES
            slab_ref[s, pl.ds(r, n, stride=dil), :] = view_ref[:, c0:c0 + LANES]
    return jnp.concatenate([slab_ref[s] for s in range(halves)], axis=1)


def _qkv_body(x_ref, g1_ref, wq_ref, wk_ref, wv_ref, qg_ref, kg_ref, gbd_ref,
              cos_ref, sa_ref, sb_ref, *outs, split):
    h = _rms(x_ref[...], g1_ref[...]).astype(wq_ref.dtype)
    cos = _lanes6(cos_ref[...])
    sa = _lanes6(sa_ref[...])
    sb = _lanes6(sb_ref[...])
    gbd = gbd_ref[...]

    def normed_rotated(w_ref, hg_ref):
        t = _mm(h, w_ref[...])
        ss = _mm(t * t, gbd)
        tn = t * lax.rsqrt(ss * (1.0 / HEAD_DIM) + EPS) * hg_ref[...]
        return (tn * cos + pltpu.roll(tn, A_WIDTH - HEAD_DIM // 2, 1) * sa
                + pltpu.roll(tn, HEAD_DIM // 2, 1) * sb)

    q = normed_rotated(wq_ref, qg_ref) * (HEAD_DIM ** -0.5)
    k = normed_rotated(wk_ref, kg_ref)
    v = _mm(h, wv_ref[...])
    if split:
        *outs, slab_ref = outs
        for gi, (_, dil) in enumerate(A_GROUPS):
            sl = slice(gi * A_OUT, (gi + 1) * A_OUT)
            for j, t in enumerate((q, k, v)):
                _store_strided_view(t[:, sl], outs[3 * j + gi], slab_ref, dil)
        outs[9][...] = k
        outs[10][...] = v
    else:
        outs[0][...] = q
        outs[1][...] = k
        outs[2][...] = v


def _qkv(x2d, W, l, tabs, tm, split):
    R = x2d.shape[0]
    cos, sa, sb = tabs
    npos = cos.shape[0] // tm
    row = lambda i: (i, 0)
    pos = lambda i: (i % npos, 0)
    full = pl.BlockSpec((tm, A_WIDTH), row)
    scratch = []
    w_in = W["w_in"]
    wcol = lambda j: _lspec(w_in, l, (D_MODEL, A_WIDTH), (0, j))
    if split:
        dils = [dil for _, dil in A_GROUPS] * 3
        out_specs = [pl.BlockSpec((tm // d, d * A_OUT), row) for d in dils] + [full, full]
        out_shape = ([jax.ShapeDtypeStruct((R // d, d * A_OUT), BF16) for d in dils]
                     + [jax.ShapeDtypeStruct((R, A_WIDTH), F32)] * 2)
        scratch = [pltpu.VMEM((A_OUT // LANES, tm, LANES), F32)]
    else:
        out_specs = [full] * 3
        out_shape = [jax.ShapeDtypeStruct((R, A_WIDTH), F32)] * 3
    return pl.pallas_call(
        functools.partial(_qkv_body, split=split),
        grid=(R // tm,), scratch_shapes=scratch,
        in_specs=[pl.BlockSpec((tm, D_MODEL), row), _lspec(W["norm1_g"], l),
                  wcol(0), wcol(1), wcol(2),
                  _lspec(W["q_norm_g"], l), _lspec(W["k_norm_g"], l),
                  _const_spec((A_WIDTH, A_WIDTH)),
                  pl.BlockSpec((tm, LANES), pos), pl.BlockSpec((tm, LANES), pos),
                  pl.BlockSpec((tm, LANES), pos)],
        out_specs=out_specs, out_shape=out_shape,
        compiler_params=_cp("parallel"), name="qkv_proj",
    )(x2d, W["norm1_g"], w_in, w_in, w_in, W["q_norm_g"], W["k_norm_g"], W["gbd"], cos, sa, sb)


def _attn_group(q_ref, kp_ref, kc_ref, vp_ref, vc_ref, o_ref, l_ref, mask, lane_head):
    q = q_ref[...]
    zero = jnp.zeros_like(q)
    qs = jnp.concatenate([jnp.where(lane_head == hh, q, zero) for hh in range(A_HPG)], axis=0)
    kk = jnp.concatenate([kp_ref[...], kc_ref[...]], axis=0)
    vv = jnp.concatenate([vp_ref[...], vc_ref[...]], axis=0)
    s = jnp.where(mask, _bdot_nt(qs, kk), -jnp.inf)
    m = jnp.max(s, axis=-1, keepdims=True)
    e = jnp.exp(s - m)
    den = jnp.sum(e, axis=-1, keepdims=True)
    o4 = _bdot(e * (1.0 / den), vv)
    lse = m + jnp.log(den)
    o = o4[(A_HPG - 1) * BLK:]
    lo = jnp.broadcast_to(lse[(A_HPG - 1) * BLK:], (BLK, A_OUT))
    for hh in range(A_HPG - 2, -1, -1):
        sel = lane_head == hh
        o = jnp.where(sel, o4[hh * BLK:(hh + 1) * BLK], o)
        lo = jnp.where(sel, jnp.broadcast_to(lse[hh * BLK:(hh + 1) * BLK], (BLK, A_OUT)), lo)
    o_ref[...] = o
    l_ref[...] = lo


def _attn_prompt_body(*refs):
    i = pl.program_id(1)
    ins, outs = refs[:15], refs[15:]
    rows = A_HPG * BLK
    qi = lax.broadcasted_iota(jnp.int32, (rows, 2 * BLK), 0) % BLK
    kj = lax.broadcasted_iota(jnp.int32, (rows, 2 * BLK), 1)
    mask_cur = jnp.logical_and(kj >= BLK, kj - BLK <= qi)
    mask_prev = jnp.logical_and(kj < BLK, kj >= qi)
    lane_head = lax.broadcasted_iota(jnp.int32, (BLK, A_OUT), 1) // HEAD_DIM
    for gi, (_, dil) in enumerate(A_GROUPS):
        q_ref, kp_ref, kc_ref, vp_ref, vc_ref = ins[5 * gi:5 * gi + 5]
        o_ref, l_ref = outs[2 * gi:2 * gi + 2]
        mask = jnp.logical_or(mask_cur, jnp.logical_and(mask_prev, (i // dil) > 0))
        _attn_group(q_ref, kp_ref, kc_ref, vp_ref, vc_ref, o_ref, l_ref, mask, lane_head)


def _attn_prompt(qkv, B, S):
    nblk = S // BLK
    args, in_specs, out_specs, out_shape = [], [], [], []
    for gi, (_, dil) in enumerate(A_GROUPS):
        assert S % (dil * BLK) == 0
        rows = S // dil
        cur = lambda b, i, dil=dil: (b, i // dil, i % dil)
        prev = lambda b, i, dil=dil: (b, jnp.maximum(i // dil - 1, 0), i % dil)
        blk = (None, BLK, A_OUT)
        view = lambda t: t.reshape(B, rows, dil * A_OUT)
        qv, kv, vv = view(qkv[gi]), view(qkv[3 + gi]), view(qkv[6 + gi])
        args += [qv, kv, kv, vv, vv]
        in_specs += [pl.BlockSpec(blk, cur), pl.BlockSpec(blk, prev), pl.BlockSpec(blk, cur),
                     pl.BlockSpec(blk, prev), pl.BlockSpec(blk, cur)]
        out_specs += [pl.BlockSpec(blk, cur)] * 2
        out_shape += [jax.ShapeDtypeStruct((B, rows, dil * A_OUT), F32)] * 2
    res = pl.pallas_call(
        _attn_prompt_body, grid=(B, nblk), in_specs=in_specs, out_specs=out_specs,
        out_shape=out_shape, compiler_params=_cp("parallel", "parallel"), name="attn_prompt",
    )(*args)
    return [r.reshape(-1, r.shape[-1]) for r in res]


def _attn_step_body(q_ref, k_ref, v_ref, c1_ref, c2_ref, c3_ref, *outs):
    q_cols = jnp.broadcast_to(q_ref[...], (LANES, A_WIDTH)).T
    lane_head = lax.broadcasted_iota(jnp.int32, (1, A_OUT), 1) // HEAD_DIM
    for gi, (c_ref, (_, dil)) in enumerate(zip((c1_ref, c2_ref, c3_ref), A_GROUPS)):
        o_ref, l_ref = outs[2 * gi:2 * gi + 2]
        W = c_ref.shape[-1]
        pos = lax.broadcasted_iota(jnp.int32, (1, W), 1)
        valid = (pos % dil) == 0
        o_cols, w_new, lse = [], 0.0, 0.0
        for hh in range(A_HPG):
            sl = slice(gi * A_OUT + hh * HEAD_DIM, gi * A_OUT + (hh + 1) * HEAD_DIM)
            s_c = jnp.sum(c_ref[0, hh] * q_cols[sl, 0:1], axis=0, keepdims=True)
            s_c = jnp.where(valid, s_c, -jnp.inf)
            s_n = jnp.sum(q_ref[:, sl] * k_ref[:, sl], axis=-1, keepdims=True)
            m = jnp.maximum(jnp.max(s_c, axis=-1, keepdims=True), s_n)
            e_c = jnp.exp(s_c - m)
            e_n = jnp.exp(s_n - m)
            den = jnp.sum(e_c, axis=-1, keepdims=True) + e_n
            o_cols.append(jnp.sum(c_ref[1, hh] * (e_c / den), axis=-1, keepdims=True))
            w_new = jnp.where(lane_head == hh, e_n / den, w_new)
            lse = jnp.where(lane_head == hh, m + jnp.log(den), lse)
        o_col = jnp.concatenate(o_cols, axis=0)
        o_row = jnp.broadcast_to(o_col, (A_OUT, LANES)).T[0:1]
        o_ref[...] = o_row + w_new * v_ref[:, gi * A_OUT:(gi + 1) * A_OUT]
        l_ref[...] = lse


def _attn_step(q, k, v, caches, l):
    Bd = q.shape[0]
    row = lambda b: (b, 0, 0)
    args = [q.reshape(Bd, 1, A_WIDTH), k.reshape(Bd, 1, A_WIDTH), v.reshape(Bd, 1, A_WIDTH)]
    in_specs = [pl.BlockSpec((None, 1, A_WIDTH), row)] * 3
    for (window, dil), c in zip(A_GROUPS, caches):
        assert c.shape[2] == window and window % dil == 0
        args.append(jnp.transpose(c, (0, 1, 3, 4, 5, 2)))
        in_specs.append(pl.BlockSpec((None, None, 2, A_HPG, HEAD_DIM, window), lambda b: (l, b, 0, 0, 0, 0)))
    res = pl.pallas_call(
        _attn_step_body, grid=(Bd,), in_specs=in_specs,
        out_specs=[pl.BlockSpec((None, 1, A_OUT), row)] * 6,
        out_shape=[jax.ShapeDtypeStruct((Bd, 1, A_OUT), F32)] * 6,
        compiler_params=_cp("parallel"), name="attn_step",
    )(*args)
    return [r.reshape(Bd, A_OUT) for r in res]


def _lru_gates(xc, wa_ref, ba_ref, wx_ref, bx_ref, lam_ref):
    r = _sigmoid(_mm(xc, wa_ref[...]) + ba_ref[...])
    i = _sigmoid(_mm(xc, wx_ref[...]) + bx_ref[...])
    log_a = -LRU_C * r * jax.nn.softplus(-lam_ref[...])
    a = jnp.exp(log_a)
    b = jnp.sqrt(_neg_expm1(2.0 * log_a)) * (i * xc)
    return a, b


def _scan_rows(a, b, h0):
    T = a.shape[0]
    row = lax.broadcasted_iota(jnp.int32, a.shape, 0) % SUBLANES
    k = 1
    while k < SUBLANES:
        keep = row >= k
        a_s = jnp.where(keep, pltpu.roll(a, k, 0), 1.0)
        b_s = jnp.where(keep, pltpu.roll(b, k, 0), 0.0)
        b = a * b_s + b
        a = a * a_s
        k *= 2
    out, carry = [], h0
    for j in range(T // SUBLANES):
        rows = slice(j * SUBLANES, (j + 1) * SUBLANES)
        hj = b[rows] + a[rows] * carry
        out.append(hj)
        carry = hj[SUBLANES - 1:SUBLANES]
    return jnp.concatenate(out, axis=0)


def _lru_body(x_ref, g1_ref, w_ref, cw_ref, cb_ref, wa_ref, ba_ref, wx_ref, bx_ref, lam_ref,
              ob_ref, tail_ref, ext_ref, hc_ref, *, T):
    @pl.when(pl.program_id(1) == 0)
    def _():
        ext_ref[0:SUBLANES, :] = jnp.zeros((SUBLANES, LRU_WIDTH), F32)
        hc_ref[...] = jnp.zeros_like(hc_ref)

    h = _rms(x_ref[...], g1_ref[...])
    xb = _mm(h, w_ref[...])
    ext_ref[SUBLANES:SUBLANES + T, :] = xb
    xc = cb_ref[...]
    for kk in range(CONV_W - 1):
        xc = xc + ext_ref[pl.ds(SUBLANES - (CONV_W - 1) + kk, T), :] * cw_ref[kk:kk + 1, :]
    xc = xc + xb * cw_ref[CONV_W - 1:CONV_W, :]
    tail = xb[T - SUBLANES:T]
    ext_ref[0:SUBLANES, :] = tail
    tail_ref[...] = tail

    a, b = _lru_gates(xc, wa_ref, ba_ref, wx_ref, bx_ref, lam_ref)
    hfull = _scan_rows(a, b, hc_ref[0:1, :])
    ob_ref[...] = hfull
    hc_ref[...] = jnp.broadcast_to(hfull[T - 1:T], hc_ref.shape)


def _lru_prompt(x, W, l, T):
    B, S, _ = x.shape
    tile = lambda b, s: (b, s, 0)
    names = ("norm1_g", None, "conv_b_w", "conv_b_b", "lru_wa_bd", "lru_ba", "lru_wx_bd", "lru_bx", "lru_lambda")
    specs = [_lspec(W[n], l) if n else _lspec(W["w_in"], l, (D_MODEL, LRU_WIDTH), (0, IN_OFFS[3] // LRU_WIDTH))
             for n in names]
    return pl.pallas_call(
        functools.partial(_lru_body, T=T),
        grid=(B, S // T),
        in_specs=[pl.BlockSpec((None, T, D_MODEL), tile)] + specs,
        out_specs=[pl.BlockSpec((None, T, LRU_WIDTH), tile),
                   pl.BlockSpec((None, SUBLANES, LRU_WIDTH), lambda b, s: (b, 0, 0))],
        out_shape=[jax.ShapeDtypeStruct((B, S, LRU_WIDTH), F32),
                   jax.ShapeDtypeStruct((B, SUBLANES, LRU_WIDTH), F32)],
        scratch_shapes=[pltpu.VMEM((T + SUBLANES, LRU_WIDTH), F32),
                        pltpu.VMEM((SUBLANES, LRU_WIDTH), F32)],
        compiler_params=_cp("parallel", "arbitrary"), name="lru_prompt",
    )(x, *[W[n] if n else W["w_in"] for n in names])


def _gelu_ln(uv, lg_ref, lb_ref):
    uv = jax.nn.gelu(uv)
    u = uv[:, :SGU_WIDTH]
    v = uv[:, SGU_WIDTH:]
    mu = jnp.mean(v, axis=-1, keepdims=True)
    var = jnp.mean(jnp.square(v - mu), axis=-1, keepdims=True)
    v = (v - mu) * lax.rsqrt(var + EPS) * lg_ref[...] + lb_ref[...]
    return u, v


def _sgu_body(x_ref, g1_ref, w_ref, lg_ref, lb_ref, ws_ref, bs_ref, oc_ref, *, T):
    h = _rms(x_ref[...], g1_ref[...])
    u, v = _gelu_ln(_mm(h, w_ref[...]), lg_ref, lb_ref)
    qi = lax.broadcasted_iota(jnp.int32, (SGU_CHUNK, SGU_CHUNK), 0)
    kj = lax.broadcasted_iota(jnp.int32, (SGU_CHUNK, SGU_CHUNK), 1)
    tril = (kj <= qi).astype(F32)
    lane = lax.broadcasted_iota(jnp.int32, (SGU_CHUNK, SGU_WIDTH), 1)
    gw = SGU_WIDTH // SGU_GROUPS
    wms = [(ws_ref[g] * tril).astype(BF16) for g in range(SGU_GROUPS)]
    for c in range(T // SGU_CHUNK):
        rows = slice(c * SGU_CHUNK, (c + 1) * SGU_CHUNK)
        vc = v[rows].astype(BF16)
        mixed = jnp.dot(wms[SGU_GROUPS - 1], vc, preferred_element_type=F32)
        for g in range(SGU_GROUPS - 2, -1, -1):
            mixed = jnp.where(lane < (g + 1) * gw, jnp.dot(wms[g], vc, preferred_element_type=F32), mixed)
        oc_ref[rows, :] = u[rows] * (mixed + bs_ref[...])


def _sgu_prompt(x2d, W, l, T):
    R = x2d.shape[0]
    row = lambda i: (i, 0)
    return pl.pallas_call(
        functools.partial(_sgu_body, T=T),
        grid=(R // T,),
        in_specs=[pl.BlockSpec((T, D_MODEL), row), _lspec(W["norm1_g"], l),
                  _lspec(W["w_in"], l, (D_MODEL, 2 * SGU_WIDTH), (0, IN_OFFS[4] // (2 * SGU_WIDTH))),
                  _lspec(W["sgu_ln_g"], l), _lspec(W["sgu_ln_b"], l), _lspec(W["sgu_w"], l),
                  _lspec(W["sgu_b_tab"], l)],
        out_specs=pl.BlockSpec((T, SGU_WIDTH), row),
        out_shape=jax.ShapeDtypeStruct((R, SGU_WIDTH), F32),
        compiler_params=_cp("parallel"), name="sgu_prompt",
    )(x2d, W["norm1_g"], W["w_in"], W["sgu_ln_g"], W["sgu_ln_b"], W["sgu_w"], W["sgu_b_tab"])


def _ssd_dt_a(dtr, dtb_ref, alog_ref):
    lane = lax.broadcasted_iota(jnp.int32, (1, LANES), 1)
    dt = jax.nn.softplus(dtr + dtb_ref[...])
    A = jnp.where(lane < SSD_HEADS, -jnp.exp(alog_ref[...]), 0.0)
    return dt, A * dt


def _ssd_gate_norm(y, xs, z, dsk_ref, ng_ref):
    y = y + dsk_ref[...] * xs
    y = y * _silu(z)
    gw = SSD_INNER // SSD_GROUPS
    parts = []
    for g in range(SSD_GROUPS):
        yg = y[:, g * gw:(g + 1) * gw]
        parts.append(yg * lax.rsqrt(jnp.mean(yg * yg, axis=-1, keepdims=True) + EPS))
    return jnp.concatenate(parts, axis=1) * ng_ref[...]


def _ssd_chunk(xs, dt, a, Bm, Cm, st_ref, y_ref, row0):
    Q = SSD_CHUNK
    qi = lax.broadcasted_iota(jnp.int32, (Q, Q), 0)
    kj = lax.broadcasted_iota(jnp.int32, (Q, Q), 1)
    tril = kj <= qi
    cs = jnp.dot(tril.astype(F32), a, preferred_element_type=F32, precision=HIGHEST)
    cs_t = cs.T
    ecs = jnp.exp(cs)
    cs_last = cs[Q - 1:Q, :]
    to_end = jnp.exp(cs_last - cs)
    e_last = jnp.exp(cs_last)
    hpg = SSD_HEADS // SSD_GROUPS
    for g in range(SSD_GROUPS):
        Cg = Cm[:, g * SSD_STATE:(g + 1) * SSD_STATE].astype(BF16)
        Bg = Bm[:, g * SSD_STATE:(g + 1) * SSD_STATE].astype(BF16)
        G = _bdot_nt(Cg, Bg)
        for hh in range(hpg):
            hd = g * hpg + hh
            sl = slice(hd * HEAD_DIM, (hd + 1) * HEAD_DIM)
            Lh = jnp.exp(jnp.where(tril, cs[:, hd:hd + 1] - cs_t[hd:hd + 1, :], -jnp.inf))
            Xh = xs[:, sl] * dt[:, hd:hd + 1]
            Sp = st_ref[sl, :]
            y = _bdot(G * Lh, Xh) + ecs[:, hd:hd + 1] * _bdot_nt(Cg, Sp)
            y_ref[row0:row0 + Q, sl] = y
            st_ref[sl, :] = e_last[:, hd:hd + 1] * Sp + _bdot_tn(Xh * to_end[:, hd:hd + 1], Bg)


def _ssd_body(x_ref, g1_ref, wz_ref, wxbc_ref, wdt_ref, cw_ref, cb_ref, dtb_ref, alog_ref, dsk_ref,
              ng_ref, od_ref, tail_ref, fin_ref, ext_ref, st_ref, y_ref, *, T):
    @pl.when(pl.program_id(1) == 0)
    def _():
        ext_ref[0:SUBLANES, :] = jnp.zeros((SUBLANES, SSD_CONV_CH), F32)
        st_ref[...] = jnp.zeros_like(st_ref)

    h = _rms(x_ref[...], g1_ref[...]).astype(BF16)
    z = _mm(h, wz_ref[...])
    xbc = _mm(h, wxbc_ref[...])
    dtr = _mm(h, wdt_ref[...])
    ext_ref[SUBLANES:SUBLANES + T, :] = xbc
    xc = cb_ref[...]
    for kk in range(CONV_W - 1):
        xc = xc + ext_ref[pl.ds(SUBLANES - (CONV_W - 1) + kk, T), :] * cw_ref[kk:kk + 1, :]
    xc = xc + xbc * cw_ref[CONV_W - 1:CONV_W, :]
    tail = xbc[T - SUBLANES:T]
    ext_ref[0:SUBLANES, :] = tail
    tail_ref[...] = tail

    xc = _silu(xc)
    xs = xc[:, :SSD_INNER]
    gn = SSD_GROUPS * SSD_STATE
    Bm = xc[:, SSD_INNER:SSD_INNER + gn]
    Cm = xc[:, SSD_INNER + gn:]
    dt, a = _ssd_dt_a(dtr, dtb_ref, alog_ref)
    for c in range(T // SSD_CHUNK):
        rows = slice(c * SSD_CHUNK, (c + 1) * SSD_CHUNK)
        _ssd_chunk(xs[rows], dt[rows], a[rows], Bm[rows], Cm[rows], st_ref, y_ref, c * SSD_CHUNK)
    od_ref[...] = _ssd_gate_norm(y_ref[...], xs, z, dsk_ref, ng_ref)
    fin_ref[...] = st_ref[...]


def _ssd_prompt(x, W, l, T):
    B, S, _ = x.shape
    tile = lambda b, s: (b, s, 0)
    perb = lambda b, s: (b, 0, 0)
    wx = W["w_xbcdt"]
    return pl.pallas_call(
        functools.partial(_ssd_body, T=T),
        grid=(B, S // T),
        in_specs=[pl.BlockSpec((None, T, D_MODEL), tile), _lspec(W["norm1_g"], l),
                  _lspec(W["w_in"], l, (D_MODEL, SSD_INNER), (0, IN_OFFS[5] // SSD_INNER)),
                  _lspec(wx, l, (D_MODEL, SSD_CONV_CH), (0, 0)),
                  _lspec(wx, l, (D_MODEL, LANES), (0, SSD_CONV_CH // LANES)),
                  _lspec(W["ssd_conv_w"], l), _lspec(W["ssd_conv_b"], l), _lspec(W["ssd_dt_bias"], l),
                  _lspec(W["ssd_a_log"], l), _lspec(W["ssd_d_tab"], l), _lspec(W["ssd_norm_g"], l)],
        out_specs=[pl.BlockSpec((None, T, SSD_INNER), tile),
                   pl.BlockSpec((None, SUBLANES, SSD_CONV_CH), perb),
                   pl.BlockSpec((None, SSD_INNER, SSD_STATE), perb)],
        out_shape=[jax.ShapeDtypeStruct((B, S, SSD_INNER), F32),
                   jax.ShapeDtypeStruct((B, SUBLANES, SSD_CONV_CH), F32),
                   jax.ShapeDtypeStruct((B, SSD_INNER, SSD_STATE), F32)],
        scratch_shapes=[pltpu.VMEM((T + SUBLANES, SSD_CONV_CH), F32),
                        pltpu.VMEM((SSD_INNER, SSD_STATE), F32),
                        pltpu.VMEM((T, SSD_INNER), F32)],
        compiler_params=_cp("parallel", "arbitrary"), name="ssd_prompt",
    )(x, W["norm1_g"], W["w_in"], wx, wx, W["ssd_conv_w"], W["ssd_conv_b"],
      W["ssd_dt_bias"], W["ssd_a_log"], W["ssd_d_tab"], W["ssd_norm_g"])


def _proj_body(x_ref, g1_ref, w_ref, o_ref):
    o_ref[...] = _mm(_rms(x_ref[...], g1_ref[...]), w_ref[...])


def _proj(x2d, W, l, col0, ncols, tn):
    R = x2d.shape[0]
    assert col0 % tn == 0
    j0 = col0 // tn
    nblk = pl.cdiv(ncols, tn)
    return pl.pallas_call(
        _proj_body, grid=(nblk,),
        in_specs=[_const_spec((R, D_MODEL)), _lspec(W["norm1_g"], l),
                  pl.BlockSpec((None, D_MODEL, tn), lambda j: (l, 0, j0 + j))],
        out_specs=pl.BlockSpec((R, tn), lambda j: (0, j)),
        out_shape=jax.ShapeDtypeStruct((R, nblk * tn), F32),
        compiler_params=_cp("parallel"), name="proj_rest",
    )(x2d, W["norm1_g"], W["w_in"])


def _conv_step(xnew, st_ref, cw_ref, cb_ref, C):
    out = cb_ref[...]
    for kk in range(CONV_W - 1):
        out = out + st_ref[:, kk * C:(kk + 1) * C] * cw_ref[kk:kk + 1, :]
    return out + xnew * cw_ref[CONV_W - 1:CONV_W, :]


def _dec_branches_body(proj_ref, dtr_ref, scb_ref, hb_ref, scd_ref,
                       cwb_ref, cbb_ref, wa_ref, ba_ref, wx_ref, bx_ref, lam_ref,
                       lg_ref, lb_ref, w00_ref, b0_ref,
                       cwd_ref, cbd_ref, dtb_ref, alog_ref,
                       ob_ref, ncb_ref, oc_ref, vc_ref, ncd_ref, xs_ref, xdt_ref, ea_ref, bc_ref,
                       *, offs):
    o_xb, o_uv, o_xbc = offs
    xb = proj_ref[:, o_xb:o_xb + LRU_WIDTH]
    xc = _conv_step(xb, scb_ref, cwb_ref, cbb_ref, LRU_WIDTH)
    a, b = _lru_gates(xc, wa_ref, ba_ref, wx_ref, bx_ref, lam_ref)
    ob_ref[...] = a * hb_ref[...] + b
    ncb_ref[:, 0:2 * LRU_WIDTH] = scb_ref[:, LRU_WIDTH:3 * LRU_WIDTH]
    ncb_ref[:, 2 * LRU_WIDTH:3 * LRU_WIDTH] = xb
    u, v = _gelu_ln(proj_ref[:, o_uv:o_uv + 2 * SGU_WIDTH], lg_ref, lb_ref)
    vc_ref[...] = v
    oc_ref[...] = u * (w00_ref[...] * v + b0_ref[...])
    xbc = proj_ref[:, o_xbc:o_xbc + SSD_CONV_CH]
    xcd = _silu(_conv_step(xbc, scd_ref, cwd_ref, cbd_ref, SSD_CONV_CH))
    ncd_ref[:, 0:2 * SSD_CONV_CH] = scd_ref[:, SSD_CONV_CH:3 * SSD_CONV_CH]
    ncd_ref[:, 2 * SSD_CONV_CH:3 * SSD_CONV_CH] = xbc
    xs = xcd[:, :SSD_INNER]
    dt, a_dt = _ssd_dt_a(dtr_ref[...], dtb_ref, alog_ref)
    hrow = lax.broadcasted_iota(jnp.int32, (LANES, SSD_INNER), 0)
    hlane = lax.broadcasted_iota(jnp.int32, (LANES, SSD_INNER), 1)
    expand = (hlane // HEAD_DIM == hrow).astype(F32)
    dt_e = jnp.dot(dt, expand, preferred_element_type=F32, precision=HIGHEST)
    a_e = jnp.dot(a_dt, expand, preferred_element_type=F32, precision=HIGHEST)
    xs_ref[...] = xs
    xdt_ref[...] = xs * dt_e
    ea_ref[...] = jnp.exp(a_e)
    bc_ref[...] = xcd[:, SSD_INNER:]


def _dec_branches(proj, dtr, offs, scb, hb, scd, W, l):
    Bd = proj.shape[0]
    f = lambda n: jax.ShapeDtypeStruct((Bd, n), F32)
    whole = lambda a: pl.BlockSpec(a.shape, lambda i: (0,) * a.ndim)
    names = ("conv_b_w", "conv_b_b", "lru_wa_bd", "lru_ba", "lru_wx_bd", "lru_bx", "lru_lambda",
             "sgu_ln_g", "sgu_ln_b", "sgu_w00_tab", "sgu_b0_tab",
             "ssd_conv_w", "ssd_conv_b", "ssd_dt_bias", "ssd_a_log")
    acts = (proj, dtr, scb, hb, scd)
    widths = (LRU_WIDTH, 3 * LRU_WIDTH, SGU_WIDTH, SGU_WIDTH, 3 * SSD_CONV_CH, SSD_INNER, SSD_INNER,
              SSD_INNER, 2 * SSD_GROUPS * SSD_STATE)
    return pl.pallas_call(
        functools.partial(_dec_branches_body, offs=offs),
        grid=(1,),
        in_specs=[whole(a) for a in acts] + [_lspec(W[n], l) for n in names],
        out_specs=[pl.BlockSpec((Bd, n), lambda i: (0, 0)) for n in widths],
        out_shape=[f(n) for n in widths],
        compiler_params=_cp("arbitrary"), name="dec_branches",
    )(*acts, *[W[n] for n in names])


def _dec_ssd_body(xs_ref, xdt_ref, ea_ref, bc_ref, z_ref, st_ref, dsk_ref, ng_ref, od_ref, ns_ref):
    gn = SSD_GROUPS * SSD_STATE
    half = SSD_INNER // SSD_GROUPS
    ridx = lax.broadcasted_iota(jnp.int32, (LANES, SSD_INNER), 0)
    rows = jnp.where(ridx == 0, jnp.broadcast_to(xdt_ref[...], (LANES, SSD_INNER)),
                     jnp.where(ridx == 1, jnp.broadcast_to(ea_ref[...], (LANES, SSD_INNER)), 0.0))
    cols = rows.T
    xdt_c = cols[:, 0:1]
    ea_c = cols[:, 1:2]
    bc = bc_ref[...]
    Bm, Cm = bc[:, :gn], bc[:, gn:]
    rowi = lax.broadcasted_iota(jnp.int32, (SSD_INNER, SSD_STATE), 0)
    b_full = jnp.where(rowi < half, jnp.broadcast_to(Bm[:, :SSD_STATE], (SSD_INNER, SSD_STATE)),
                       jnp.broadcast_to(Bm[:, SSD_STATE:], (SSD_INNER, SSD_STATE)))
    new = ea_c * st_ref[...] + xdt_c * b_full
    ns_ref[...] = new
    cidx = lax.broadcasted_iota(jnp.int32, (SUBLANES, SSD_STATE), 0)
    c8 = jnp.where(cidx == 0, jnp.broadcast_to(Cm[:, :SSD_STATE], (SUBLANES, SSD_STATE)),
                   jnp.where(cidx == 1, jnp.broadcast_to(Cm[:, SSD_STATE:], (SUBLANES, SSD_STATE)), 0.0))
    y8 = lax.dot_general(c8, new, (((1,), (1,)), ((), ())), preferred_element_type=F32,
                         precision=HIGHEST)
    lane = lax.broadcasted_iota(jnp.int32, (1, SSD_INNER), 1)
    y = jnp.where(lane < half, y8[0:1], y8[1:2])
    od_ref[...] = _ssd_gate_norm(y, xs_ref[...], z_ref[...], dsk_ref, ng_ref)


def _dec_ssd(xs, xdt, ea, bc, z, state_all, W, l):
    Bd = xs.shape[0]
    row = lambda b: (b, 0, 0)
    r3 = lambda t: t.reshape(Bd, 1, t.shape[-1])
    st = state_all.reshape(state_all.shape[0], Bd, SSD_INNER, SSD_STATE)
    od, ns = pl.pallas_call(
        _dec_ssd_body, grid=(Bd,),
        in_specs=[pl.BlockSpec((None, 1, SSD_INNER), row)] * 3
                 + [pl.BlockSpec((None, 1, 2 * SSD_GROUPS * SSD_STATE), row),
                    pl.BlockSpec((None, 1, SSD_INNER), row),
                    pl.BlockSpec((None, None, SSD_INNER, SSD_STATE), lambda b: (l, b, 0, 0)),
                    _lspec(W["ssd_d_tab"], l), _lspec(W["ssd_norm_g"], l)],
        out_specs=[pl.BlockSpec((None, 1, SSD_INNER), row),
                   pl.BlockSpec((None, SSD_INNER, SSD_STATE), row)],
        out_shape=[jax.ShapeDtypeStruct((Bd, 1, SSD_INNER), F32),
                   jax.ShapeDtypeStruct((Bd, SSD_INNER, SSD_STATE), F32)],
        compiler_params=_cp("parallel"), name="dec_ssd",
    )(r3(xs), r3(xdt), r3(ea), r3(bc), r3(z), st, W["ssd_d_tab"], W["ssd_norm_g"])
    return od.reshape(Bd, SSD_INNER), ns.reshape(state_all.shape[1:])


def _merge_body(x_ref, o1_ref, l1_ref, o2_ref, l2_ref, o3_ref, l3_ref, ob_ref, oc_ref, od_ref,
                g1_ref, wg_ref, wpa_ref, wpb_ref, wpc_ref, wpd_ref, wo_ref, out_ref, *slab, dils, pre_gates):
    x = x_ref[...]
    h = None if pre_gates else _rms(x, g1_ref[...]).astype(wg_ref.dtype)
    tok = lambda ref, d: _load_strided_view(ref, slab[0], d) if d > 1 else ref[...]
    o1, o2, o3 = (tok(r, d) for r, d in zip((o1_ref, o2_ref, o3_ref), dils))
    l1, l2, l3 = (tok(r, d) for r, d in zip((l1_ref, l2_ref, l3_ref), dils))
    m = jnp.maximum(jnp.maximum(l1, l2), l3)
    e1, e2, e3 = jnp.exp(l1 - m), jnp.exp(l2 - m), jnp.exp(l3 - m)
    den = e1 + e2 + e3
    oa = (e1 / den) * o1 + (e2 / den) * o2 + (e3 / den) * o3
    merged = None
    for bi, (o, w_ref) in enumerate(((oa, wpa_ref), (ob_ref[...], wpb_ref),
                                     (oc_ref[...], wpc_ref), (od_ref[...], wpd_ref))):
        cols = slice(bi * D_MODEL, (bi + 1) * D_MODEL)
        gate = _sigmoid(wg_ref[:, cols] if pre_gates else _mm(h, wg_ref[:, cols]))
        term = gate * _mm(o, w_ref[...])
        merged = term if merged is None else merged + term
    out_ref[...] = x + _mm(merged, wo_ref[...])


def _merge(x2d, att, ob, oc, od, W, l, tm, dils, gates=None):
    R = x2d.shape[0]
    row = lambda i: (i, 0)
    names = ("norm1_g", "w_gates", "w_pa", "w_pb", "w_pc", "w_pd", "w_o")
    params = [W[n] for n in names]
    pspecs = [_lspec(a, l) for a in params]
    if gates is not None:
        params[1], pspecs[1] = gates, pl.BlockSpec((tm, N_BRANCH * D_MODEL), row)
    att_specs = [pl.BlockSpec((tm // d, d * A_OUT), row) for d in dils for _ in range(2)]
    scratch = [pltpu.VMEM((A_OUT // LANES, tm, LANES), F32)] if max(dils) > 1 else []
    return pl.pallas_call(
        functools.partial(_merge_body, dils=dils, pre_gates=gates is not None), grid=(R // tm,),
        in_specs=[pl.BlockSpec((tm, D_MODEL), row)] + att_specs
                 + [pl.BlockSpec((tm, LRU_WIDTH), row)] * 3 + pspecs,
        out_specs=pl.BlockSpec((tm, D_MODEL), row),
        out_shape=jax.ShapeDtypeStruct((R, D_MODEL), F32), scratch_shapes=scratch,
        compiler_params=_cp("parallel"), name="merge",
    )(x2d, *att, ob, oc, od, *params)


def _route(logits):
    lane = lax.broadcasted_iota(jnp.int32, logits.shape, 1)
    big = jnp.int32(LANES)
    ninf = -jnp.inf
    gl = jnp.where(lane < N_EXPERT_GROUPS, logits, ninf)
    gm = jnp.max(gl, axis=-1, keepdims=True)
    gsel = jnp.min(jnp.where(gl == gm, lane, big), axis=-1, keepdims=True)
    pg = 1.0 / jnp.sum(jnp.exp(gl - gm), axis=-1, keepdims=True)
    lo = N_EXPERT_GROUPS + EXPERTS_PER_GROUP * gsel
    el = jnp.where(jnp.logical_and(lane >= lo, lane < lo + EXPERTS_PER_GROUP), logits, ninf)
    t1 = jnp.max(el, axis=-1, keepdims=True)
    i1 = jnp.min(jnp.where(el == t1, lane, big), axis=-1, keepdims=True)
    el2 = jnp.where(lane == i1, ninf, el)
    t2 = jnp.max(el2, axis=-1, keepdims=True)
    i2 = jnp.min(jnp.where(el2 == t2, lane, big), axis=-1, keepdims=True)
    e2 = jnp.exp(t2 - t1)
    den = 1.0 + e2
    w1 = (1.0 / den) * pg
    w2 = (e2 / den) * pg
    return jnp.where(lane == i1, w1, 0.0) + jnp.where(lane == i2, w2, 0.0), gsel


def _moe_body(x_ref, g2_ref, wr_ref, br_ref, wg_ref, wu_ref, wd_ref, out_ref, h_ref, comb_ref, acc_ref):
    e = pl.program_id(1)

    @pl.when(e == 0)
    def _():
        h = _rms(x_ref[...], g2_ref[...])
        h_ref[...] = h.astype(h_ref.dtype)
        logits = jnp.dot(h, wr_ref[...], preferred_element_type=F32, precision=HIGHEST) + br_ref[...]
        comb_ref[...] = _route(logits)[0]
        acc_ref[...] = jnp.zeros_like(acc_ref)

    h = h_ref[...]
    lane = lax.broadcasted_iota(jnp.int32, comb_ref.shape, 1)
    c = jnp.sum(jnp.where(lane == N_EXPERT_GROUPS + e, comb_ref[...], 0.0), axis=-1, keepdims=True)
    hg = _mm(h, wg_ref[...])
    hu = _mm(h, wu_ref[...])
    act = _silu(hg) * hu * c
    acc_ref[...] += _mm(act, wd_ref[...])

    @pl.when(e == N_EXPERTS - 1)
    def _():
        out_ref[...] = x_ref[...] + acc_ref[...]


def _moe(x2d, W, l, tm):
    R = x2d.shape[0]
    row = lambda i, e: (i, 0)
    expert = lambda r, c: pl.BlockSpec((None, None, r, c), lambda i, e: (l, e, 0, 0))
    return pl.pallas_call(
        _moe_body, grid=(R // tm, N_EXPERTS),
        in_specs=[pl.BlockSpec((tm, D_MODEL), row), _lspec(W["norm2_g"], l), _lspec(W["w_router"], l),
                  _lspec(W["b_router"], l),
                  expert(D_MODEL, D_EXPERT), expert(D_MODEL, D_EXPERT), expert(D_EXPERT, D_MODEL)],
        out_specs=pl.BlockSpec((tm, D_MODEL), row),
        out_shape=jax.ShapeDtypeStruct((R, D_MODEL), F32),
        scratch_shapes=[pltpu.VMEM((tm, D_MODEL), W["moe_w_gate"].dtype), pltpu.VMEM((tm, LANES), F32),
                        pltpu.VMEM((tm, D_MODEL), F32)],
        compiler_params=_cp("parallel", "arbitrary"), name="moe",
    )(x2d, W["norm2_g"], W["w_router"], W["b_router"], W["moe_w_gate"], W["moe_w_up"],
      W["moe_w_down"])


GID_LANE = N_EXPERT_GROUPS + N_EXPERTS
SEG_ALIGN = 16
MOE_RB = TM_MOE // N_EXPERT_GROUPS + 32
PERM_BLK = 128
UNPERM_BLK = 256
MOE_USED = -(-(TM_MOE + N_EXPERT_GROUPS * SEG_ALIGN) // PERM_BLK) * PERM_BLK
MOE_TMP = MOE_USED + MOE_RB
assert MOE_RB % SEG_ALIGN == 0 and TM_MOE % UNPERM_BLK == 0


def _route_body(x_ref, g2_ref, wr_ref, br_ref, comb_ref, cnt_ref):
    h = _rms(x_ref[...], g2_ref[...])
    logits = _dot3(h, wr_ref[...]) + br_ref[...]
    comb, gsel = _route(logits)
    lane = lax.broadcasted_iota(jnp.int32, comb.shape, 1)
    comb_ref[...] = jnp.where(lane == GID_LANE, gsel.astype(F32), comb)
    counts = jnp.sum((lane == gsel).astype(F32), axis=0, keepdims=True)
    cnt_ref[...] = jnp.broadcast_to(counts, cnt_ref.shape)


def _route_call(x2d, W, l, tm):
    R = x2d.shape[0]
    return pl.pallas_call(
        _route_body, grid=(R // tm,),
        in_specs=[pl.BlockSpec((tm, D_MODEL), lambda i: (i, 0)), _lspec(W["norm2_g"], l),
                  _lspec(W["w_router"], l), _lspec(W["b_router"], l)],
        out_specs=[pl.BlockSpec((tm, LANES), lambda i: (i, 0)),
                   pl.BlockSpec((None, SUBLANES, LANES), lambda i: (i, 0, 0))],
        out_shape=[jax.ShapeDtypeStruct((R, LANES), F32),
                   jax.ShapeDtypeStruct((R // tm, SUBLANES, LANES), F32)],
        compiler_params=_cp("parallel"), name="moe_route",
    )(x2d, W["norm2_g"], W["w_router"], W["b_router"])


def _moe_sorted_body(off_ref, nch_ref, x_ref, comb_ref, g2_ref, wg_ref, wu_ref, wd_ref, out_ref,
                     sh_ref, sc_ref, acc_ref, rcol_ref, lo_ref, *, TM, RB):
    i = pl.program_id(0)
    g = pl.program_id(1)
    TMP = sh_ref.shape[0]

    @pl.when(g == 0)
    def _():
        h2 = _rms(x_ref[...], g2_ref[...]).astype(BF16)
        comb = comb_ref[...]
        lane = lax.broadcasted_iota(jnp.int32, comb.shape, 1)
        gid = jnp.sum(jnp.where(lane == GID_LANE, comb, 0.0), axis=-1, keepdims=True)
        onehot = jnp.logical_and(lane < N_EXPERT_GROUPS, lane.astype(F32) == gid).astype(F32)
        trow = lax.broadcasted_iota(jnp.int32, onehot.shape, 0)
        run = onehot
        k = 1
        while k < TM:
            run = run + jnp.where(trow >= k, pltpu.roll(run, k, 0), 0.0)
            k *= 2
        prefix = run - onehot
        lane1 = lax.broadcasted_iota(jnp.int32, (1, LANES), 1)
        offv = jnp.zeros((1, LANES), F32)
        for gg in range(N_EXPERT_GROUPS):
            offv = jnp.where(lane1 == gg, off_ref[i * N_EXPERT_GROUPS + gg].astype(F32), offv)
        rank = jnp.sum(onehot * (offv + prefix), axis=-1, keepdims=True)
        rank_b = jnp.broadcast_to(rank, (TM, LANES))
        rcol_ref[...] = rank_b
        rank_row = rank_b.T[0:1]
        c1 = comb.astype(BF16)
        c2 = (comb - c1.astype(F32)).astype(BF16)
        payload = jnp.concatenate([h2, c1, c2], axis=1)
        for blk in range(MOE_USED // PERM_BLK):
            rows = slice(blk * PERM_BLK, (blk + 1) * PERM_BLK)
            srow = lax.broadcasted_iota(jnp.int32, (PERM_BLK, TM), 0) + blk * PERM_BLK
            perm = (srow.astype(F32) == rank_row).astype(BF16)
            moved = jnp.dot(perm, payload, preferred_element_type=F32)
            sh_ref[rows, :] = moved[:, :D_MODEL].astype(BF16)
            sc_ref[rows, :] = moved[:, D_MODEL:D_MODEL + LANES] + moved[:, D_MODEL + LANES:]
        sh_ref[MOE_USED:, :] = jnp.zeros((TMP - MOE_USED, D_MODEL), BF16)
        sc_ref[MOE_USED:, :] = jnp.zeros((TMP - MOE_USED, LANES), F32)
        acc_ref[...] = jnp.zeros_like(acc_ref)

    off = off_ref[i * N_EXPERT_GROUPS + g]
    lane_c = lax.broadcasted_iota(jnp.int32, (RB, LANES), 1)

    def chunk(j, carry):
        r0 = pl.multiple_of(off + j * RB, SEG_ALIGN)
        rows = sh_ref[pl.ds(r0, RB), :]
        cc = sc_ref[pl.ds(r0, RB), :]
        y = jnp.zeros((RB, D_MODEL), F32)
        for e in range(EXPERTS_PER_GROUP):
            ce = jnp.sum(jnp.where(lane_c == N_EXPERT_GROUPS + EXPERTS_PER_GROUP * g + e, cc, 0.0),
                         axis=-1, keepdims=True)
            hg = jnp.dot(rows, wg_ref[e], preferred_element_type=F32)
            hu = jnp.dot(rows, wu_ref[e], preferred_element_type=F32)
            y = y + _bdot(_silu(hg) * hu * ce, wd_ref[e])
        acc_ref[pl.ds(r0, RB), :] += y
        return carry

    lax.fori_loop(0, nch_ref[i * N_EXPERT_GROUPS + g], chunk, 0)

    @pl.when(g == N_EXPERT_GROUPS - 1)
    def _():
        for blk in range(MOE_USED // PERM_BLK):
            rows = slice(blk * PERM_BLK, (blk + 1) * PERM_BLK)
            a = acc_ref[rows, :]
            hi = a.astype(BF16)
            sh_ref[rows, :] = hi
            lo_ref[rows, :] = (a - hi.astype(F32)).astype(BF16)
        scol = lax.broadcasted_iota(jnp.int32, (UNPERM_BLK, MOE_USED), 1).astype(F32)
        for blk in range(TM // UNPERM_BLK):
            rows = slice(blk * UNPERM_BLK, (blk + 1) * UNPERM_BLK)
            unperm = (scol == rcol_ref[rows, 0:1]).astype(BF16)
            out_ref[rows, :] = x_ref[rows, :] + (
                jnp.dot(unperm, sh_ref[0:MOE_USED, :], preferred_element_type=F32)
                + jnp.dot(unperm, lo_ref[0:MOE_USED, :], preferred_element_type=F32))


def _moe_sorted(x2d, W, l):
    R = x2d.shape[0]
    TM, RB, TMP = TM_MOE, MOE_RB, MOE_TMP
    nt = R // TM
    comb, counts = _route_call(x2d, W, l, TM)
    cnt = counts[:, 0, :N_EXPERT_GROUPS].astype(jnp.int32)
    padded = (cnt + SEG_ALIGN - 1) // SEG_ALIGN * SEG_ALIGN
    off = (jnp.cumsum(padded, axis=1) - padded).reshape(-1)
    nch = ((cnt + RB - 1) // RB).reshape(-1)
    row = lambda i, g, *_: (i, 0)
    const = lambda a: pl.BlockSpec((None,) + tuple(a.shape[1:]), lambda i, g, *_: (l,) + (0,) * (a.ndim - 1))
    grp = lambda r, c: pl.BlockSpec((None, None, EXPERTS_PER_GROUP, r, c), lambda i, g, *_: (l, g, 0, 0, 0))
    gview = lambda a: a.reshape(a.shape[0], N_EXPERT_GROUPS, EXPERTS_PER_GROUP, a.shape[2], a.shape[3])
    return pl.pallas_call(
        functools.partial(_moe_sorted_body, TM=TM, RB=RB),
        grid_spec=pltpu.PrefetchScalarGridSpec(
            num_scalar_prefetch=2, grid=(nt, N_EXPERT_GROUPS),
            in_specs=[pl.BlockSpec((TM, D_MODEL), row, pipeline_mode=pl.Buffered(1)),
                      pl.BlockSpec((TM, LANES), row), const(W["norm2_g"]),
                      grp(D_MODEL, D_EXPERT), grp(D_MODEL, D_EXPERT), grp(D_EXPERT, D_MODEL)],
            out_specs=pl.BlockSpec((TM, D_MODEL), row),
            scratch_shapes=[pltpu.VMEM((TMP, D_MODEL), BF16), pltpu.VMEM((TMP, LANES), F32),
                            pltpu.VMEM((TMP, D_MODEL), F32), pltpu.VMEM((TM, LANES), F32),
                            pltpu.VMEM((TMP, D_MODEL), BF16)]),
        out_shape=jax.ShapeDtypeStruct((R, D_MODEL), F32),
        compiler_params=_cp("parallel", "arbitrary"), name="moe_sorted",
    )(off, nch, x2d, comb, W["norm2_g"], gview(W["moe_w_gate"]), gview(W["moe_w_up"]), gview(W["moe_w_down"]))


def _rope_tables(pos):
    half = HEAD_DIM // 2
    inv = ROPE_THETA ** (-jnp.arange(half, dtype=F32) / half)
    ang = pos.astype(F32)[:, None] * inv[None]
    c, s = jnp.cos(ang), jnp.sin(ang)
    z = jnp.zeros_like(s)
    reps = LANES // HEAD_DIM
    return (jnp.tile(jnp.concatenate([c, c], 1), (1, reps)),
            jnp.tile(jnp.concatenate([-s, z], 1), (1, reps)),
            jnp.tile(jnp.concatenate([z, s], 1), (1, reps)))


def _block_diag(w):
    L, n, k, _ = w.shape
    eye = jnp.eye(n, dtype=w.dtype)
    return (eye[None, :, None, :, None] * w[:, :, :, None, :]).reshape(L, n * k, n * k)


def _small_params(p):
    vec = lambda t: t.reshape(t.shape[0], 1, -1)
    W = {}
    for n in ("norm1_g", "norm2_g", "conv_b_b", "lru_ba", "lru_bx", "lru_lambda", "sgu_ln_g", "sgu_ln_b",
              "ssd_conv_b", "ssd_norm_g"):
        W[n] = vec(p[n])
    reps = A_WIDTH // HEAD_DIM
    W["q_norm_g"] = vec(jnp.tile(p["q_norm_g"], (1, reps)))
    W["k_norm_g"] = vec(jnp.tile(p["k_norm_g"], (1, reps)))
    W["conv_b_w"] = p["conv_b_w"]
    W["ssd_conv_w"] = p["ssd_conv_w"]
    W["sgu_w"] = p["sgu_w"]
    gw = SGU_WIDTH // SGU_GROUPS
    W["sgu_b_tab"] = jnp.repeat(jnp.swapaxes(p["sgu_b"], 1, 2), gw, axis=2)
    W["sgu_w00_tab"] = vec(jnp.repeat(p["sgu_w"][:, :, 0, 0], gw, axis=1))
    W["sgu_b0_tab"] = vec(jnp.repeat(p["sgu_b"][:, :, 0], gw, axis=1))
    padh = lambda t: vec(jnp.pad(t, ((0, 0), (0, LANES - SSD_HEADS))))
    W["ssd_dt_bias"], W["ssd_a_log"] = padh(p["ssd_dt_bias"]), padh(p["ssd_a_log"])
    W["ssd_d_tab"] = vec(jnp.repeat(p["ssd_d"], HEAD_DIM, axis=1))
    we = jnp.transpose(p["router_exp_w"], (0, 2, 1, 3)).reshape(-1, D_MODEL, N_EXPERTS)
    wr = jnp.concatenate([p["router_group_w"], we], axis=2)
    W["w_router"] = jnp.pad(wr, ((0, 0), (0, 0), (0, LANES - wr.shape[2])))
    br = jnp.concatenate([p["router_group_b"], p["router_exp_b"].reshape(-1, N_EXPERTS)], axis=1)
    W["b_router"] = vec(jnp.pad(br, ((0, 0), (0, LANES - br.shape[1]))))
    return W


def _matrix_params(p, wdt):
    W = {}
    w_in = p["w_in"].astype(wdt)
    W["w_in"] = w_in
    o_xbc, o_dt, o_g = IN_OFFS[6], IN_OFFS[7], IN_OFFS[8]
    W["w_xbcdt"] = jnp.concatenate(
        [w_in[:, :, o_xbc:o_dt], jnp.pad(w_in[:, :, o_dt:o_g], ((0, 0), (0, 0), (0, LANES - SSD_HEADS)))], axis=2)
    W["w_gates"] = w_in[:, :, o_g:]
    reps = A_WIDTH // HEAD_DIM
    W["gbd"] = _block_diag(jnp.ones((1, reps, HEAD_DIM, HEAD_DIM), wdt))[0]
    W["lru_wa_bd"] = _block_diag(p["lru_wa"]).astype(wdt)
    W["lru_wx_bd"] = _block_diag(p["lru_wx"]).astype(wdt)
    for n in ("w_pa", "w_pb", "w_pc", "w_pd", "w_o", "moe_w_gate", "moe_w_up", "moe_w_down"):
        W[n] = p[n].astype(wdt)
    return W


def _kv_rows(k, v, gi):
    sl = slice(gi * A_OUT, (gi + 1) * A_OUT)
    shp = k.shape[:-1] + (A_HPG, HEAD_DIM)
    return jnp.stack([k[..., sl].reshape(shp), v[..., sl].reshape(shp)], axis=-3)


def _prompt_layer(x, W, l, tabs):
    B, S, _ = x.shape
    x2d = x.reshape(B * S, D_MODEL)
    *qkv, k, v = _qkv(x2d, W, l, tabs, TM_QKV, split=True)
    att = _attn_prompt(qkv, B, S)
    ob, tail_b = _lru_prompt(x, W, l, T_SEQ)
    oc = _sgu_prompt(x2d, W, l, T_SGU)
    od, tail_d, fin = _ssd_prompt(x, W, l, T_SEQ)
    x2d = _merge(x2d, att, ob.reshape(B * S, -1), oc, od.reshape(B * S, -1), W, l, TM_MERGE,
                 tuple(dil for _, dil in A_GROUPS))
    x2d = _moe_sorted(x2d, W, l)
    k3, v3 = k.reshape(B, S, -1), v.reshape(B, S, -1)
    kvs = [_kv_rows(k3[:, S - min(w, S):], v3[:, S - min(w, S):], gi) for gi, (w, _) in enumerate(A_GROUPS)]
    nb = CONV_W - 1
    states = (tail_b[:, SUBLANES - nb:], ob[:, -1], tail_d[:, SUBLANES - nb:],
              fin.reshape(B, SSD_HEADS, HEAD_DIM, SSD_STATE))
    return x2d.reshape(B, S, D_MODEL), kvs, states


def _decode_layer(x2d, W, l, tabs, caches, scb, hb, scd, ssm_all):
    Bd = x2d.shape[0]
    q, k, v = _qkv(x2d, W, l, tabs, Bd, split=False)
    att = _attn_step(q, k, v, caches, l)
    col0 = IN_OFFS[3]
    proj = _proj(x2d, W, l, col0, sum(IN_SIZES) - col0, 256)
    dtr = proj[:, IN_OFFS[7] - col0:IN_OFFS[7] - col0 + LANES]
    gates = proj[:, IN_OFFS[8] - col0:IN_OFFS[8] - col0 + N_BRANCH * D_MODEL]
    offs = (0, IN_OFFS[4] - col0, IN_OFFS[6] - col0)
    nb = CONV_W - 1
    ob, ncb, oc, vc, ncd, xs, xdt, ea, bc = _dec_branches(
        proj, dtr, offs, scb.reshape(Bd, nb * LRU_WIDTH), hb, scd.reshape(Bd, nb * SSD_CONV_CH), W, l)
    z = proj[:, IN_OFFS[5] - col0:IN_OFFS[5] - col0 + SSD_INNER]
    od, nssm = _dec_ssd(xs, xdt, ea, bc, z, ssm_all, W, l)
    x2d = _merge(x2d, att, ob, oc, od, W, l, Bd, (1,) * len(A_GROUPS), gates=gates)
    x2d = _moe(x2d, W, l, Bd)
    kvs = [_kv_rows(k, v, gi)[:, None] for gi in range(len(A_GROUPS))]
    states = (ncb.reshape(Bd, nb, LRU_WIDTH), ob, vc[:, None], ncd.reshape(Bd, nb, SSD_CONV_CH), nssm)
    return x2d, kvs, states


def kernel(x_prompt, x_sample, cache_kv_a1, cache_kv_a2, cache_kv_a3, state_conv_b, state_h_b, state_conv_d, state_ssm_d, norm1_g, w_in, q_norm_g, k_norm_g, conv_b_w, conv_b_b, lru_wa, lru_ba, lru_wx, lru_bx, lru_lambda, sgu_ln_g, sgu_ln_b, sgu_w, sgu_b, ssd_conv_w, ssd_conv_b, ssd_dt_bias, ssd_a_log, ssd_d, ssd_norm_g, w_pa, w_pb, w_pc, w_pd, w_o, norm2_g, router_group_w, router_group_b, router_exp_w, router_exp_b, moe_w_gate, moe_w_up, moe_w_down):
    p = dict(norm1_g=norm1_g, w_in=w_in, q_norm_g=q_norm_g, k_norm_g=k_norm_g, conv_b_w=conv_b_w,
             conv_b_b=conv_b_b, lru_wa=lru_wa, lru_ba=lru_ba, lru_wx=lru_wx, lru_bx=lru_bx,
             lru_lambda=lru_lambda, sgu_ln_g=sgu_ln_g, sgu_ln_b=sgu_ln_b, sgu_w=sgu_w, sgu_b=sgu_b,
             ssd_conv_w=ssd_conv_w, ssd_conv_b=ssd_conv_b, ssd_dt_bias=ssd_dt_bias, ssd_a_log=ssd_a_log,
             ssd_d=ssd_d, ssd_norm_g=ssd_norm_g, w_pa=w_pa, w_pb=w_pb, w_pc=w_pc, w_pd=w_pd, w_o=w_o,
             norm2_g=norm2_g, router_group_w=router_group_w, router_group_b=router_group_b,
             router_exp_w=router_exp_w, router_exp_b=router_exp_b, moe_w_gate=moe_w_gate,
             moe_w_up=moe_w_up, moe_w_down=moe_w_down)
    B, S, _ = x_prompt.shape
    Bd = x_sample.shape[0]
    depth = w_in.shape[0]
    small = _small_params(p)
    Wp = dict(small, **_matrix_params(p, BF16))
    Wd = dict(small, **_matrix_params(p, F32))
    tabs_p = _rope_tables(jnp.arange(S))
    tabs_s = tuple(jnp.broadcast_to(t, (Bd, LANES)) for t in _rope_tables(PAST_LEN + jnp.arange(1)))
    caches = (cache_kv_a1, cache_kv_a2, cache_kv_a3)
    yp, ys = x_prompt, x_sample.reshape(Bd, D_MODEL)
    P = [[] for _ in range(7)]
    Sx = [[] for _ in range(8)]
    for l in range(depth):
        yp, kvs, st = _prompt_layer(yp, Wp, l, tabs_p)
        for dst, val in zip(P, list(kvs) + list(st)):
            dst.append(val)
        ys, kvs, st = _decode_layer(ys, Wd, l, tabs_s, caches, state_conv_b[l], state_h_b[l],
                                    state_conv_d[l], state_ssm_d)
        for dst, val in zip(Sx, list(kvs) + list(st)):
            dst.append(val)
    st = jnp.stack
    return (yp, ys.reshape(Bd, 1, D_MODEL)) + tuple(st(t) for t in P) + tuple(st(t) for t in Sx)
```

```python
import functools

import jax
import jax.numpy as jnp
from jax import lax
from jax.experimental import pallas as pl
from jax.experimental.pallas import tpu as pltpu

F32 = jnp.float32
BF16 = jnp.bfloat16
HIGHEST = lax.Precision.HIGHEST

D_MODEL = 1024
DEPTH = 4
PAST_LEN = 8192
EPS = 1e-6
HEAD_DIM = 64
A_HPG = 4
A_GROUPS = ((128, 1), (512, 4), (2048, 16))
A_WIDTH = 768
A_OUT = 256
BLK = 128
ROPE_THETA = 10000.0
LRU_WIDTH = 768
LRU_C = 8.0
CONV_W = 4
SGU_WIDTH = 768
SGU_GROUPS = 4
SGU_CHUNK = 128
SSD_INNER = 768
SSD_HEADS = 12
SSD_GROUPS = 2
SSD_STATE = 128
SSD_CHUNK = 128
SSD_CONV_CH = 1280
N_BRANCH = 4
N_EXPERT_GROUPS = 4
EXPERTS_PER_GROUP = 4
N_EXPERTS = 16
D_EXPERT = 512
IN_SIZES = (768, 768, 768, 768, 1536, 768, 1280, 12, 4096)
IN_OFFS = tuple(sum(IN_SIZES[:i]) for i in range(len(IN_SIZES)))

LANES = 128
SUBLANES = 8
VMEM_LIMIT = 56 * 1024 * 1024

TM_QKV = 512
T_SEQ = 256
T_SGU = 512
TM_MERGE = 512
TM_MOE = 1024


def _cp(*sem):
    return pltpu.CompilerParams(dimension_semantics=sem, vmem_limit_bytes=VMEM_LIMIT)


def _const_spec(shape):
    nd = len(shape)
    return pl.BlockSpec(shape, lambda *_: (0,) * nd, pipeline_mode=pl.Buffered(1))


def _lspec(a, l, block=None, idx=None):
    shape = tuple(a.shape[1:]) if block is None else tuple(block)
    tail = (0,) * len(shape) if idx is None else tuple(idx)
    return pl.BlockSpec((None,) + shape, lambda *_: (l,) + tail, pipeline_mode=pl.Buffered(1))


def _rms(x, g):
    return x * lax.rsqrt(jnp.mean(x * x, axis=-1, keepdims=True) + EPS) * g


def _dot3(a, w):
    a1 = a.astype(BF16)
    a2 = (a - a1.astype(F32)).astype(BF16)
    w1 = w.astype(BF16)
    w2 = (w - w1.astype(F32)).astype(BF16)
    d = lambda x, y: jnp.dot(x, y, preferred_element_type=F32)
    return d(a1, w1) + (d(a1, w2) + d(a2, w1))


def _mm(a, w):
    if w.dtype == F32:
        return _dot3(a.astype(F32), w)
    return jnp.dot(a.astype(BF16), w, preferred_element_type=F32)


def _bdot(a, b):
    return jnp.dot(a.astype(BF16), b.astype(BF16), preferred_element_type=F32)


def _bdot_nt(a, b):
    return lax.dot_general(a.astype(BF16), b.astype(BF16), (((1,), (1,)), ((), ())),
                           preferred_element_type=F32)


def _bdot_tn(a, b):
    return lax.dot_general(a.astype(BF16), b.astype(BF16), (((0,), (0,)), ((), ())),
                           preferred_element_type=F32)


def _sigmoid(x):
    return jax.nn.sigmoid(x)


def _silu(x):
    return x * jax.nn.sigmoid(x)


def _neg_expm1(x):
    t = jnp.tanh(0.5 * x)
    return -2.0 * t / (1.0 - t)


def _lanes6(t):
    return jnp.concatenate([t] * 6, axis=1)


def _store_strided_view(t, out_ref, slab_ref, dil):
    if dil == 1:
        out_ref[...] = t.astype(out_ref.dtype)
        return
    T = t.shape[0]
    halves = A_OUT // LANES
    for s in range(halves):
        slab_ref[s] = t[:, s * LANES:(s + 1) * LANES]
    for r in range(dil):
        for s in range(halves):
            c0 = r * A_OUT + s * LANES
            out_ref[:, c0:c0 + LANES] = slab_ref[s, pl.ds(r, T // dil, stride=dil), :].astype(out_ref.dtype)


def _load_strided_view(view_ref, slab_ref, dil):
    if dil == 1:
        return view_ref[...]
    n = view_ref.shape[0]
    halves = A_OUT // LANES
    for r in range(dil):
        for s in range(halves):
            c0 = r * A_OUT + s * LANES
            slab_ref[s, pl.ds(r, n, stride=dil), :] = view_ref[:, c0:c0 + LANES]
    return jnp.concatenate([slab_ref[s] for s in range(halves)], axis=1)


def _qkv_body(x_ref, g1_ref, wq_ref, wk_ref, wv_ref, qg_ref, kg_ref, gbd_ref,
              cos_ref, sa_ref, sb_ref, *outs, split):
    h = _rms(x_ref[...], g1_ref[...]).astype(wq_ref.dtype)
    cos = _lanes6(cos_ref[...])
    sa = _lanes6(sa_ref[...])
    sb = _lanes6(sb_ref[...])
    gbd = gbd_ref[...]

    def normed_rotated(w_ref, hg_ref):
        t = _mm(h, w_ref[...])
        ss = _mm(t * t, gbd)
        tn = t * lax.rsqrt(ss * (1.0 / HEAD_DIM) + EPS) * hg_ref[...]
        return (tn * cos + pltpu.roll(tn, A_WIDTH - HEAD_DIM // 2, 1) * sa
                + pltpu.roll(tn, HEAD_DIM // 2, 1) * sb)

    q = normed_rotated(wq_ref, qg_ref) * (HEAD_DIM ** -0.5)
    k = normed_rotated(wk_ref, kg_ref)
    v = _mm(h, wv_ref[...])
    if split:
        *outs, slab_ref = outs
        for gi, (_, dil) in enumerate(A_GROUPS):
            sl = slice(gi * A_OUT, (gi + 1) * A_OUT)
            for j, t in enumerate((q, k, v)):
                _store_strided_view(t[:, sl], outs[3 * j + gi], slab_ref, dil)
        outs[9][...] = k
        outs[10][...] = v
    else:
        outs[0][...] = q
        outs[1][...] = k
        outs[2][...] = v


def _qkv(x2d, W, l, tabs, tm, split):
    R = x2d.shape[0]
    cos, sa, sb = tabs
    npos = cos.shape[0] // tm
    row = lambda i: (i, 0)
    pos = lambda i: (i % npos, 0)
    full = pl.BlockSpec((tm, A_WIDTH), row)
    scratch = []
    w_in = W["w_in"]
    wcol = lambda j: _lspec(w_in, l, (D_MODEL, A_WIDTH), (0, j))
    if split:
        dils = [dil for _, dil in A_GROUPS] * 3
        out_specs = [pl.BlockSpec((tm // d, d * A_OUT), row) for d in dils] + [full, full]
        out_shape = ([jax.ShapeDtypeStruct((R // d, d * A_OUT), BF16) for d in dils]
                     + [jax.ShapeDtypeStruct((R, A_WIDTH), F32)] * 2)
        scratch = [pltpu.VMEM((A_OUT // LANES, tm, LANES), F32)]
    else:
        out_specs = [full] * 3
        out_shape = [jax.ShapeDtypeStruct((R, A_WIDTH), F32)] * 3
    return pl.pallas_call(
        functools.partial(_qkv_body, split=split),
        grid=(R // tm,), scratch_shapes=scratch,
        in_specs=[pl.BlockSpec((tm, D_MODEL), row), _lspec(W["norm1_g"], l),
                  wcol(0), wcol(1), wcol(2),
                  _lspec(W["q_norm_g"], l), _lspec(W["k_norm_g"], l),
                  _const_spec((A_WIDTH, A_WIDTH)),
                  pl.BlockSpec((tm, LANES), pos), pl.BlockSpec((tm, LANES), pos),
                  pl.BlockSpec((tm, LANES), pos)],
        out_specs=out_specs, out_shape=out_shape,
        compiler_params=_cp("parallel"), name="qkv_proj",
    )(x2d, W["norm1_g"], w_in, w_in, w_in, W["q_norm_g"], W["k_norm_g"], W["gbd"], cos, sa, sb)


def _attn_group(q_ref, kp_ref, kc_ref, vp_ref, vc_ref, o_ref, l_ref, mask, lane_head):
    q = q_ref[...]
    zero = jnp.zeros_like(q)
    qs = jnp.concatenate([jnp.where(lane_head == hh, q, zero) for hh in range(A_HPG)], axis=0)
    kk = jnp.concatenate([kp_ref[...], kc_ref[...]], axis=0)
    vv = jnp.concatenate([vp_ref[...], vc_ref[...]], axis=0)
    s = jnp.where(mask, _bdot_nt(qs, kk), -jnp.inf)
    m = jnp.max(s, axis=-1, keepdims=True)
    e = jnp.exp(s - m)
    den = jnp.sum(e, axis=-1, keepdims=True)
    o4 = _bdot(e * (1.0 / den), vv)
    lse = m + jnp.log(den)
    o = o4[(A_HPG - 1) * BLK:]
    lo = jnp.broadcast_to(lse[(A_HPG - 1) * BLK:], (BLK, A_OUT))
    for hh in range(A_HPG - 2, -1, -1):
        sel = lane_head == hh
        o = jnp.where(sel, o4[hh * BLK:(hh + 1) * BLK], o)
        lo = jnp.where(sel, jnp.broadcast_to(lse[hh * BLK:(hh + 1) * BLK], (BLK, A_OUT)), lo)
    o_ref[...] = o
    l_ref[...] = lo


def _attn_prompt_body(*refs):
    i = pl.program_id(1)
    ins, outs = refs[:15], refs[15:]
    rows = A_HPG * BLK
    qi = lax.broadcasted_iota(jnp.int32, (rows, 2 * BLK), 0) % BLK
    kj = lax.broadcasted_iota(jnp.int32, (rows, 2 * BLK), 1)
    mask_cur = jnp.logical_and(kj >= BLK, kj - BLK <= qi)
    mask_prev = jnp.logical_and(kj < BLK, kj >= qi)
    lane_head = lax.broadcasted_iota(jnp.int32, (BLK, A_OUT), 1) // HEAD_DIM
    for gi, (_, dil) in enumerate(A_GROUPS):
        q_ref, kp_ref, kc_ref, vp_ref, vc_ref = ins[5 * gi:5 * gi + 5]
        o_ref, l_ref = outs[2 * gi:2 * gi + 2]
        mask = jnp.logical_or(mask_cur, jnp.logical_and(mask_prev, (i // dil) > 0))
        _attn_group(q_ref, kp_ref, kc_ref, vp_ref, vc_ref, o_ref, l_ref, mask, lane_head)


def _attn_prompt(qkv, B, S):
    nblk = S // BLK
    args, in_specs, out_specs, out_shape = [], [], [], []
    for gi, (_, dil) in enumerate(A_GROUPS):
        assert S % (dil * BLK) == 0
        rows = S // dil
        cur = lambda b, i, dil=dil: (b, i // dil, i % dil)
        prev = lambda b, i, dil=dil: (b, jnp.maximum(i // dil - 1, 0), i % dil)
        blk = (None, BLK, A_OUT)
        view = lambda t: t.reshape(B, rows, dil * A_OUT)
        qv, kv, vv = view(qkv[gi]), view(qkv[3 + gi]), view(qkv[6 + gi])
        args += [qv, kv, kv, vv, vv]
        in_specs += [pl.BlockSpec(blk, cur), pl.BlockSpec(blk, prev), pl.BlockSpec(blk, cur),
                     pl.BlockSpec(blk, prev), pl.BlockSpec(blk, cur)]
        out_specs += [pl.BlockSpec(blk, cur)] * 2
        out_shape += [jax.ShapeDtypeStruct((B, rows, dil * A_OUT), F32)] * 2
    res = pl.pallas_call(
        _attn_prompt_body, grid=(B, nblk), in_specs=in_specs, out_specs=out_specs,
        out_shape=out_shape, compiler_params=_cp("parallel", "parallel"), name="attn_prompt",
    )(*args)
    return [r.reshape(-1, r.shape[-1]) for r in res]


def _attn_step_body(q_ref, k_ref, v_ref, c1_ref, c2_ref, c3_ref, *outs):
    down = lambda r: jnp.broadcast_to(r[...], (LANES, A_WIDTH)).T
    q_cols, k_cols, v_cols = down(q_ref), down(k_ref), down(v_ref)
    lane2 = lax.broadcasted_iota(jnp.int32, (A_OUT, LANES), 1)
    for gi, (c_ref, (_, dil)) in enumerate(zip((c1_ref, c2_ref, c3_ref), A_GROUPS)):
        o_ref, l_ref = outs[2 * gi:2 * gi + 2]
        W = c_ref.shape[-1]
        rows = slice(gi * A_OUT, (gi + 1) * A_OUT)
        heads = lambda t: t[rows, 0:1].reshape(A_HPG, HEAD_DIM, 1)
        qh, kh, vh = heads(q_cols), heads(k_cols), heads(v_cols)
        pos = lax.broadcasted_iota(jnp.int32, (1, 1, W), 2)
        valid = (pos % dil) == 0
        s_c = jnp.where(valid, jnp.sum(c_ref[0] * qh, axis=1, keepdims=True), -jnp.inf)
        s_n = jnp.sum(qh * kh, axis=1, keepdims=True)
        m = jnp.maximum(jnp.max(s_c, axis=2, keepdims=True), s_n)
        e_c = jnp.exp(s_c - m)
        e_n = jnp.exp(s_n - m)
        den = jnp.sum(e_c, axis=2, keepdims=True) + e_n
        o = jnp.sum(c_ref[1] * (e_c / den), axis=2, keepdims=True) + (e_n / den) * vh
        lse = jnp.broadcast_to(m + jnp.log(den), (A_HPG, HEAD_DIM, 1))
        both = jnp.where(lane2 == 0, o.reshape(A_OUT, 1), jnp.where(lane2 == 1, lse.reshape(A_OUT, 1), 0.0))
        both_t = both.T
        o_ref[...] = both_t[0:1]
        l_ref[...] = both_t[1:2]


def _attn_step(q, k, v, caches, l):
    Bd = q.shape[0]
    row = lambda b: (b, 0, 0)
    args = [q.reshape(Bd, 1, A_WIDTH), k.reshape(Bd, 1, A_WIDTH), v.reshape(Bd, 1, A_WIDTH)]
    in_specs = [pl.BlockSpec((None, 1, A_WIDTH), row)] * 3
    for (window, dil), c in zip(A_GROUPS, caches):
        assert c.shape[2] == window and window % dil == 0
        args.append(jnp.transpose(c, (0, 1, 3, 4, 5, 2)))
        in_specs.append(pl.BlockSpec((None, None, 2, A_HPG, HEAD_DIM, window), lambda b: (l, b, 0, 0, 0, 0)))
    res = pl.pallas_call(
        _attn_step_body, grid=(Bd,), in_specs=in_specs,
        out_specs=[pl.BlockSpec((None, 1, A_OUT), row)] * 6,
        out_shape=[jax.ShapeDtypeStruct((Bd, 1, A_OUT), F32)] * 6,
        compiler_params=_cp("parallel"), name="attn_step",
    )(*args)
    return [r.reshape(Bd, A_OUT) for r in res]


def _lru_gates(xc, wa_ref, ba_ref, wx_ref, bx_ref, lam_ref):
    r = _sigmoid(_mm(xc, wa_ref[...]) + ba_ref[...])
    i = _sigmoid(_mm(xc, wx_ref[...]) + bx_ref[...])
    log_a = -LRU_C * r * jax.nn.softplus(-lam_ref[...])
    a = jnp.exp(log_a)
    b = jnp.sqrt(_neg_expm1(2.0 * log_a)) * (i * xc)
    return a, b


def _scan_rows(a, b, h0):
    T = a.shape[0]
    row = lax.broadcasted_iota(jnp.int32, a.shape, 0) % SUBLANES
    k = 1
    while k < SUBLANES:
        keep = row >= k
        a_s = jnp.where(keep, pltpu.roll(a, k, 0), 1.0)
        b_s = jnp.where(keep, pltpu.roll(b, k, 0), 0.0)
        b = a * b_s + b
        a = a * a_s
        k *= 2
    out, carry = [], h0
    for j in range(T // SUBLANES):
        rows = slice(j * SUBLANES, (j + 1) * SUBLANES)
        hj = b[rows] + a[rows] * carry
        out.append(hj)
        carry = hj[SUBLANES - 1:SUBLANES]
    return jnp.concatenate(out, axis=0)


def _lru_body(x_ref, g1_ref, w_ref, cw_ref, cb_ref, wa_ref, ba_ref, wx_ref, bx_ref, lam_ref,
              ob_ref, tail_ref, ext_ref, hc_ref, *, T):
    @pl.when(pl.program_id(1) == 0)
    def _():
        ext_ref[0:SUBLANES, :] = jnp.zeros((SUBLANES, LRU_WIDTH), F32)
        hc_ref[...] = jnp.zeros_like(hc_ref)

    h = _rms(x_ref[...], g1_ref[...])
    xb = _mm(h, w_ref[...])
    ext_ref[SUBLANES:SUBLANES + T, :] = xb
    xc = cb_ref[...]
    for kk in range(CONV_W - 1):
        xc = xc + ext_ref[pl.ds(SUBLANES - (CONV_W - 1) + kk, T), :] * cw_ref[kk:kk + 1, :]
    xc = xc + xb * cw_ref[CONV_W - 1:CONV_W, :]
    tail = xb[T - SUBLANES:T]
    ext_ref[0:SUBLANES, :] = tail
    tail_ref[...] = tail

    a, b = _lru_gates(xc, wa_ref, ba_ref, wx_ref, bx_ref, lam_ref)
    hfull = _scan_rows(a, b, hc_ref[0:1, :])
    ob_ref[...] = hfull
    hc_ref[...] = jnp.broadcast_to(hfull[T - 1:T], hc_ref.shape)


def _lru_prompt(x, W, l, T):
    B, S, _ = x.shape
    tile = lambda b, s: (b, s, 0)
    names = ("norm1_g", None, "conv_b_w", "conv_b_b", "lru_wa_bd", "lru_ba", "lru_wx_bd", "lru_bx", "lru_lambda")
    specs = [_lspec(W[n], l) if n else _lspec(W["w_in"], l, (D_MODEL, LRU_WIDTH), (0, IN_OFFS[3] // LRU_WIDTH))
             for n in names]
    return pl.pallas_call(
        functools.partial(_lru_body, T=T),
        grid=(B, S // T),
        in_specs=[pl.BlockSpec((None, T, D_MODEL), tile)] + specs,
        out_specs=[pl.BlockSpec((None, T, LRU_WIDTH), tile),
                   pl.BlockSpec((None, SUBLANES, LRU_WIDTH), lambda b, s: (b, 0, 0))],
        out_shape=[jax.ShapeDtypeStruct((B, S, LRU_WIDTH), F32),
                   jax.ShapeDtypeStruct((B, SUBLANES, LRU_WIDTH), F32)],
        scratch_shapes=[pltpu.VMEM((T + SUBLANES, LRU_WIDTH), F32),
                        pltpu.VMEM((SUBLANES, LRU_WIDTH), F32)],
        compiler_params=_cp("parallel", "arbitrary"), name="lru_prompt",
    )(x, *[W[n] if n else W["w_in"] for n in names])


def _gelu_ln(uv, lg_ref, lb_ref):
    uv = jax.nn.gelu(uv)
    u = uv[:, :SGU_WIDTH]
    v = uv[:, SGU_WIDTH:]
    mu = jnp.mean(v, axis=-1, keepdims=True)
    var = jnp.mean(jnp.square(v - mu), axis=-1, keepdims=True)
    v = (v - mu) * lax.rsqrt(var + EPS) * lg_ref[...] + lb_ref[...]
    return u, v


def _sgu_body(x_ref, g1_ref, w_ref, lg_ref, lb_ref, ws_ref, bs_ref, oc_ref, *, T):
    h = _rms(x_ref[...], g1_ref[...])
    u, v = _gelu_ln(_mm(h, w_ref[...]), lg_ref, lb_ref)
    qi = lax.broadcasted_iota(jnp.int32, (SGU_CHUNK, SGU_CHUNK), 0)
    kj = lax.broadcasted_iota(jnp.int32, (SGU_CHUNK, SGU_CHUNK), 1)
    tril = (kj <= qi).astype(F32)
    lane = lax.broadcasted_iota(jnp.int32, (SGU_CHUNK, SGU_WIDTH), 1)
    gw = SGU_WIDTH // SGU_GROUPS
    wms = [(ws_ref[g] * tril).astype(BF16) for g in range(SGU_GROUPS)]
    for c in range(T // SGU_CHUNK):
        rows = slice(c * SGU_CHUNK, (c + 1) * SGU_CHUNK)
        vc = v[rows].astype(BF16)
        mixed = jnp.dot(wms[SGU_GROUPS - 1], vc, preferred_element_type=F32)
        for g in range(SGU_GROUPS - 2, -1, -1):
            mixed = jnp.where(lane < (g + 1) * gw, jnp.dot(wms[g], vc, preferred_element_type=F32), mixed)
        oc_ref[rows, :] = u[rows] * (mixed + bs_ref[...])


def _sgu_prompt(x2d, W, l, T):
    R = x2d.shape[0]
    row = lambda i: (i, 0)
    return pl.pallas_call(
        functools.partial(_sgu_body, T=T),
        grid=(R // T,),
        in_specs=[pl.BlockSpec((T, D_MODEL), row), _lspec(W["norm1_g"], l),
                  _lspec(W["w_in"], l, (D_MODEL, 2 * SGU_WIDTH), (0, IN_OFFS[4] // (2 * SGU_WIDTH))),
                  _lspec(W["sgu_ln_g"], l), _lspec(W["sgu_ln_b"], l), _lspec(W["sgu_w"], l),
                  _lspec(W["sgu_b_tab"], l)],
        out_specs=pl.BlockSpec((T, SGU_WIDTH), row),
        out_shape=jax.ShapeDtypeStruct((R, SGU_WIDTH), F32),
        compiler_params=_cp("parallel"), name="sgu_prompt",
    )(x2d, W["norm1_g"], W["w_in"], W["sgu_ln_g"], W["sgu_ln_b"], W["sgu_w"], W["sgu_b_tab"])


def _ssd_dt_a(dtr, dtb_ref, alog_ref):
    lane = lax.broadcasted_iota(jnp.int32, (1, LANES), 1)
    dt = jax.nn.softplus(dtr + dtb_ref[...])
    A = jnp.where(lane < SSD_HEADS, -jnp.exp(alog_ref[...]), 0.0)
    return dt, A * dt


def _ssd_gate_norm(y, xs, z, dsk_ref, ng_ref):
    y = y + dsk_ref[...] * xs
    y = y * _silu(z)
    gw = SSD_INNER // SSD_GROUPS
    parts = []
    for g in range(SSD_GROUPS):
        yg = y[:, g * gw:(g + 1) * gw]
        parts.append(yg * lax.rsqrt(jnp.mean(yg * yg, axis=-1, keepdims=True) + EPS))
    return jnp.concatenate(parts, axis=1) * ng_ref[...]


def _ssd_chunk(xs, dt, a, Bm, Cm, st_ref, y_ref, row0):
    Q = SSD_CHUNK
    qi = lax.broadcasted_iota(jnp.int32, (Q, Q), 0)
    kj = lax.broadcasted_iota(jnp.int32, (Q, Q), 1)
    tril = kj <= qi
    cs = jnp.dot(tril.astype(F32), a, preferred_element_type=F32, precision=HIGHEST)
    cs_t = cs.T
    ecs = jnp.exp(cs)
    cs_last = cs[Q - 1:Q, :]
    to_end = jnp.exp(cs_last - cs)
    e_last = jnp.exp(cs_last)
    hpg = SSD_HEADS // SSD_GROUPS
    first = lax.broadcasted_iota(jnp.int32, (Q, LANES), 1) < HEAD_DIM
    first_row = lax.broadcasted_iota(jnp.int32, (LANES, 1), 0) < HEAD_DIM
    for g in range(SSD_GROUPS):
        Cg = Cm[:, g * SSD_STATE:(g + 1) * SSD_STATE].astype(BF16)
        Bg = Bm[:, g * SSD_STATE:(g + 1) * SSD_STATE].astype(BF16)
        G = _bdot_nt(Cg, Bg)
        for pp in range(hpg // 2):
            h0 = g * hpg + 2 * pp
            h1 = h0 + 1
            pl2 = slice(h0 * HEAD_DIM, (h0 + 2) * HEAD_DIM)
            per_head = lambda t: jnp.where(first, t[:, h0:h0 + 1], t[:, h1:h1 + 1])
            X2 = xs[:, pl2] * per_head(dt)
            L0 = jnp.exp(jnp.where(tril, cs[:, h0:h0 + 1] - cs_t[h0:h0 + 1, :], -jnp.inf))
            L1 = jnp.exp(jnp.where(tril, cs[:, h1:h1 + 1] - cs_t[h1:h1 + 1, :], -jnp.inf))
            Sp = st_ref[pl2, :]
            y = (_bdot(G * L0, jnp.where(first, X2, 0.0)) + _bdot(G * L1, jnp.where(first, 0.0, X2))
                 + per_head(ecs) * _bdot_nt(Cg, Sp))
            y_ref[row0:row0 + Q, pl2] = y
            keep = jnp.where(first_row, e_last[:, h0:h0 + 1], e_last[:, h1:h1 + 1])
            st_ref[pl2, :] = keep * Sp + _bdot_tn(X2 * per_head(to_end), Bg)


def _ssd_body(x_ref, g1_ref, wz_ref, wxbc_ref, wdt_ref, cw_ref, cb_ref, dtb_ref, alog_ref, dsk_ref,
              ng_ref, od_ref, tail_ref, fin_ref, ext_ref, st_ref, y_ref, *, T):
    @pl.when(pl.program_id(1) == 0)
    def _():
        ext_ref[0:SUBLANES, :] = jnp.zeros((SUBLANES, SSD_CONV_CH), F32)
        st_ref[...] = jnp.zeros_like(st_ref)

    h = _rms(x_ref[...], g1_ref[...]).astype(BF16)
    z = _mm(h, wz_ref[...])
    xbc = _mm(h, wxbc_ref[...])
    dtr = _mm(h, wdt_ref[...])
    ext_ref[SUBLANES:SUBLANES + T, :] = xbc
    xc = cb_ref[...]
    for kk in range(CONV_W - 1):
        xc = xc + ext_ref[pl.ds(SUBLANES - (CONV_W - 1) + kk, T), :] * cw_ref[kk:kk + 1, :]
    xc = xc + xbc * cw_ref[CONV_W - 1:CONV_W, :]
    tail = xbc[T - SUBLANES:T]
    ext_ref[0:SUBLANES, :] = tail
    tail_ref[...] = tail

    xc = _silu(xc)
    xs = xc[:, :SSD_INNER]
    gn = SSD_GROUPS * SSD_STATE
    Bm = xc[:, SSD_INNER:SSD_INNER + gn]
    Cm = xc[:, SSD_INNER + gn:]
    dt, a = _ssd_dt_a(dtr, dtb_ref, alog_ref)
    for c in range(T // SSD_CHUNK):
        rows = slice(c * SSD_CHUNK, (c + 1) * SSD_CHUNK)
        _ssd_chunk(xs[rows], dt[rows], a[rows], Bm[rows], Cm[rows], st_ref, y_ref, c * SSD_CHUNK)
    od_ref[...] = _ssd_gate_norm(y_ref[...], xs, z, dsk_ref, ng_ref)
    fin_ref[...] = st_ref[...]


def _ssd_prompt(x, W, l, T):
    B, S, _ = x.shape
    tile = lambda b, s: (b, s, 0)
    perb = lambda b, s: (b, 0, 0)
    wx = W["w_xbcdt"]
    return pl.pallas_call(
        functools.partial(_ssd_body, T=T),
        grid=(B, S // T),
        in_specs=[pl.BlockSpec((None, T, D_MODEL), tile), _lspec(W["norm1_g"], l),
                  _lspec(W["w_in"], l, (D_MODEL, SSD_INNER), (0, IN_OFFS[5] // SSD_INNER)),
                  _lspec(wx, l, (D_MODEL, SSD_CONV_CH), (0, 0)),
                  _lspec(wx, l, (D_MODEL, LANES), (0, SSD_CONV_CH // LANES)),
                  _lspec(W["ssd_conv_w"], l), _lspec(W["ssd_conv_b"], l), _lspec(W["ssd_dt_bias"], l),
                  _lspec(W["ssd_a_log"], l), _lspec(W["ssd_d_tab"], l), _lspec(W["ssd_norm_g"], l)],
        out_specs=[pl.BlockSpec((None, T, SSD_INNER), tile),
                   pl.BlockSpec((None, SUBLANES, SSD_CONV_CH), perb),
                   pl.BlockSpec((None, SSD_INNER, SSD_STATE), perb)],
        out_shape=[jax.ShapeDtypeStruct((B, S, SSD_INNER), F32),
                   jax.ShapeDtypeStruct((B, SUBLANES, SSD_CONV_CH), F32),
                   jax.ShapeDtypeStruct((B, SSD_INNER, SSD_STATE), F32)],
        scratch_shapes=[pltpu.VMEM((T + SUBLANES, SSD_CONV_CH), F32),
                        pltpu.VMEM((SSD_INNER, SSD_STATE), F32),
                        pltpu.VMEM((T, SSD_INNER), F32)],
        compiler_params=_cp("parallel", "arbitrary"), name="ssd_prompt",
    )(x, W["norm1_g"], W["w_in"], wx, wx, W["ssd_conv_w"], W["ssd_conv_b"],
      W["ssd_dt_bias"], W["ssd_a_log"], W["ssd_d_tab"], W["ssd_norm_g"])


def _proj_body(x_ref, g1_ref, w_ref, o_ref):
    o_ref[...] = _mm(_rms(x_ref[...], g1_ref[...]), w_ref[...])


def _proj(x2d, W, l, col0, ncols, tn):
    R = x2d.shape[0]
    assert col0 % tn == 0
    j0 = col0 // tn
    nblk = pl.cdiv(ncols, tn)
    return pl.pallas_call(
        _proj_body, grid=(nblk,),
        in_specs=[_const_spec((R, D_MODEL)), _lspec(W["norm1_g"], l),
                  pl.BlockSpec((None, D_MODEL, tn), lambda j: (l, 0, j0 + j))],
        out_specs=pl.BlockSpec((R, tn), lambda j: (0, j)),
        out_shape=jax.ShapeDtypeStruct((R, nblk * tn), F32),
        compiler_params=_cp("parallel"), name="proj_rest",
    )(x2d, W["norm1_g"], W["w_in"])


def _conv_step(xnew, st_ref, cw_ref, cb_ref, C):
    out = cb_ref[...]
    for kk in range(CONV_W - 1):
        out = out + st_ref[:, kk * C:(kk + 1) * C] * cw_ref[kk:kk + 1, :]
    return out + xnew * cw_ref[CONV_W - 1:CONV_W, :]


def _dec_branches_body(proj_ref, dtr_ref, scb_ref, hb_ref, scd_ref,
                       cwb_ref, cbb_ref, wa_ref, ba_ref, wx_ref, bx_ref, lam_ref,
                       lg_ref, lb_ref, w00_ref, b0_ref,
                       cwd_ref, cbd_ref, dtb_ref, alog_ref,
                       ob_ref, ncb_ref, oc_ref, vc_ref, ncd_ref, xs_ref, xdt_ref, ea_ref, bc_ref,
                       *, offs):
    o_xb, o_uv, o_xbc = offs
    xb = proj_ref[:, o_xb:o_xb + LRU_WIDTH]
    xc = _conv_step(xb, scb_ref, cwb_ref, cbb_ref, LRU_WIDTH)
    a, b = _lru_gates(xc, wa_ref, ba_ref, wx_ref, bx_ref, lam_ref)
    ob_ref[...] = a * hb_ref[...] + b
    ncb_ref[:, 0:2 * LRU_WIDTH] = scb_ref[:, LRU_WIDTH:3 * LRU_WIDTH]
    ncb_ref[:, 2 * LRU_WIDTH:3 * LRU_WIDTH] = xb
    u, v = _gelu_ln(proj_ref[:, o_uv:o_uv + 2 * SGU_WIDTH], lg_ref, lb_ref)
    vc_ref[...] = v
    oc_ref[...] = u * (w00_ref[...] * v + b0_ref[...])
    xbc = proj_ref[:, o_xbc:o_xbc + SSD_CONV_CH]
    xcd = _silu(_conv_step(xbc, scd_ref, cwd_ref, cbd_ref, SSD_CONV_CH))
    ncd_ref[:, 0:2 * SSD_CONV_CH] = scd_ref[:, SSD_CONV_CH:3 * SSD_CONV_CH]
    ncd_ref[:, 2 * SSD_CONV_CH:3 * SSD_CONV_CH] = xbc
    xs = xcd[:, :SSD_INNER]
    dt, a_dt = _ssd_dt_a(dtr_ref[...], dtb_ref, alog_ref)
    hrow = lax.broadcasted_iota(jnp.int32, (LANES, SSD_INNER), 0)
    hlane = lax.broadcasted_iota(jnp.int32, (LANES, SSD_INNER), 1)
    expand = (hlane // HEAD_DIM == hrow).astype(F32)
    dt_e = jnp.dot(dt, expand, preferred_element_type=F32, precision=HIGHEST)
    a_e = jnp.dot(a_dt, expand, preferred_element_type=F32, precision=HIGHEST)
    xs_ref[...] = xs
    xdt_ref[...] = xs * dt_e
    ea_ref[...] = jnp.exp(a_e)
    bc_ref[...] = xcd[:, SSD_INNER:]


def _dec_branches(proj, dtr, offs, scb, hb, scd, W, l):
    Bd = proj.shape[0]
    f = lambda n: jax.ShapeDtypeStruct((Bd, n), F32)
    whole = lambda a: pl.BlockSpec(a.shape, lambda i: (0,) * a.ndim)
    names = ("conv_b_w", "conv_b_b", "lru_wa_bd", "lru_ba", "lru_wx_bd", "lru_bx", "lru_lambda",
             "sgu_ln_g", "sgu_ln_b", "sgu_w00_tab", "sgu_b0_tab",
             "ssd_conv_w", "ssd_conv_b", "ssd_dt_bias", "ssd_a_log")
    acts = (proj, dtr, scb, hb, scd)
    widths = (LRU_WIDTH, 3 * LRU_WIDTH, SGU_WIDTH, SGU_WIDTH, 3 * SSD_CONV_CH, SSD_INNER, SSD_INNER,
              SSD_INNER, 2 * SSD_GROUPS * SSD_STATE)
    return pl.pallas_call(
        functools.partial(_dec_branches_body, offs=offs),
        grid=(1,),
        in_specs=[whole(a) for a in acts] + [_lspec(W[n], l) for n in names],
        out_specs=[pl.BlockSpec((Bd, n), lambda i: (0, 0)) for n in widths],
        out_shape=[f(n) for n in widths],
        compiler_params=_cp("arbitrary"), name="dec_branches",
    )(*acts, *[W[n] for n in names])


def _dec_ssd_body(xs_ref, xdt_ref, ea_ref, bc_ref, z_ref, st_ref, dsk_ref, ng_ref, od_ref, ns_ref):
    gn = SSD_GROUPS * SSD_STATE
    half = SSD_INNER // SSD_GROUPS
    ridx = lax.broadcasted_iota(jnp.int32, (LANES, SSD_INNER), 0)
    rows = jnp.where(ridx == 0, jnp.broadcast_to(xdt_ref[...], (LANES, SSD_INNER)),
                     jnp.where(ridx == 1, jnp.broadcast_to(ea_ref[...], (LANES, SSD_INNER)), 0.0))
    cols = rows.T
    xdt_c = cols[:, 0:1]
    ea_c = cols[:, 1:2]
    bc = bc_ref[...]
    Bm, Cm = bc[:, :gn], bc[:, gn:]
    rowi = lax.broadcasted_iota(jnp.int32, (SSD_INNER, SSD_STATE), 0)
    b_full = jnp.where(rowi < half, jnp.broadcast_to(Bm[:, :SSD_STATE], (SSD_INNER, SSD_STATE)),
                       jnp.broadcast_to(Bm[:, SSD_STATE:], (SSD_INNER, SSD_STATE)))
    new = ea_c * st_ref[...] + xdt_c * b_full
    ns_ref[...] = new
    cidx = lax.broadcasted_iota(jnp.int32, (SUBLANES, SSD_STATE), 0)
    c8 = jnp.where(cidx == 0, jnp.broadcast_to(Cm[:, :SSD_STATE], (SUBLANES, SSD_STATE)),
                   jnp.where(cidx == 1, jnp.broadcast_to(Cm[:, SSD_STATE:], (SUBLANES, SSD_STATE)), 0.0))
    y8 = lax.dot_general(c8, new, (((1,), (1,)), ((), ())), preferred_element_type=F32,
                         precision=HIGHEST)
    lane = lax.broadcasted_iota(jnp.int32, (1, SSD_INNER), 1)
    y = jnp.where(lane < half, y8[0:1], y8[1:2])
    od_ref[...] = _ssd_gate_norm(y, xs_ref[...], z_ref[...], dsk_ref, ng_ref)


def _dec_ssd(xs, xdt, ea, bc, z, state_all, W, l):
    Bd = xs.shape[0]
    row = lambda b: (b, 0, 0)
    r3 = lambda t: t.reshape(Bd, 1, t.shape[-1])
    st = state_all.reshape(state_all.shape[0], Bd, SSD_INNER, SSD_STATE)
    od, ns = pl.pallas_call(
        _dec_ssd_body, grid=(Bd,),
        in_specs=[pl.BlockSpec((None, 1, SSD_INNER), row)] * 3
                 + [pl.BlockSpec((None, 1, 2 * SSD_GROUPS * SSD_STATE), row),
                    pl.BlockSpec((None, 1, SSD_INNER), row),
                    pl.BlockSpec((None, None, SSD_INNER, SSD_STATE), lambda b: (l, b, 0, 0)),
                    _lspec(W["ssd_d_tab"], l), _lspec(W["ssd_norm_g"], l)],
        out_specs=[pl.BlockSpec((None, 1, SSD_INNER), row),
                   pl.BlockSpec((None, SSD_INNER, SSD_STATE), row)],
        out_shape=[jax.ShapeDtypeStruct((Bd, 1, SSD_INNER), F32),
                   jax.ShapeDtypeStruct((Bd, SSD_INNER, SSD_STATE), F32)],
        compiler_params=_cp("parallel"), name="dec_ssd",
    )(r3(xs), r3(xdt), r3(ea), r3(bc), r3(z), st, W["ssd_d_tab"], W["ssd_norm_g"])
    return od.reshape(Bd, SSD_INNER), ns.reshape(state_all.shape[1:])


def _merge_body(x_ref, o1_ref, l1_ref, o2_ref, l2_ref, o3_ref, l3_ref, ob_ref, oc_ref, od_ref,
                g1_ref, wg_ref, wpa_ref, wpb_ref, wpc_ref, wpd_ref, wo_ref, out_ref, *slab, dils, pre_gates):
    x = x_ref[...]
    h = None if pre_gates else _rms(x, g1_ref[...]).astype(wg_ref.dtype)
    tok = lambda ref, d: _load_strided_view(ref, slab[0], d) if d > 1 else ref[...]
    o1, o2, o3 = (tok(r, d) for r, d in zip((o1_ref, o2_ref, o3_ref), dils))
    l1, l2, l3 = (tok(r, d) for r, d in zip((l1_ref, l2_ref, l3_ref), dils))
    m = jnp.maximum(jnp.maximum(l1, l2), l3)
    e1, e2, e3 = jnp.exp(l1 - m), jnp.exp(l2 - m), jnp.exp(l3 - m)
    den = e1 + e2 + e3
    oa = (e1 / den) * o1 + (e2 / den) * o2 + (e3 / den) * o3
    merged = None
    for bi, (o, w_ref) in enumerate(((oa, wpa_ref), (ob_ref[...], wpb_ref),
                                     (oc_ref[...], wpc_ref), (od_ref[...], wpd_ref))):
        cols = slice(bi * D_MODEL, (bi + 1) * D_MODEL)
        gate = _sigmoid(wg_ref[:, cols] if pre_gates else _mm(h, wg_ref[:, cols]))
        term = gate * _mm(o, w_ref[...])
        merged = term if merged is None else merged + term
    out_ref[...] = x + _mm(merged, wo_ref[...])


def _merge(x2d, att, ob, oc, od, W, l, tm, dils, gates=None):
    R = x2d.shape[0]
    row = lambda i: (i, 0)
    names = ("norm1_g", "w_gates", "w_pa", "w_pb", "w_pc", "w_pd", "w_o")
    params = [W[n] for n in names]
    pspecs = [_lspec(a, l) for a in params]
    if gates is not None:
        params[1], pspecs[1] = gates, pl.BlockSpec((tm, N_BRANCH * D_MODEL), row)
    att_specs = [pl.BlockSpec((tm // d, d * A_OUT), row) for d in dils for _ in range(2)]
    scratch = [pltpu.VMEM((A_OUT // LANES, tm, LANES), F32)] if max(dils) > 1 else []
    return pl.pallas_call(
        functools.partial(_merge_body, dils=dils, pre_gates=gates is not None), grid=(R // tm,),
        in_specs=[pl.BlockSpec((tm, D_MODEL), row)] + att_specs
                 + [pl.BlockSpec((tm, LRU_WIDTH), row)] * 3 + pspecs,
        out_specs=pl.BlockSpec((tm, D_MODEL), row),
        out_shape=jax.ShapeDtypeStruct((R, D_MODEL), F32), scratch_shapes=scratch,
        compiler_params=_cp("parallel"), name="merge",
    )(x2d, *att, ob, oc, od, *params)


def _route(logits):
    lane = lax.broadcasted_iota(jnp.int32, logits.shape, 1)
    big = jnp.int32(LANES)
    ninf = -jnp.inf
    gl = jnp.where(lane < N_EXPERT_GROUPS, logits, ninf)
    gm = jnp.max(gl, axis=-1, keepdims=True)
    gsel = jnp.min(jnp.where(gl == gm, lane, big), axis=-1, keepdims=True)
    pg = 1.0 / jnp.sum(jnp.exp(gl - gm), axis=-1, keepdims=True)
    lo = N_EXPERT_GROUPS + EXPERTS_PER_GROUP * gsel
    el = jnp.where(jnp.logical_and(lane >= lo, lane < lo + EXPERTS_PER_GROUP), logits, ninf)
    t1 = jnp.max(el, axis=-1, keepdims=True)
    i1 = jnp.min(jnp.where(el == t1, lane, big), axis=-1, keepdims=True)
    el2 = jnp.where(lane == i1, ninf, el)
    t2 = jnp.max(el2, axis=-1, keepdims=True)
    i2 = jnp.min(jnp.where(el2 == t2, lane, big), axis=-1, keepdims=True)
    e2 = jnp.exp(t2 - t1)
    den = 1.0 + e2
    w1 = (1.0 / den) * pg
    w2 = (e2 / den) * pg
    return jnp.where(lane == i1, w1, 0.0) + jnp.where(lane == i2, w2, 0.0), gsel


def _moe_body(x_ref, g2_ref, wr_ref, br_ref, wg_ref, wu_ref, wd_ref, out_ref, h_ref, comb_ref, acc_ref):
    e = pl.program_id(1)

    @pl.when(e == 0)
    def _():
        h = _rms(x_ref[...], g2_ref[...])
        h_ref[...] = h.astype(h_ref.dtype)
        logits = jnp.dot(h, wr_ref[...], preferred_element_type=F32, precision=HIGHEST) + br_ref[...]
        comb_ref[...] = _route(logits)[0]
        acc_ref[...] = jnp.zeros_like(acc_ref)

    h = h_ref[...]
    lane = lax.broadcasted_iota(jnp.int32, comb_ref.shape, 1)
    c = jnp.sum(jnp.where(lane == N_EXPERT_GROUPS + e, comb_ref[...], 0.0), axis=-1, keepdims=True)
    hg = _mm(h, wg_ref[...])
    hu = _mm(h, wu_ref[...])
    act = _silu(hg) * hu * c
    acc_ref[...] += _mm(act, wd_ref[...])

    @pl.when(e == N_EXPERTS - 1)
    def _():
        out_ref[...] = x_ref[...] + acc_ref[...]


def _moe(x2d, W, l, tm):
    R = x2d.shape[0]
    row = lambda i, e: (i, 0)
    expert = lambda r, c: pl.BlockSpec((None, None, r, c), lambda i, e: (l, e, 0, 0))
    return pl.pallas_call(
        _moe_body, grid=(R // tm, N_EXPERTS),
        in_specs=[pl.BlockSpec((tm, D_MODEL), row), _lspec(W["norm2_g"], l), _lspec(W["w_router"], l),
                  _lspec(W["b_router"], l),
                  expert(D_MODEL, D_EXPERT), expert(D_MODEL, D_EXPERT), expert(D_EXPERT, D_MODEL)],
        out_specs=pl.BlockSpec((tm, D_MODEL), row),
        out_shape=jax.ShapeDtypeStruct((R, D_MODEL), F32),
        scratch_shapes=[pltpu.VMEM((tm, D_MODEL), W["moe_w_gate"].dtype), pltpu.VMEM((tm, LANES), F32),
                        pltpu.VMEM((tm, D_MODEL), F32)],
        compiler_params=_cp("parallel", "arbitrary"), name="moe",
    )(x2d, W["norm2_g"], W["w_router"], W["b_router"], W["moe_w_gate"], W["moe_w_up"],
      W["moe_w_down"])


GID_LANE = N_EXPERT_GROUPS + N_EXPERTS
SEG_ALIGN = 16
MOE_RB = TM_MOE // N_EXPERT_GROUPS + 32
PERM_BLK = 128
UNPERM_BLK = 256
MOE_USED = -(-(TM_MOE + N_EXPERT_GROUPS * SEG_ALIGN) // PERM_BLK) * PERM_BLK
MOE_TMP = MOE_USED + MOE_RB
assert MOE_RB % SEG_ALIGN == 0 and TM_MOE % UNPERM_BLK == 0


def _route_body(x_ref, g2_ref, wr_ref, br_ref, comb_ref, cnt_ref):
    h = _rms(x_ref[...], g2_ref[...])
    logits = _dot3(h, wr_ref[...]) + br_ref[...]
    comb, gsel = _route(logits)
    lane = lax.broadcasted_iota(jnp.int32, comb.shape, 1)
    comb_ref[...] = jnp.where(lane == GID_LANE, gsel.astype(F32), comb)
    counts = jnp.sum((lane == gsel).astype(F32), axis=0, keepdims=True)
    cnt_ref[...] = jnp.broadcast_to(counts, cnt_ref.shape)


def _route_call(x2d, W, l, tm):
    R = x2d.shape[0]
    return pl.pallas_call(
        _route_body, grid=(R // tm,),
        in_specs=[pl.BlockSpec((tm, D_MODEL), lambda i: (i, 0)), _lspec(W["norm2_g"], l),
                  _lspec(W["w_router"], l), _lspec(W["b_router"], l)],
        out_specs=[pl.BlockSpec((tm, LANES), lambda i: (i, 0)),
                   pl.BlockSpec((None, SUBLANES, LANES), lambda i: (i, 0, 0))],
        out_shape=[jax.ShapeDtypeStruct((R, LANES), F32),
                   jax.ShapeDtypeStruct((R // tm, SUBLANES, LANES), F32)],
        compiler_params=_cp("parallel"), name="moe_route",
    )(x2d, W["norm2_g"], W["w_router"], W["b_router"])


def _moe_sorted_body(off_ref, nch_ref, x_ref, comb_ref, g2_ref, wg_ref, wu_ref, wd_ref, out_ref,
                     sh_ref, sc_ref, acc_ref, rcol_ref, lo_ref, *, TM, RB):
    i = pl.program_id(0)
    g = pl.program_id(1)
    TMP = sh_ref.shape[0]

    @pl.when(g == 0)
    def _():
        h2 = _rms(x_ref[...], g2_ref[...]).astype(BF16)
        comb = comb_ref[...]
        lane = lax.broadcasted_iota(jnp.int32, comb.shape, 1)
        gid = jnp.sum(jnp.where(lane == GID_LANE, comb, 0.0), axis=-1, keepdims=True)
        onehot = jnp.logical_and(lane < N_EXPERT_GROUPS, lane.astype(F32) == gid).astype(F32)
        trow = lax.broadcasted_iota(jnp.int32, onehot.shape, 0)
        run = onehot
        k = 1
        while k < TM:
            run = run + jnp.where(trow >= k, pltpu.roll(run, k, 0), 0.0)
            k *= 2
        prefix = run - onehot
        lane1 = lax.broadcasted_iota(jnp.int32, (1, LANES), 1)
        offv = jnp.zeros((1, LANES), F32)
        for gg in range(N_EXPERT_GROUPS):
            offv = jnp.where(lane1 == gg, off_ref[i * N_EXPERT_GROUPS + gg].astype(F32), offv)
        rank = jnp.sum(onehot * (offv + prefix), axis=-1, keepdims=True)
        rank_b = jnp.broadcast_to(rank, (TM, LANES))
        rcol_ref[...] = rank_b
        rank_row = rank_b.T[0:1]
        c1 = comb.astype(BF16)
        c2 = (comb - c1.astype(F32)).astype(BF16)
        payload = jnp.concatenate([h2, c1, c2], axis=1)
        for blk in range(MOE_USED // PERM_BLK):
            rows = slice(blk * PERM_BLK, (blk + 1) * PERM_BLK)
            srow = lax.broadcasted_iota(jnp.int32, (PERM_BLK, TM), 0) + blk * PERM_BLK
            perm = (srow.astype(F32) == rank_row).astype(BF16)
            moved = jnp.dot(perm, payload, preferred_element_type=F32)
            sh_ref[rows, :] = moved[:, :D_MODEL].astype(BF16)
            sc_ref[rows, :] = moved[:, D_MODEL:D_MODEL + LANES] + moved[:, D_MODEL + LANES:]
        sh_ref[MOE_USED:, :] = jnp.zeros((TMP - MOE_USED, D_MODEL), BF16)
        sc_ref[MOE_USED:, :] = jnp.zeros((TMP - MOE_USED, LANES), F32)
        acc_ref[...] = jnp.zeros_like(acc_ref)

    off = off_ref[i * N_EXPERT_GROUPS + g]
    lane_c = lax.broadcasted_iota(jnp.int32, (RB, LANES), 1)

    def chunk(j, carry):
        r0 = pl.multiple_of(off + j * RB, SEG_ALIGN)
        rows = sh_ref[pl.ds(r0, RB), :]
        cc = sc_ref[pl.ds(r0, RB), :]
        y = jnp.zeros((RB, D_MODEL), F32)
        for e in range(EXPERTS_PER_GROUP):
            ce = jnp.sum(jnp.where(lane_c == N_EXPERT_GROUPS + EXPERTS_PER_GROUP * g + e, cc, 0.0),
                         axis=-1, keepdims=True)
            hg = jnp.dot(rows, wg_ref[e], preferred_element_type=F32)
            hu = jnp.dot(rows, wu_ref[e], preferred_element_type=F32)
            y = y + _bdot(_silu(hg) * hu * ce, wd_ref[e])
        acc_ref[pl.ds(r0, RB), :] += y
        return carry

    lax.fori_loop(0, nch_ref[i * N_EXPERT_GROUPS + g], chunk, 0)

    @pl.when(g == N_EXPERT_GROUPS - 1)
    def _():
        for blk in range(MOE_USED // PERM_BLK):
            rows = slice(blk * PERM_BLK, (blk + 1) * PERM_BLK)
            a = acc_ref[rows, :]
            hi = a.astype(BF16)
            sh_ref[rows, :] = hi
            lo_ref[rows, :] = (a - hi.astype(F32)).astype(BF16)
        scol = lax.broadcasted_iota(jnp.int32, (UNPERM_BLK, MOE_USED), 1).astype(F32)
        for blk in range(TM // UNPERM_BLK):
            rows = slice(blk * UNPERM_BLK, (blk + 1) * UNPERM_BLK)
            unperm = (scol == rcol_ref[rows, 0:1]).astype(BF16)
            out_ref[rows, :] = x_ref[rows, :] + (
                jnp.dot(unperm, sh_ref[0:MOE_USED, :], preferred_element_type=F32)
                + jnp.dot(unperm, lo_ref[0:MOE_USED, :], preferred_element_type=F32))


def _moe_sorted(x2d, W, l):
    R = x2d.shape[0]
    TM, RB, TMP = TM_MOE, MOE_RB, MOE_TMP
    nt = R // TM
    comb, counts = _route_call(x2d, W, l, TM)
    cnt = counts[:, 0, :N_EXPERT_GROUPS].astype(jnp.int32)
    padded = (cnt + SEG_ALIGN - 1) // SEG_ALIGN * SEG_ALIGN
    off = (jnp.cumsum(padded, axis=1) - padded).reshape(-1)
    nch = ((cnt + RB - 1) // RB).reshape(-1)
    row = lambda i, g, *_: (i, 0)
    const = lambda a: pl.BlockSpec((None,) + tuple(a.shape[1:]), lambda i, g, *_: (l,) + (0,) * (a.ndim - 1))
    grp = lambda r, c: pl.BlockSpec((None, None, EXPERTS_PER_GROUP, r, c), lambda i, g, *_: (l, g, 0, 0, 0))
    gview = lambda a: a.reshape(a.shape[0], N_EXPERT_GROUPS, EXPERTS_PER_GROUP, a.shape[2], a.shape[3])
    return pl.pallas_call(
        functools.partial(_moe_sorted_body, TM=TM, RB=RB),
        grid_spec=pltpu.PrefetchScalarGridSpec(
            num_scalar_prefetch=2, grid=(nt, N_EXPERT_GROUPS),
            in_specs=[pl.BlockSpec((TM, D_MODEL), row, pipeline_mode=pl.Buffered(1)),
                      pl.BlockSpec((TM, LANES), row), const(W["norm2_g"]),
                      grp(D_MODEL, D_EXPERT), grp(D_MODEL, D_EXPERT), grp(D_EXPERT, D_MODEL)],
            out_specs=pl.BlockSpec((TM, D_MODEL), row),
            scratch_shapes=[pltpu.VMEM((TMP, D_MODEL), BF16), pltpu.VMEM((TMP, LANES), F32),
                            pltpu.VMEM((TMP, D_MODEL), F32), pltpu.VMEM((TM, LANES), F32),
                            pltpu.VMEM((TMP, D_MODEL), BF16)]),
        out_shape=jax.ShapeDtypeStruct((R, D_MODEL), F32),
        compiler_params=_cp("parallel", "arbitrary"), name="moe_sorted",
    )(off, nch, x2d, comb, W["norm2_g"], gview(W["moe_w_gate"]), gview(W["moe_w_up"]), gview(W["moe_w_down"]))


def _rope_tables(pos):
    half = HEAD_DIM // 2
    inv = ROPE_THETA ** (-jnp.arange(half, dtype=F32) / half)
    ang = pos.astype(F32)[:, None] * inv[None]
    c, s = jnp.cos(ang), jnp.sin(ang)
    z = jnp.zeros_like(s)
    reps = LANES // HEAD_DIM
    return (jnp.tile(jnp.concatenate([c, c], 1), (1, reps)),
            jnp.tile(jnp.concatenate([-s, z], 1), (1, reps)),
            jnp.tile(jnp.concatenate([z, s], 1), (1, reps)))


def _block_diag(w):
    L, n, k, _ = w.shape
    eye = jnp.eye(n, dtype=w.dtype)
    return (eye[None, :, None, :, None] * w[:, :, :, None, :]).reshape(L, n * k, n * k)


def _small_params(p):
    vec = lambda t: t.reshape(t.shape[0], 1, -1)
    W = {}
    for n in ("norm1_g", "norm2_g", "conv_b_b", "lru_ba", "lru_bx", "lru_lambda", "sgu_ln_g", "sgu_ln_b",
              "ssd_conv_b", "ssd_norm_g"):
        W[n] = vec(p[n])
    reps = A_WIDTH // HEAD_DIM
    W["q_norm_g"] = vec(jnp.tile(p["q_norm_g"], (1, reps)))
    W["k_norm_g"] = vec(jnp.tile(p["k_norm_g"], (1, reps)))
    W["conv_b_w"] = p["conv_b_w"]
    W["ssd_conv_w"] = p["ssd_conv_w"]
    W["sgu_w"] = p["sgu_w"]
    gw = SGU_WIDTH // SGU_GROUPS
    W["sgu_b_tab"] = jnp.repeat(jnp.swapaxes(p["sgu_b"], 1, 2), gw, axis=2)
    W["sgu_w00_tab"] = vec(jnp.repeat(p["sgu_w"][:, :, 0, 0], gw, axis=1))
    W["sgu_b0_tab"] = vec(jnp.repeat(p["sgu_b"][:, :, 0], gw, axis=1))
    padh = lambda t: vec(jnp.pad(t, ((0, 0), (0, LANES - SSD_HEADS))))
    W["ssd_dt_bias"], W["ssd_a_log"] = padh(p["ssd_dt_bias"]), padh(p["ssd_a_log"])
    W["ssd_d_tab"] = vec(jnp.repeat(p["ssd_d"], HEAD_DIM, axis=1))
    we = jnp.transpose(p["router_exp_w"], (0, 2, 1, 3)).reshape(-1, D_MODEL, N_EXPERTS)
    wr = jnp.concatenate([p["router_group_w"], we], axis=2)
    W["w_router"] = jnp.pad(wr, ((0, 0), (0, 0), (0, LANES - wr.shape[2])))
    br = jnp.concatenate([p["router_group_b"], p["router_exp_b"].reshape(-1, N_EXPERTS)], axis=1)
    W["b_router"] = vec(jnp.pad(br, ((0, 0), (0, LANES - br.shape[1]))))
    return W


def _matrix_params(p, wdt):
    W = {}
    w_in = p["w_in"].astype(wdt)
    W["w_in"] = w_in
    o_xbc, o_dt, o_g = IN_OFFS[6], IN_OFFS[7], IN_OFFS[8]
    W["w_xbcdt"] = jnp.concatenate(
        [w_in[:, :, o_xbc:o_dt], jnp.pad(w_in[:, :, o_dt:o_g], ((0, 0), (0, 0), (0, LANES - SSD_HEADS)))], axis=2)
    W["w_gates"] = w_in[:, :, o_g:]
    reps = A_WIDTH // HEAD_DIM
    W["gbd"] = _block_diag(jnp.ones((1, reps, HEAD_DIM, HEAD_DIM), wdt))[0]
    W["lru_wa_bd"] = _block_diag(p["lru_wa"]).astype(wdt)
    W["lru_wx_bd"] = _block_diag(p["lru_wx"]).astype(wdt)
    for n in ("w_pa", "w_pb", "w_pc", "w_pd", "w_o", "moe_w_gate", "moe_w_up", "moe_w_down"):
        W[n] = p[n].astype(wdt)
    return W


def _kv_rows(k, v, gi):
    sl = slice(gi * A_OUT, (gi + 1) * A_OUT)
    shp = k.shape[:-1] + (A_HPG, HEAD_DIM)
    return jnp.stack([k[..., sl].reshape(shp), v[..., sl].reshape(shp)], axis=-3)


def _prompt_layer(x, W, l, tabs):
    B, S, _ = x.shape
    x2d = x.reshape(B * S, D_MODEL)
    *qkv, k, v = _qkv(x2d, W, l, tabs, TM_QKV, split=True)
    att = _attn_prompt(qkv, B, S)
    ob, tail_b = _lru_prompt(x, W, l, T_SEQ)
    oc = _sgu_prompt(x2d, W, l, T_SGU)
    od, tail_d, fin = _ssd_prompt(x, W, l, T_SEQ)
    x2d = _merge(x2d, att, ob.reshape(B * S, -1), oc, od.reshape(B * S, -1), W, l, TM_MERGE,
                 tuple(dil for _, dil in A_GROUPS))
    x2d = _moe_sorted(x2d, W, l)
    k3, v3 = k.reshape(B, S, -1), v.reshape(B, S, -1)
    kvs = [_kv_rows(k3[:, S - min(w, S):], v3[:, S - min(w, S):], gi) for gi, (w, _) in enumerate(A_GROUPS)]
    nb = CONV_W - 1
    states = (tail_b[:, SUBLANES - nb:], ob[:, -1], tail_d[:, SUBLANES - nb:],
              fin.reshape(B, SSD_HEADS, HEAD_DIM, SSD_STATE))
    return x2d.reshape(B, S, D_MODEL), kvs, states


def _decode_layer(x2d, W, l, tabs, caches, scb, hb, scd, ssm_all):
    Bd = x2d.shape[0]
    q, k, v = _qkv(x2d, W, l, tabs, Bd, split=False)
    att = _attn_step(q, k, v, caches, l)
    col0 = IN_OFFS[3]
    proj = _proj(x2d, W, l, col0, sum(IN_SIZES) - col0, 256)
    dtr = proj[:, IN_OFFS[7] - col0:IN_OFFS[7] - col0 + LANES]
    gates = proj[:, IN_OFFS[8] - col0:IN_OFFS[8] - col0 + N_BRANCH * D_MODEL]
    offs = (0, IN_OFFS[4] - col0, IN_OFFS[6] - col0)
    nb = CONV_W - 1
    ob, ncb, oc, vc, ncd, xs, xdt, ea, bc = _dec_branches(
        proj, dtr, offs, scb.reshape(Bd, nb * LRU_WIDTH), hb, scd.reshape(Bd, nb * SSD_CONV_CH), W, l)
    z = proj[:, IN_OFFS[5] - col0:IN_OFFS[5] - col0 + SSD_INNER]
    od, nssm = _dec_ssd(xs, xdt, ea, bc, z, ssm_all, W, l)
    x2d = _merge(x2d, att, ob, oc, od, W, l, Bd, (1,) * len(A_GROUPS), gates=gates)
    x2d = _moe(x2d, W, l, Bd)
    kvs = [_kv_rows(k, v, gi)[:, None] for gi in range(len(A_GROUPS))]
    states = (ncb.reshape(Bd, nb, LRU_WIDTH), ob, vc[:, None], ncd.reshape(Bd, nb, SSD_CONV_CH), nssm)
    return x2d, kvs, states


def kernel(x_prompt, x_sample, cache_kv_a1, cache_kv_a2, cache_kv_a3, state_conv_b, state_h_b, state_conv_d, state_ssm_d, norm1_g, w_in, q_norm_g, k_norm_g, conv_b_w, conv_b_b, lru_wa, lru_ba, lru_wx, lru_bx, lru_lambda, sgu_ln_g, sgu_ln_b, sgu_w, sgu_b, ssd_conv_w, ssd_conv_b, ssd_dt_bias, ssd_a_log, ssd_d, ssd_norm_g, w_pa, w_pb, w_pc, w_pd, w_o, norm2_g, router_group_w, router_group_b, router_exp_w, router_exp_b, moe_w_gate, moe_w_up, moe_w_down):
    p = dict(norm1_g=norm1_g, w_in=w_in, q_norm_g=q_norm_g, k_norm_g=k_norm_g, conv_b_w=conv_b_w,
             conv_b_b=conv_b_b, lru_wa=lru_wa, lru_ba=lru_ba, lru_wx=lru_wx, lru_bx=lru_bx,
             lru_lambda=lru_lambda, sgu_ln_g=sgu_ln_g, sgu_ln_b=sgu_ln_b, sgu_w=sgu_w, sgu_b=sgu_b,
             ssd_conv_w=ssd_conv_w, ssd_conv_b=ssd_conv_b, ssd_dt_bias=ssd_dt_bias, ssd_a_log=ssd_a_log,
             ssd_d=ssd_d, ssd_norm_g=ssd_norm_g, w_pa=w_pa, w_pb=w_pb, w_pc=w_pc, w_pd=w_pd, w_o=w_o,
             norm2_g=norm2_g, router_group_w=router_group_w, router_group_b=router_group_b,
             router_exp_w=router_exp_w, router_exp_b=router_exp_b, moe_w_gate=moe_w_gate,
             moe_w_up=moe_w_up, moe_w_down=moe_w_down)
    B, S, _ = x_prompt.shape
    Bd = x_sample.shape[0]
    depth = w_in.shape[0]
    small = _small_params(p)
    Wp = dict(small, **_matrix_params(p, BF16))
    Wd = dict(small, **_matrix_params(p, F32))
    tabs_p = _rope_tables(jnp.arange(S))
    tabs_s = tuple(jnp.broadcast_to(t, (Bd, LANES)) for t in _rope_tables(PAST_LEN + jnp.arange(1)))
    caches = (cache_kv_a1, cache_kv_a2, cache_kv_a3)
    yp, ys = x_prompt, x_sample.reshape(Bd, D_MODEL)
    P = [[] for _ in range(7)]
    Sx = [[] for _ in range(8)]
    for l in range(depth):
        yp, kvs, st = _prompt_layer(yp, Wp, l, tabs_p)
        for dst, val in zip(P, list(kvs) + list(st)):
            dst.append(val)
        ys, kvs, st = _decode_layer(ys, Wd, l, tabs_s, caches, state_conv_b[l], state_h_b[l],
                                    state_conv_d[l], state_ssm_d)
        for dst, val in zip(Sx, list(kvs) + list(st)):
            dst.append(val)
    st = jnp.stack
    return (yp, ys.reshape(Bd, 1, D_MODEL)) + tuple(st(t) for t in P) + tuple(st(t) for t in Sx)
```

```python
import functools

import jax
import jax.numpy as jnp
from jax import lax
from jax.experimental import pallas as pl
from jax.experimental.pallas import tpu as pltpu

F32 = jnp.float32
BF16 = jnp.bfloat16
HIGHEST = lax.Precision.HIGHEST

D_MODEL = 1024
DEPTH = 4
PAST_LEN = 8192
EPS = 1e-6
HEAD_DIM = 64
A_HPG = 4
A_GROUPS = ((128, 1), (512, 4), (2048, 16))
A_WIDTH = 768
A_OUT = 256
BLK = 128
ROPE_THETA = 10000.0
LRU_WIDTH = 768
LRU_C = 8.0
CONV_W = 4
SGU_WIDTH = 768
SGU_GROUPS = 4
SGU_CHUNK = 128
SSD_INNER = 768
SSD_HEADS = 12
SSD_GROUPS = 2
SSD_STATE = 128
SSD_CHUNK = 128
SSD_CONV_CH = 1280
N_BRANCH = 4
N_EXPERT_GROUPS = 4
EXPERTS_PER_GROUP = 4
N_EXPERTS = 16
D_EXPERT = 512
IN_SIZES = (768, 768, 768, 768, 1536, 768, 1280, 12, 4096)
IN_OFFS = tuple(sum(IN_SIZES[:i]) for i in range(len(IN_SIZES)))

LANES = 128
SUBLANES = 8
VMEM_LIMIT = 56 * 1024 * 1024

TM_QKV = 512
T_SEQ = 256
T_SGU = 512
TM_MERGE = 512
TM_MOE = 1024


def _cp(*sem):
    return pltpu.CompilerParams(dimension_semantics=sem, vmem_limit_bytes=VMEM_LIMIT)


def _const_spec(shape):
    nd = len(shape)
    return pl.BlockSpec(shape, lambda *_: (0,) * nd, pipeline_mode=pl.Buffered(1))


def _lspec(a, l, block=None, idx=None):
    shape = tuple(a.shape[1:]) if block is None else tuple(block)
    tail = (0,) * len(shape) if idx is None else tuple(idx)
    return pl.BlockSpec((None,) + shape, lambda *_: (l,) + tail, pipeline_mode=pl.Buffered(1))


def _rms(x, g):
    return x * lax.rsqrt(jnp.mean(x * x, axis=-1, keepdims=True) + EPS) * g


def _dot3(a, w):
    a1 = a.astype(BF16)
    a2 = (a - a1.astype(F32)).astype(BF16)
    w1 = w.astype(BF16)
    w2 = (w - w1.astype(F32)).astype(BF16)
    d = lambda x, y: jnp.dot(x, y, preferred_element_type=F32)
    return d(a1, w1) + (d(a1, w2) + d(a2, w1))


def _mm(a, w):
    if w.dtype == F32:
        return _dot3(a.astype(F32), w)
    return jnp.dot(a.astype(BF16), w, preferred_element_type=F32)


def _bdot(a, b):
    return jnp.dot(a.astype(BF16), b.astype(BF16), preferred_element_type=F32)


def _bdot_nt(a, b):
    return lax.dot_general(a.astype(BF16), b.astype(BF16), (((1,), (1,)), ((), ())),
                           preferred_element_type=F32)


def _bdot_tn(a, b):
    return lax.dot_general(a.astype(BF16), b.astype(BF16), (((0,), (0,)), ((), ())),
                           preferred_element_type=F32)


def _sigmoid(x):
    return jax.nn.sigmoid(x)


def _silu(x):
    return x * jax.nn.sigmoid(x)


def _neg_expm1(x):
    t = jnp.tanh(0.5 * x)
    return -2.0 * t / (1.0 - t)


def _lanes6(t):
    return jnp.concatenate([t] * 6, axis=1)


def _store_strided_view(t, out_ref, slab_ref, dil):
    if dil == 1:
        out_ref[...] = t.astype(out_ref.dtype)
        return
    T = t.shape[0]
    halves = A_OUT // LANES
    for s in range(halves):
        slab_ref[s] = t[:, s * LANES:(s + 1) * LANES]
    for r in range(dil):
        for s in range(halves):
            c0 = r * A_OUT + s * LANES
            out_ref[:, c0:c0 + LANES] = slab_ref[s, pl.ds(r, T // dil, stride=dil), :].astype(out_ref.dtype)


def _load_strided_view(view_ref, slab_ref, dil):
    if dil == 1:
        return view_ref[...]
    n = view_ref.shape[0]
    halves = A_OUT // LANES
    for r in range(dil):
        for s in range(halves):
            c0 = r * A_OUT + s * LANES
            slab_ref[s, pl.ds(r, n, stride=dil), :] = view_ref[:, c0:c0 + LANES]
    return jnp.concatenate([slab_ref[s] for s in range(halves)], axis=1)


def _qkv_body(x_ref, g1_ref, wq_ref, wk_ref, wv_ref, qg_ref, kg_ref, gbd_ref,
              cos_ref, sa_ref, sb_ref, *outs, split):
    h = _rms(x_ref[...], g1_ref[...]).astype(wq_ref.dtype)
    cos = _lanes6(cos_ref[...])
    sa = _lanes6(sa_ref[...])
    sb = _lanes6(sb_ref[...])
    gbd = gbd_ref[...]

    def normed_rotated(w_ref, hg_ref):
        t = _mm(h, w_ref[...])
        ss = _mm(t * t, gbd)
        tn = t * lax.rsqrt(ss * (1.0 / HEAD_DIM) + EPS) * hg_ref[...]
        return (tn * cos + pltpu.roll(tn, A_WIDTH - HEAD_DIM // 2, 1) * sa
                + pltpu.roll(tn, HEAD_DIM // 2, 1) * sb)

    q = normed_rotated(wq_ref, qg_ref) * (HEAD_DIM ** -0.5)
    k = normed_rotated(wk_ref, kg_ref)
    v = _mm(h, wv_ref[...])
    if split:
        *outs, slab_ref = outs
        for gi, (_, dil) in enumerate(A_GROUPS):
            sl = slice(gi * A_OUT, (gi + 1) * A_OUT)
            for j, t in enumerate((q, k, v)):
                _store_strided_view(t[:, sl], outs[3 * j + gi], slab_ref, dil)
        outs[9][...] = k
        outs[10][...] = v
    else:
        outs[0][...] = q
        outs[1][...] = k
        outs[2][...] = v


def _qkv(x2d, W, l, tabs, tm, split):
    R = x2d.shape[0]
    cos, sa, sb = tabs
    npos = cos.shape[0] // tm
    row = lambda i: (i, 0)
    pos = lambda i: (i % npos, 0)
    full = pl.BlockSpec((tm, A_WIDTH), row)
    scratch = []
    w_in = W["w_in"]
    wcol = lambda j: _lspec(w_in, l, (D_MODEL, A_WIDTH), (0, j))
    if split:
        dils = [dil for _, dil in A_GROUPS] * 3
        out_specs = [pl.BlockSpec((tm // d, d * A_OUT), row) for d in dils] + [full, full]
        out_shape = ([jax.ShapeDtypeStruct((R // d, d * A_OUT), BF16) for d in dils]
                     + [jax.ShapeDtypeStruct((R, A_WIDTH), F32)] * 2)
        scratch = [pltpu.VMEM((A_OUT // LANES, tm, LANES), F32)]
    else:
        out_specs = [full] * 3
        out_shape = [jax.ShapeDtypeStruct((R, A_WIDTH), F32)] * 3
    return pl.pallas_call(
        functools.partial(_qkv_body, split=split),
        grid=(R // tm,), scratch_shapes=scratch,
        in_specs=[pl.BlockSpec((tm, D_MODEL), row), _lspec(W["norm1_g"], l),
                  wcol(0), wcol(1), wcol(2),
                  _lspec(W["q_norm_g"], l), _lspec(W["k_norm_g"], l),
                  _const_spec((A_WIDTH, A_WIDTH)),
                  pl.BlockSpec((tm, LANES), pos), pl.BlockSpec((tm, LANES), pos),
                  pl.BlockSpec((tm, LANES), pos)],
        out_specs=out_specs, out_shape=out_shape,
        compiler_params=_cp("parallel"), name="qkv_proj",
    )(x2d, W["norm1_g"], w_in, w_in, w_in, W["q_norm_g"], W["k_norm_g"], W["gbd"], cos, sa, sb)


def _attn_group(q_ref, kp_ref, kc_ref, vp_ref, vc_ref, o_ref, l_ref, mask, lane_head):
    q = q_ref[...]
    zero = jnp.zeros_like(q)
    qs = jnp.concatenate([jnp.where(lane_head == hh, q, zero) for hh in range(A_HPG)], axis=0)
    kk = jnp.concatenate([kp_ref[...], kc_ref[...]], axis=0)
    vv = jnp.concatenate([vp_ref[...], vc_ref[...]], axis=0)
    s = jnp.where(mask, _bdot_nt(qs, kk), -jnp.inf)
    m = jnp.max(s, axis=-1, keepdims=True)
    e = jnp.exp(s - m)
    den = jnp.sum(e, axis=-1, keepdims=True)
    o4 = _bdot(e * (1.0 / den), vv)
    lse = m + jnp.log(den)
    o = o4[(A_HPG - 1) * BLK:]
    lo = jnp.broadcast_to(lse[(A_HPG - 1) * BLK:], (BLK, A_OUT))
    for hh in range(A_HPG - 2, -1, -1):
        sel = lane_head == hh
        o = jnp.where(sel, o4[hh * BLK:(hh + 1) * BLK], o)
        lo = jnp.where(sel, jnp.broadcast_to(lse[hh * BLK:(hh + 1) * BLK], (BLK, A_OUT)), lo)
    o_ref[...] = o
    l_ref[...] = lo


def _attn_prompt_body(*refs):
    i = pl.program_id(1)
    ins, outs = refs[:15], refs[15:]
    rows = A_HPG * BLK
    qi = lax.broadcasted_iota(jnp.int32, (rows, 2 * BLK), 0) % BLK
    kj = lax.broadcasted_iota(jnp.int32, (rows, 2 * BLK), 1)
    mask_cur = jnp.logical_and(kj >= BLK, kj - BLK <= qi)
    mask_prev = jnp.logical_and(kj < BLK, kj >= qi)
    lane_head = lax.broadcasted_iota(jnp.int32, (BLK, A_OUT), 1) // HEAD_DIM
    for gi, (_, dil) in enumerate(A_GROUPS):
        q_ref, kp_ref, kc_ref, vp_ref, vc_ref = ins[5 * gi:5 * gi + 5]
        o_ref, l_ref = outs[2 * gi:2 * gi + 2]
        mask = jnp.logical_or(mask_cur, jnp.logical_and(mask_prev, (i // dil) > 0))
        _attn_group(q_ref, kp_ref, kc_ref, vp_ref, vc_ref, o_ref, l_ref, mask, lane_head)


def _attn_prompt(qkv, B, S):
    nblk = S // BLK
    args, in_specs, out_specs, out_shape = [], [], [], []
    for gi, (_, dil) in enumerate(A_GROUPS):
        assert S % (dil * BLK) == 0
        rows = S // dil
        cur = lambda b, i, dil=dil: (b, i // dil, i % dil)
        prev = lambda b, i, dil=dil: (b, jnp.maximum(i // dil - 1, 0), i % dil)
        blk = (None, BLK, A_OUT)
        view = lambda t: t.reshape(B, rows, dil * A_OUT)
        qv, kv, vv = view(qkv[gi]), view(qkv[3 + gi]), view(qkv[6 + gi])
        args += [qv, kv, kv, vv, vv]
        in_specs += [pl.BlockSpec(blk, cur), pl.BlockSpec(blk, prev), pl.BlockSpec(blk, cur),
                     pl.BlockSpec(blk, prev), pl.BlockSpec(blk, cur)]
        out_specs += [pl.BlockSpec(blk, cur)] * 2
        out_shape += [jax.ShapeDtypeStruct((B, rows, dil * A_OUT), F32)] * 2
    res = pl.pallas_call(
        _attn_prompt_body, grid=(B, nblk), in_specs=in_specs, out_specs=out_specs,
        out_shape=out_shape, compiler_params=_cp("parallel", "parallel"), name="attn_prompt",
    )(*args)
    return [r.reshape(-1, r.shape[-1]) for r in res]


def _attn_step_body(q_ref, k_ref, v_ref, c1_ref, c2_ref, c3_ref, *outs):
    down = lambda r: jnp.broadcast_to(r[...], (LANES, A_WIDTH)).T
    q_cols, k_cols, v_cols = down(q_ref), down(k_ref), down(v_ref)
    lane2 = lax.broadcasted_iota(jnp.int32, (A_OUT, LANES), 1)
    for gi, (c_ref, (_, dil)) in enumerate(zip((c1_ref, c2_ref, c3_ref), A_GROUPS)):
        o_ref, l_ref = outs[2 * gi:2 * gi + 2]
        W = c_ref.shape[-1]
        rows = slice(gi * A_OUT, (gi + 1) * A_OUT)
        heads = lambda t: t[rows, 0:1].reshape(A_HPG, HEAD_DIM, 1)
        qh, kh, vh = heads(q_cols), heads(k_cols), heads(v_cols)
        pos = lax.broadcasted_iota(jnp.int32, (1, 1, W), 2)
        valid = (pos % dil) == 0
        s_c = jnp.where(valid, jnp.sum(c_ref[0] * qh, axis=1, keepdims=True), -jnp.inf)
        s_n = jnp.sum(qh * kh, axis=1, keepdims=True)
        m = jnp.maximum(jnp.max(s_c, axis=2, keepdims=True), s_n)
        e_c = jnp.exp(s_c - m)
        e_n = jnp.exp(s_n - m)
        den = jnp.sum(e_c, axis=2, keepdims=True) + e_n
        o = jnp.sum(c_ref[1] * (e_c / den), axis=2, keepdims=True) + (e_n / den) * vh
        lse = jnp.broadcast_to(m + jnp.log(den), (A_HPG, HEAD_DIM, 1))
        both = jnp.where(lane2 == 0, o.reshape(A_OUT, 1), jnp.where(lane2 == 1, lse.reshape(A_OUT, 1), 0.0))
        both_t = both.T
        o_ref[...] = both_t[0:1]
        l_ref[...] = both_t[1:2]


def _attn_step(q, k, v, caches, l):
    Bd = q.shape[0]
    row = lambda b: (b, 0, 0)
    args = [q.reshape(Bd, 1, A_WIDTH), k.reshape(Bd, 1, A_WIDTH), v.reshape(Bd, 1, A_WIDTH)]
    in_specs = [pl.BlockSpec((None, 1, A_WIDTH), row)] * 3
    for (window, dil), c in zip(A_GROUPS, caches):
        assert c.shape[2] == window and window % dil == 0
        args.append(jnp.transpose(c, (0, 1, 3, 4, 5, 2)))
        in_specs.append(pl.BlockSpec((None, None, 2, A_HPG, HEAD_DIM, window), lambda b: (l, b, 0, 0, 0, 0)))
    res = pl.pallas_call(
        _attn_step_body, grid=(Bd,), in_specs=in_specs,
        out_specs=[pl.BlockSpec((None, 1, A_OUT), row)] * 6,
        out_shape=[jax.ShapeDtypeStruct((Bd, 1, A_OUT), F32)] * 6,
        compiler_params=_cp("parallel"), name="attn_step",
    )(*args)
    return [r.reshape(Bd, A_OUT) for r in res]


def _lru_gates(xc, wa_ref, ba_ref, wx_ref, bx_ref, lam_ref):
    r = _sigmoid(_mm(xc, wa_ref[...]) + ba_ref[...])
    i = _sigmoid(_mm(xc, wx_ref[...]) + bx_ref[...])
    log_a = -LRU_C * r * jax.nn.softplus(-lam_ref[...])
    a = jnp.exp(log_a)
    b = jnp.sqrt(_neg_expm1(2.0 * log_a)) * (i * xc)
    return a, b


def _scan_rows(a, b, h0):
    T = a.shape[0]
    row = lax.broadcasted_iota(jnp.int32, a.shape, 0) % SUBLANES
    k = 1
    while k < SUBLANES:
        keep = row >= k
        a_s = jnp.where(keep, pltpu.roll(a, k, 0), 1.0)
        b_s = jnp.where(keep, pltpu.roll(b, k, 0), 0.0)
        b = a * b_s + b
        a = a * a_s
        k *= 2
    out, carry = [], h0
    for j in range(T // SUBLANES):
        rows = slice(j * SUBLANES, (j + 1) * SUBLANES)
        hj = b[rows] + a[rows] * carry
        out.append(hj)
        carry = hj[SUBLANES - 1:SUBLANES]
    return jnp.concatenate(out, axis=0)


def _lru_body(x_ref, g1_ref, w_ref, cw_ref, cb_ref, wa_ref, ba_ref, wx_ref, bx_ref, lam_ref,
              ob_ref, tail_ref, ext_ref, hc_ref, *, T):
    @pl.when(pl.program_id(1) == 0)
    def _():
        ext_ref[0:SUBLANES, :] = jnp.zeros((SUBLANES, LRU_WIDTH), F32)
        hc_ref[...] = jnp.zeros_like(hc_ref)

    h = _rms(x_ref[...], g1_ref[...])
    xb = _mm(h, w_ref[...])
    ext_ref[SUBLANES:SUBLANES + T, :] = xb
    xc = cb_ref[...]
    for kk in range(CONV_W - 1):
        xc = xc + ext_ref[pl.ds(SUBLANES - (CONV_W - 1) + kk, T), :] * cw_ref[kk:kk + 1, :]
    xc = xc + xb * cw_ref[CONV_W - 1:CONV_W, :]
    tail = xb[T - SUBLANES:T]
    ext_ref[0:SUBLANES, :] = tail
    tail_ref[...] = tail

    a, b = _lru_gates(xc, wa_ref, ba_ref, wx_ref, bx_ref, lam_ref)
    hfull = _scan_rows(a, b, hc_ref[0:1, :])
    ob_ref[...] = hfull
    hc_ref[...] = jnp.broadcast_to(hfull[T - 1:T], hc_ref.shape)


def _lru_prompt(x, W, l, T):
    B, S, _ = x.shape
    tile = lambda b, s: (b, s, 0)
    names = ("norm1_g", None, "conv_b_w", "conv_b_b", "lru_wa_bd", "lru_ba", "lru_wx_bd", "lru_bx", "lru_lambda")
    specs = [_lspec(W[n], l) if n else _lspec(W["w_in"], l, (D_MODEL, LRU_WIDTH), (0, IN_OFFS[3] // LRU_WIDTH))
             for n in names]
    return pl.pallas_call(
        functools.partial(_lru_body, T=T),
        grid=(B, S // T),
        in_specs=[pl.BlockSpec((None, T, D_MODEL), tile)] + specs,
        out_specs=[pl.BlockSpec((None, T, LRU_WIDTH), tile),
                   pl.BlockSpec((None, SUBLANES, LRU_WIDTH), lambda b, s: (b, 0, 0))],
        out_shape=[jax.ShapeDtypeStruct((B, S, LRU_WIDTH), F32),
                   jax.ShapeDtypeStruct((B, SUBLANES, LRU_WIDTH), F32)],
        scratch_shapes=[pltpu.VMEM((T + SUBLANES, LRU_WIDTH), F32),
                        pltpu.VMEM((SUBLANES, LRU_WIDTH), F32)],
        compiler_params=_cp("parallel", "arbitrary"), name="lru_prompt",
    )(x, *[W[n] if n else W["w_in"] for n in names])


def _gelu_ln(uv, lg_ref, lb_ref):
    uv = jax.nn.gelu(uv)
    u = uv[:, :SGU_WIDTH]
    v = uv[:, SGU_WIDTH:]
    mu = jnp.mean(v, axis=-1, keepdims=True)
    var = jnp.mean(jnp.square(v - mu), axis=-1, keepdims=True)
    v = (v - mu) * lax.rsqrt(var + EPS) * lg_ref[...] + lb_ref[...]
    return u, v


def _sgu_body(x_ref, g1_ref, w_ref, lg_ref, lb_ref, ws_ref, bs_ref, oc_ref, *, T):
    h = _rms(x_ref[...], g1_ref[...])
    u, v = _gelu_ln(_mm(h, w_ref[...]), lg_ref, lb_ref)
    qi = lax.broadcasted_iota(jnp.int32, (SGU_CHUNK, SGU_CHUNK), 0)
    kj = lax.broadcasted_iota(jnp.int32, (SGU_CHUNK, SGU_CHUNK), 1)
    tril = (kj <= qi).astype(F32)
    lane = lax.broadcasted_iota(jnp.int32, (SGU_CHUNK, SGU_WIDTH), 1)
    gw = SGU_WIDTH // SGU_GROUPS
    wms = [(ws_ref[g] * tril).astype(BF16) for g in range(SGU_GROUPS)]
    for c in range(T // SGU_CHUNK):
        rows = slice(c * SGU_CHUNK, (c + 1) * SGU_CHUNK)
        vc = v[rows].astype(BF16)
        mixed = jnp.dot(wms[SGU_GROUPS - 1], vc, preferred_element_type=F32)
        for g in range(SGU_GROUPS - 2, -1, -1):
            mixed = jnp.where(lane < (g + 1) * gw, jnp.dot(wms[g], vc, preferred_element_type=F32), mixed)
        oc_ref[rows, :] = u[rows] * (mixed + bs_ref[...])


def _sgu_prompt(x2d, W, l, T):
    R = x2d.shape[0]
    row = lambda i: (i, 0)
    return pl.pallas_call(
        functools.partial(_sgu_body, T=T),
        grid=(R // T,),
        in_specs=[pl.BlockSpec((T, D_MODEL), row), _lspec(W["norm1_g"], l),
                  _lspec(W["w_in"], l, (D_MODEL, 2 * SGU_WIDTH), (0, IN_OFFS[4] // (2 * SGU_WIDTH))),
                  _lspec(W["sgu_ln_g"], l), _lspec(W["sgu_ln_b"], l), _lspec(W["sgu_w"], l),
                  _lspec(W["sgu_b_tab"], l)],
        out_specs=pl.BlockSpec((T, SGU_WIDTH), row),
        out_shape=jax.ShapeDtypeStruct((R, SGU_WIDTH), F32),
        compiler_params=_cp("parallel"), name="sgu_prompt",
    )(x2d, W["norm1_g"], W["w_in"], W["sgu_ln_g"], W["sgu_ln_b"], W["sgu_w"], W["sgu_b_tab"])


def _ssd_dt_a(dtr, dtb_ref, alog_ref):
    lane = lax.broadcasted_iota(jnp.int32, (1, LANES), 1)
    dt = jax.nn.softplus(dtr + dtb_ref[...])
    A = jnp.where(lane < SSD_HEADS, -jnp.exp(alog_ref[...]), 0.0)
    return dt, A * dt


def _ssd_gate_norm(y, xs, z, dsk_ref, ng_ref):
    y = y + dsk_ref[...] * xs
    y = y * _silu(z)
    gw = SSD_INNER // SSD_GROUPS
    parts = []
    for g in range(SSD_GROUPS):
        yg = y[:, g * gw:(g + 1) * gw]
        parts.append(yg * lax.rsqrt(jnp.mean(yg * yg, axis=-1, keepdims=True) + EPS))
    return jnp.concatenate(parts, axis=1) * ng_ref[...]


def _ssd_chunk(xs, dt, a, Bm, Cm, st_ref, y_ref, row0):
    Q = SSD_CHUNK
    qi = lax.broadcasted_iota(jnp.int32, (Q, Q), 0)
    kj = lax.broadcasted_iota(jnp.int32, (Q, Q), 1)
    tril = kj <= qi
    cs = jnp.dot(tril.astype(F32), a, preferred_element_type=F32, precision=HIGHEST)
    cs_t = cs.T
    ecs = jnp.exp(cs)
    cs_last = cs[Q - 1:Q, :]
    to_end = jnp.exp(cs_last - cs)
    e_last = jnp.exp(cs_last)
    hpg = SSD_HEADS // SSD_GROUPS
    first = lax.broadcasted_iota(jnp.int32, (Q, LANES), 1) < HEAD_DIM
    first_row = lax.broadcasted_iota(jnp.int32, (LANES, 1), 0) < HEAD_DIM
    for g in range(SSD_GROUPS):
        Cg = Cm[:, g * SSD_STATE:(g + 1) * SSD_STATE].astype(BF16)
        Bg = Bm[:, g * SSD_STATE:(g + 1) * SSD_STATE].astype(BF16)
        G = _bdot_nt(Cg, Bg)
        for pp in range(hpg // 2):
            h0 = g * hpg + 2 * pp
            h1 = h0 + 1
            pl2 = slice(h0 * HEAD_DIM, (h0 + 2) * HEAD_DIM)
            per_head = lambda t: jnp.where(first, t[:, h0:h0 + 1], t[:, h1:h1 + 1])
            X2 = xs[:, pl2] * per_head(dt)
            L0 = jnp.exp(jnp.where(tril, cs[:, h0:h0 + 1] - cs_t[h0:h0 + 1, :], -jnp.inf))
            L1 = jnp.exp(jnp.where(tril, cs[:, h1:h1 + 1] - cs_t[h1:h1 + 1, :], -jnp.inf))
            Sp = st_ref[pl2, :]
            y = (_bdot(G * L0, jnp.where(first, X2, 0.0)) + _bdot(G * L1, jnp.where(first, 0.0, X2))
                 + per_head(ecs) * _bdot_nt(Cg, Sp))
            y_ref[row0:row0 + Q, pl2] = y
            keep = jnp.where(first_row, e_last[:, h0:h0 + 1], e_last[:, h1:h1 + 1])
            st_ref[pl2, :] = keep * Sp + _bdot_tn(X2 * per_head(to_end), Bg)


def _ssd_body(x_ref, g1_ref, wz_ref, wxbc_ref, wdt_ref, cw_ref, cb_ref, dtb_ref, alog_ref, dsk_ref,
              ng_ref, od_ref, tail_ref, fin_ref, ext_ref, st_ref, y_ref, *, T):
    @pl.when(pl.program_id(1) == 0)
    def _():
        ext_ref[0:SUBLANES, :] = jnp.zeros((SUBLANES, SSD_CONV_CH), F32)
        st_ref[...] = jnp.zeros_like(st_ref)

    h = _rms(x_ref[...], g1_ref[...]).astype(BF16)
    z = _mm(h, wz_ref[...])
    xbc = _mm(h, wxbc_ref[...])
    dtr = _mm(h, wdt_ref[...])
    ext_ref[SUBLANES:SUBLANES + T, :] = xbc
    xc = cb_ref[...]
    for kk in range(CONV_W - 1):
        xc = xc + ext_ref[pl.ds(SUBLANES - (CONV_W - 1) + kk, T), :] * cw_ref[kk:kk + 1, :]
    xc = xc + xbc * cw_ref[CONV_W - 1:CONV_W, :]
    tail = xbc[T - SUBLANES:T]
    ext_ref[0:SUBLANES, :] = tail
    tail_ref[...] = tail

    xc = _silu(xc)
    xs = xc[:, :SSD_INNER]
    gn = SSD_GROUPS * SSD_STATE
    Bm = xc[:, SSD_INNER:SSD_INNER + gn]
    Cm = xc[:, SSD_INNER + gn:]
    dt, a = _ssd_dt_a(dtr, dtb_ref, alog_ref)
    for c in range(T // SSD_CHUNK):
        rows = slice(c * SSD_CHUNK, (c + 1) * SSD_CHUNK)
        _ssd_chunk(xs[rows], dt[rows], a[rows], Bm[rows], Cm[rows], st_ref, y_ref, c * SSD_CHUNK)
    od_ref[...] = _ssd_gate_norm(y_ref[...], xs, z, dsk_ref, ng_ref)
    fin_ref[...] = st_ref[...]


def _ssd_prompt(x, W, l, T):
    B, S, _ = x.shape
    tile = lambda b, s: (b, s, 0)
    perb = lambda b, s: (b, 0, 0)
    wx = W["w_xbcdt"]
    return pl.pallas_call(
        functools.partial(_ssd_body, T=T),
        grid=(B, S // T),
        in_specs=[pl.BlockSpec((None, T, D_MODEL), tile), _lspec(W["norm1_g"], l),
                  _lspec(W["w_in"], l, (D_MODEL, SSD_INNER), (0, IN_OFFS[5] // SSD_INNER)),
                  _lspec(wx, l, (D_MODEL, SSD_CONV_CH), (0, 0)),
                  _lspec(wx, l, (D_MODEL, LANES), (0, SSD_CONV_CH // LANES)),
                  _lspec(W["ssd_conv_w"], l), _lspec(W["ssd_conv_b"], l), _lspec(W["ssd_dt_bias"], l),
                  _lspec(W["ssd_a_log"], l), _lspec(W["ssd_d_tab"], l), _lspec(W["ssd_norm_g"], l)],
        out_specs=[pl.BlockSpec((None, T, SSD_INNER), tile),
                   pl.BlockSpec((None, SUBLANES, SSD_CONV_CH), perb),
                   pl.BlockSpec((None, SSD_INNER, SSD_STATE), perb)],
        out_shape=[jax.ShapeDtypeStruct((B, S, SSD_INNER), F32),
                   jax.ShapeDtypeStruct((B, SUBLANES, SSD_CONV_CH), F32),
                   jax.ShapeDtypeStruct((B, SSD_INNER, SSD_STATE), F32)],
        scratch_shapes=[pltpu.VMEM((T + SUBLANES, SSD_CONV_CH), F32),
                        pltpu.VMEM((SSD_INNER, SSD_STATE), F32),
                        pltpu.VMEM((T, SSD_INNER), F32)],
        compiler_params=_cp("parallel", "arbitrary"), name="ssd_prompt",
    )(x, W["norm1_g"], W["w_in"], wx, wx, W["ssd_conv_w"], W["ssd_conv_b"],
      W["ssd_dt_bias"], W["ssd_a_log"], W["ssd_d_tab"], W["ssd_norm_g"])


def _proj_body(x_ref, g1_ref, w_ref, o_ref):
    o_ref[...] = _mm(_rms(x_ref[...], g1_ref[...]), w_ref[...])


def _proj(x2d, W, l, col0, ncols, tn):
    R = x2d.shape[0]
    assert col0 % tn == 0
    j0 = col0 // tn
    nblk = pl.cdiv(ncols, tn)
    return pl.pallas_call(
        _proj_body, grid=(nblk,),
        in_specs=[_const_spec((R, D_MODEL)), _lspec(W["norm1_g"], l),
                  pl.BlockSpec((None, D_MODEL, tn), lambda j: (l, 0, j0 + j))],
        out_specs=pl.BlockSpec((R, tn), lambda j: (0, j)),
        out_shape=jax.ShapeDtypeStruct((R, nblk * tn), F32),
        compiler_params=_cp("parallel"), name="proj_rest",
    )(x2d, W["norm1_g"], W["w_in"])


def _conv_step(xnew, st_ref, cw_ref, cb_ref, C):
    out = cb_ref[...]
    for kk in range(CONV_W - 1):
        out = out + st_ref[:, kk * C:(kk + 1) * C] * cw_ref[kk:kk + 1, :]
    return out + xnew * cw_ref[CONV_W - 1:CONV_W, :]


def _dec_branches_body(proj_ref, dtr_ref, scb_ref, hb_ref, scd_ref,
                       cwb_ref, cbb_ref, wa_ref, ba_ref, wx_ref, bx_ref, lam_ref,
                       lg_ref, lb_ref, w00_ref, b0_ref,
                       cwd_ref, cbd_ref, dtb_ref, alog_ref,
                       ob_ref, ncb_ref, oc_ref, vc_ref, ncd_ref, xs_ref, xdt_ref, ea_ref, bc_ref,
                       *, offs):
    o_xb, o_uv, o_xbc = offs
    xb = proj_ref[:, o_xb:o_xb + LRU_WIDTH]
    xc = _conv_step(xb, scb_ref, cwb_ref, cbb_ref, LRU_WIDTH)
    a, b = _lru_gates(xc, wa_ref, ba_ref, wx_ref, bx_ref, lam_ref)
    ob_ref[...] = a * hb_ref[...] + b
    ncb_ref[:, 0:2 * LRU_WIDTH] = scb_ref[:, LRU_WIDTH:3 * LRU_WIDTH]
    ncb_ref[:, 2 * LRU_WIDTH:3 * LRU_WIDTH] = xb
    u, v = _gelu_ln(proj_ref[:, o_uv:o_uv + 2 * SGU_WIDTH], lg_ref, lb_ref)
    vc_ref[...] = v
    oc_ref[...] = u * (w00_ref[...] * v + b0_ref[...])
    xbc = proj_ref[:, o_xbc:o_xbc + SSD_CONV_CH]
    xcd = _silu(_conv_step(xbc, scd_ref, cwd_ref, cbd_ref, SSD_CONV_CH))
    ncd_ref[:, 0:2 * SSD_CONV_CH] = scd_ref[:, SSD_CONV_CH:3 * SSD_CONV_CH]
    ncd_ref[:, 2 * SSD_CONV_CH:3 * SSD_CONV_CH] = xbc
    xs = xcd[:, :SSD_INNER]
    dt, a_dt = _ssd_dt_a(dtr_ref[...], dtb_ref, alog_ref)
    hrow = lax.broadcasted_iota(jnp.int32, (LANES, SSD_INNER), 0)
    hlane = lax.broadcasted_iota(jnp.int32, (LANES, SSD_INNER), 1)
    expand = (hlane // HEAD_DIM == hrow).astype(F32)
    dt_e = jnp.dot(dt, expand, preferred_element_type=F32, precision=HIGHEST)
    a_e = jnp.dot(a_dt, expand, preferred_element_type=F32, precision=HIGHEST)
    xs_ref[...] = xs
    xdt_ref[...] = xs * dt_e
    ea_ref[...] = jnp.exp(a_e)
    bc_ref[...] = xcd[:, SSD_INNER:]


def _dec_branches(proj, dtr, offs, scb, hb, scd, W, l):
    Bd = proj.shape[0]
    f = lambda n: jax.ShapeDtypeStruct((Bd, n), F32)
    whole = lambda a: pl.BlockSpec(a.shape, lambda i: (0,) * a.ndim)
    names = ("conv_b_w", "conv_b_b", "lru_wa_bd", "lru_ba", "lru_wx_bd", "lru_bx", "lru_lambda",
             "sgu_ln_g", "sgu_ln_b", "sgu_w00_tab", "sgu_b0_tab",
             "ssd_conv_w", "ssd_conv_b", "ssd_dt_bias", "ssd_a_log")
    acts = (proj, dtr, scb, hb, scd)
    widths = (LRU_WIDTH, 3 * LRU_WIDTH, SGU_WIDTH, SGU_WIDTH, 3 * SSD_CONV_CH, SSD_INNER, SSD_INNER,
              SSD_INNER, 2 * SSD_GROUPS * SSD_STATE)
    return pl.pallas_call(
        functools.partial(_dec_branches_body, offs=offs),
        grid=(1,),
        in_specs=[whole(a) for a in acts] + [_lspec(W[n], l) for n in names],
        out_specs=[pl.BlockSpec((Bd, n), lambda i: (0, 0)) for n in widths],
        out_shape=[f(n) for n in widths],
        compiler_params=_cp("arbitrary"), name="dec_branches",
    )(*acts, *[W[n] for n in names])


def _dec_ssd_body(xs_ref, xdt_ref, ea_ref, bc_ref, z_ref, st_ref, dsk_ref, ng_ref, od_ref, ns_ref):
    gn = SSD_GROUPS * SSD_STATE
    half = SSD_INNER // SSD_GROUPS
    ridx = lax.broadcasted_iota(jnp.int32, (LANES, SSD_INNER), 0)
    rows = jnp.where(ridx == 0, jnp.broadcast_to(xdt_ref[...], (LANES, SSD_INNER)),
                     jnp.where(ridx == 1, jnp.broadcast_to(ea_ref[...], (LANES, SSD_INNER)), 0.0))
    cols = rows.T
    xdt_c = cols[:, 0:1]
    ea_c = cols[:, 1:2]
    bc = bc_ref[...]
    Bm, Cm = bc[:, :gn], bc[:, gn:]
    rowi = lax.broadcasted_iota(jnp.int32, (SSD_INNER, SSD_STATE), 0)
    b_full = jnp.where(rowi < half, jnp.broadcast_to(Bm[:, :SSD_STATE], (SSD_INNER, SSD_STATE)),
                       jnp.broadcast_to(Bm[:, SSD_STATE:], (SSD_INNER, SSD_STATE)))
    new = ea_c * st_ref[...] + xdt_c * b_full
    ns_ref[...] = new
    cidx = lax.broadcasted_iota(jnp.int32, (SUBLANES, SSD_STATE), 0)
    c8 = jnp.where(cidx == 0, jnp.broadcast_to(Cm[:, :SSD_STATE], (SUBLANES, SSD_STATE)),
                   jnp.where(cidx == 1, jnp.broadcast_to(Cm[:, SSD_STATE:], (SUBLANES, SSD_STATE)), 0.0))
    y8 = lax.dot_general(c8, new, (((1,), (1,)), ((), ())), preferred_element_type=F32,
                         precision=HIGHEST)
    lane = lax.broadcasted_iota(jnp.int32, (1, SSD_INNER), 1)
    y = jnp.where(lane < half, y8[0:1], y8[1:2])
    od_ref[...] = _ssd_gate_norm(y, xs_ref[...], z_ref[...], dsk_ref, ng_ref)


def _dec_ssd(xs, xdt, ea, bc, z, state_all, W, l):
    Bd = xs.shape[0]
    row = lambda b: (b, 0, 0)
    r3 = lambda t: t.reshape(Bd, 1, t.shape[-1])
    st = state_all.reshape(state_all.shape[0], Bd, SSD_INNER, SSD_STATE)
    od, ns = pl.pallas_call(
        _dec_ssd_body, grid=(Bd,),
        in_specs=[pl.BlockSpec((None, 1, SSD_INNER), row)] * 3
                 + [pl.BlockSpec((None, 1, 2 * SSD_GROUPS * SSD_STATE), row),
                    pl.BlockSpec((None, 1, SSD_INNER), row),
                    pl.BlockSpec((None, None, SSD_INNER, SSD_STATE), lambda b: (l, b, 0, 0)),
                    _lspec(W["ssd_d_tab"], l), _lspec(W["ssd_norm_g"], l)],
        out_specs=[pl.BlockSpec((None, 1, SSD_INNER), row),
                   pl.BlockSpec((None, SSD_INNER, SSD_STATE), row)],
        out_shape=[jax.ShapeDtypeStruct((Bd, 1, SSD_INNER), F32),
                   jax.ShapeDtypeStruct((Bd, SSD_INNER, SSD_STATE), F32)],
        compiler_params=_cp("parallel"), name="dec_ssd",
    )(r3(xs), r3(xdt), r3(ea), r3(bc), r3(z), st, W["ssd_d_tab"], W["ssd_norm_g"])
    return od.reshape(Bd, SSD_INNER), ns.reshape(state_all.shape[1:])


def _merge_body(x_ref, o1_ref, l1_ref, o2_ref, l2_ref, o3_ref, l3_ref, ob_ref, oc_ref, od_ref,
                g1_ref, wg_ref, wpa_ref, wpb_ref, wpc_ref, wpd_ref, wo_ref, out_ref, *slab, dils, pre_gates):
    x = x_ref[...]
    h = None if pre_gates else _rms(x, g1_ref[...]).astype(wg_ref.dtype)
    tok = lambda ref, d: _load_strided_view(ref, slab[0], d) if d > 1 else ref[...]
    o1, o2, o3 = (tok(r, d) for r, d in zip((o1_ref, o2_ref, o3_ref), dils))
    l1, l2, l3 = (tok(r, d) for r, d in zip((l1_ref, l2_ref, l3_ref), dils))
    m = jnp.maximum(jnp.maximum(l1, l2), l3)
    e1, e2, e3 = jnp.exp(l1 - m), jnp.exp(l2 - m), jnp.exp(l3 - m)
    den = e1 + e2 + e3
    oa = (e1 / den) * o1 + (e2 / den) * o2 + (e3 / den) * o3
    merged = None
    for bi, (o, w_ref) in enumerate(((oa, wpa_ref), (ob_ref[...], wpb_ref),
                                     (oc_ref[...], wpc_ref), (od_ref[...], wpd_ref))):
        cols = slice(bi * D_MODEL, (bi + 1) * D_MODEL)
        gate = _sigmoid(wg_ref[:, cols] if pre_gates else _mm(h, wg_ref[:, cols]))
        term = gate * _mm(o, w_ref[...])
        merged = term if merged is None else merged + term
    out_ref[...] = x + _mm(merged, wo_ref[...])


def _merge(x2d, att, ob, oc, od, W, l, tm, dils, gates=None):
    R = x2d.shape[0]
    row = lambda i: (i, 0)
    names = ("norm1_g", "w_gates", "w_pa", "w_pb", "w_pc", "w_pd", "w_o")
    params = [W[n] for n in names]
    pspecs = [_lspec(a, l) for a in params]
    if gates is not None:
        params[1], pspecs[1] = gates, pl.BlockSpec((tm, N_BRANCH * D_MODEL), row)
    att_specs = [pl.BlockSpec((tm // d, d * A_OUT), row) for d in dils for _ in range(2)]
    scratch = [pltpu.VMEM((A_OUT // LANES, tm, LANES), F32)] if max(dils) > 1 else []
    return pl.pallas_call(
        functools.partial(_merge_body, dils=dils, pre_gates=gates is not None), grid=(R // tm,),
        in_specs=[pl.BlockSpec((tm, D_MODEL), row)] + att_specs
                 + [pl.BlockSpec((tm, LRU_WIDTH), row)] * 3 + pspecs,
        out_specs=pl.BlockSpec((tm, D_MODEL), row),
        out_shape=jax.ShapeDtypeStruct((R, D_MODEL), F32), scratch_shapes=scratch,
        compiler_params=_cp("parallel"), name="merge",
    )(x2d, *att, ob, oc, od, *params)


def _route(logits):
    lane = lax.broadcasted_iota(jnp.int32, logits.shape, 1)
    big = jnp.int32(LANES)
    ninf = -jnp.inf
    gl = jnp.where(lane < N_EXPERT_GROUPS, logits, ninf)
    gm = jnp.max(gl, axis=-1, keepdims=True)
    gsel = jnp.min(jnp.where(gl == gm, lane, big), axis=-1, keepdims=True)
    pg = 1.0 / jnp.sum(jnp.exp(gl - gm), axis=-1, keepdims=True)
    lo = N_EXPERT_GROUPS + EXPERTS_PER_GROUP * gsel
    el = jnp.where(jnp.logical_and(lane >= lo, lane < lo + EXPERTS_PER_GROUP), logits, ninf)
    t1 = jnp.max(el, axis=-1, keepdims=True)
    i1 = jnp.min(jnp.where(el == t1, lane, big), axis=-1, keepdims=True)
    el2 = jnp.where(lane == i1, ninf, el)
    t2 = jnp.max(el2, axis=-1, keepdims=True)
    i2 = jnp.min(jnp.where(el2 == t2, lane, big), axis=-1, keepdims=True)
    e2 = jnp.exp(t2 - t1)
    den = 1.0 + e2
    w1 = (1.0 / den) * pg
    w2 = (e2 / den) * pg
    return jnp.where(lane == i1, w1, 0.0) + jnp.where(lane == i2, w2, 0.0), gsel


def _moe_body(x_ref, g2_ref, wr_ref, br_ref, wg_ref, wu_ref, wd_ref, out_ref, h_ref, comb_ref, acc_ref):
    e = pl.program_id(1)

    @pl.when(e == 0)
    def _():
        h = _rms(x_ref[...], g2_ref[...])
        h_ref[...] = h.astype(h_ref.dtype)
        logits = jnp.dot(h, wr_ref[...], preferred_element_type=F32, precision=HIGHEST) + br_ref[...]
        comb_ref[...] = _route(logits)[0]
        acc_ref[...] = jnp.zeros_like(acc_ref)

    h = h_ref[...]
    lane = lax.broadcasted_iota(jnp.int32, comb_ref.shape, 1)
    c = jnp.sum(jnp.where(lane == N_EXPERT_GROUPS + e, comb_ref[...], 0.0), axis=-1, keepdims=True)
    hg = _mm(h, wg_ref[...])
    hu = _mm(h, wu_ref[...])
    act = _silu(hg) * hu * c
    acc_ref[...] += _mm(act, wd_ref[...])

    @pl.when(e == N_EXPERTS - 1)
    def _():
        out_ref[...] = x_ref[...] + acc_ref[...]


def _moe(x2d, W, l, tm):
    R = x2d.shape[0]
    row = lambda i, e: (i, 0)
    expert = lambda r, c: pl.BlockSpec((None, None, r, c), lambda i, e: (l, e, 0, 0))
    return pl.pallas_call(
        _moe_body, grid=(R // tm, N_EXPERTS),
        in_specs=[pl.BlockSpec((tm, D_MODEL), row), _lspec(W["norm2_g"], l), _lspec(W["w_router"], l),
                  _lspec(W["b_router"], l),
                  expert(D_MODEL, D_EXPERT), expert(D_MODEL, D_EXPERT), expert(D_EXPERT, D_MODEL)],
        out_specs=pl.BlockSpec((tm, D_MODEL), row),
        out_shape=jax.ShapeDtypeStruct((R, D_MODEL), F32),
        scratch_shapes=[pltpu.VMEM((tm, D_MODEL), W["moe_w_gate"].dtype), pltpu.VMEM((tm, LANES), F32),
                        pltpu.VMEM((tm, D_MODEL), F32)],
        compiler_params=_cp("parallel", "arbitrary"), name="moe",
    )(x2d, W["norm2_g"], W["w_router"], W["b_router"], W["moe_w_gate"], W["moe_w_up"],
      W["moe_w_down"])


GID_LANE = N_EXPERT_GROUPS + N_EXPERTS
SEG_ALIGN = 16
MOE_RB = TM_MOE // N_EXPERT_GROUPS + 32
PERM_BLK = 128
UNPERM_BLK = 256
MOE_USED = -(-(TM_MOE + N_EXPERT_GROUPS * SEG_ALIGN) // PERM_BLK) * PERM_BLK
MOE_TMP = MOE_USED + MOE_RB
assert MOE_RB % SEG_ALIGN == 0 and TM_MOE % UNPERM_BLK == 0


def _route_body(x_ref, g2_ref, wr_ref, br_ref, comb_ref, cnt_ref):
    h = _rms(x_ref[...], g2_ref[...])
    logits = _dot3(h, wr_ref[...]) + br_ref[...]
    comb, gsel = _route(logits)
    lane = lax.broadcasted_iota(jnp.int32, comb.shape, 1)
    comb_ref[...] = jnp.where(lane == GID_LANE, gsel.astype(F32), comb)
    counts = jnp.sum((lane == gsel).astype(F32), axis=0, keepdims=True)
    cnt_ref[...] = jnp.broadcast_to(counts, cnt_ref.shape)


def _route_call(x2d, W, l, tm):
    R = x2d.shape[0]
    return pl.pallas_call(
        _route_body, grid=(R // tm,),
        in_specs=[pl.BlockSpec((tm, D_MODEL), lambda i: (i, 0)), _lspec(W["norm2_g"], l),
                  _lspec(W["w_router"], l), _lspec(W["b_router"], l)],
        out_specs=[pl.BlockSpec((tm, LANES), lambda i: (i, 0)),
                   pl.BlockSpec((None, SUBLANES, LANES), lambda i: (i, 0, 0))],
        out_shape=[jax.ShapeDtypeStruct((R, LANES), F32),
                   jax.ShapeDtypeStruct((R // tm, SUBLANES, LANES), F32)],
        compiler_params=_cp("parallel"), name="moe_route",
    )(x2d, W["norm2_g"], W["w_router"], W["b_router"])


def _snake_group(i, s):
    return jnp.where(i % 2 == 0, s, N_EXPERT_GROUPS - 1 - s)


def _moe_sorted_body(off_ref, nch_ref, x_ref, comb_ref, g2_ref, wg_ref, wu_ref, wd_ref, out_ref,
                     sh_ref, sc_ref, acc_ref, rcol_ref, lo_ref, *, TM, RB):
    i = pl.program_id(0)
    s = pl.program_id(1)
    g = _snake_group(i, s)
    TMP = sh_ref.shape[0]

    @pl.when(s == 0)
    def _():
        h2 = _rms(x_ref[...], g2_ref[...]).astype(BF16)
        comb = comb_ref[...]
        lane = lax.broadcasted_iota(jnp.int32, comb.shape, 1)
        gid = jnp.sum(jnp.where(lane == GID_LANE, comb, 0.0), axis=-1, keepdims=True)
        onehot = jnp.logical_and(lane < N_EXPERT_GROUPS, lane.astype(F32) == gid).astype(F32)
        trow = lax.broadcasted_iota(jnp.int32, onehot.shape, 0)
        run = onehot
        k = 1
        while k < TM:
            run = run + jnp.where(trow >= k, pltpu.roll(run, k, 0), 0.0)
            k *= 2
        prefix = run - onehot
        lane1 = lax.broadcasted_iota(jnp.int32, (1, LANES), 1)
        offv = jnp.zeros((1, LANES), F32)
        for gg in range(N_EXPERT_GROUPS):
            offv = jnp.where(lane1 == gg, off_ref[i * N_EXPERT_GROUPS + gg].astype(F32), offv)
        rank = jnp.sum(onehot * (offv + prefix), axis=-1, keepdims=True)
        rank_b = jnp.broadcast_to(rank, (TM, LANES))
        rcol_ref[...] = rank_b
        rank_row = rank_b.T[0:1]
        c1 = comb.astype(BF16)
        c2 = (comb - c1.astype(F32)).astype(BF16)
        payload = jnp.concatenate([h2, c1, c2], axis=1)
        for blk in range(MOE_USED // PERM_BLK):
            rows = slice(blk * PERM_BLK, (blk + 1) * PERM_BLK)
            srow = lax.broadcasted_iota(jnp.int32, (PERM_BLK, TM), 0) + blk * PERM_BLK
            perm = (srow.astype(F32) == rank_row).astype(BF16)
            moved = jnp.dot(perm, payload, preferred_element_type=F32)
            sh_ref[rows, :] = moved[:, :D_MODEL].astype(BF16)
            sc_ref[rows, :] = moved[:, D_MODEL:D_MODEL + LANES] + moved[:, D_MODEL + LANES:]
        sh_ref[MOE_USED:, :] = jnp.zeros((TMP - MOE_USED, D_MODEL), BF16)
        sc_ref[MOE_USED:, :] = jnp.zeros((TMP - MOE_USED, LANES), F32)
        acc_ref[...] = jnp.zeros_like(acc_ref)

    off = off_ref[i * N_EXPERT_GROUPS + g]
    lane_c = lax.broadcasted_iota(jnp.int32, (RB, LANES), 1)

    def chunk(j, carry):
        r0 = pl.multiple_of(off + j * RB, SEG_ALIGN)
        rows = sh_ref[pl.ds(r0, RB), :]
        cc = sc_ref[pl.ds(r0, RB), :]
        y = jnp.zeros((RB, D_MODEL), F32)
        for e in range(EXPERTS_PER_GROUP):
            ce = jnp.sum(jnp.where(lane_c == N_EXPERT_GROUPS + EXPERTS_PER_GROUP * g + e, cc, 0.0),
                         axis=-1, keepdims=True)
            hg = jnp.dot(rows, wg_ref[e], preferred_element_type=F32)
            hu = jnp.dot(rows, wu_ref[e], preferred_element_type=F32)
            y = y + _bdot(_silu(hg) * hu * ce, wd_ref[e])
        acc_ref[pl.ds(r0, RB), :] += y
        return carry

    lax.fori_loop(0, nch_ref[i * N_EXPERT_GROUPS + g], chunk, 0)

    @pl.when(s == N_EXPERT_GROUPS - 1)
    def _():
        for blk in range(MOE_USED // PERM_BLK):
            rows = slice(blk * PERM_BLK, (blk + 1) * PERM_BLK)
            a = acc_ref[rows, :]
            hi = a.astype(BF16)
            sh_ref[rows, :] = hi
            lo_ref[rows, :] = (a - hi.astype(F32)).astype(BF16)
        scol = lax.broadcasted_iota(jnp.int32, (UNPERM_BLK, MOE_USED), 1).astype(F32)
        for blk in range(TM // UNPERM_BLK):
            rows = slice(blk * UNPERM_BLK, (blk + 1) * UNPERM_BLK)
            unperm = (scol == rcol_ref[rows, 0:1]).astype(BF16)
            out_ref[rows, :] = x_ref[rows, :] + (
                jnp.dot(unperm, sh_ref[0:MOE_USED, :], preferred_element_type=F32)
                + jnp.dot(unperm, lo_ref[0:MOE_USED, :], preferred_element_type=F32))


def _moe_sorted(x2d, W, l):
    R = x2d.shape[0]
    TM, RB, TMP = TM_MOE, MOE_RB, MOE_TMP
    nt = R // TM
    comb, counts = _route_call(x2d, W, l, TM)
    cnt = counts[:, 0, :N_EXPERT_GROUPS].astype(jnp.int32)
    padded = (cnt + SEG_ALIGN - 1) // SEG_ALIGN * SEG_ALIGN
    off = (jnp.cumsum(padded, axis=1) - padded).reshape(-1)
    nch = ((cnt + RB - 1) // RB).reshape(-1)
    row = lambda i, g, *_: (i, 0)
    const = lambda a: pl.BlockSpec((None,) + tuple(a.shape[1:]), lambda i, g, *_: (l,) + (0,) * (a.ndim - 1))
    grp = lambda r, c: pl.BlockSpec((None, None, EXPERTS_PER_GROUP, r, c),
                                    lambda i, s, *_: (l, _snake_group(i, s), 0, 0, 0))
    gview = lambda a: a.reshape(a.shape[0], N_EXPERT_GROUPS, EXPERTS_PER_GROUP, a.shape[2], a.shape[3])
    return pl.pallas_call(
        functools.partial(_moe_sorted_body, TM=TM, RB=RB),
        grid_spec=pltpu.PrefetchScalarGridSpec(
            num_scalar_prefetch=2, grid=(nt, N_EXPERT_GROUPS),
            in_specs=[pl.BlockSpec((TM, D_MODEL), row, pipeline_mode=pl.Buffered(1)),
                      pl.BlockSpec((TM, LANES), row), const(W["norm2_g"]),
                      grp(D_MODEL, D_EXPERT), grp(D_MODEL, D_EXPERT), grp(D_EXPERT, D_MODEL)],
            out_specs=pl.BlockSpec((TM, D_MODEL), row),
            scratch_shapes=[pltpu.VMEM((TMP, D_MODEL), BF16), pltpu.VMEM((TMP, LANES), F32),
                            pltpu.VMEM((TMP, D_MODEL), F32), pltpu.VMEM((TM, LANES), F32),
                            pltpu.VMEM((TMP, D_MODEL), BF16)]),
        out_shape=jax.ShapeDtypeStruct((R, D_MODEL), F32),
        compiler_params=_cp("parallel", "arbitrary"), name="moe_sorted",
    )(off, nch, x2d, comb, W["norm2_g"], gview(W["moe_w_gate"]), gview(W["moe_w_up"]), gview(W["moe_w_down"]))


def _rope_tables(pos):
    half = HEAD_DIM // 2
    inv = ROPE_THETA ** (-jnp.arange(half, dtype=F32) / half)
    ang = pos.astype(F32)[:, None] * inv[None]
    c, s = jnp.cos(ang), jnp.sin(ang)
    z = jnp.zeros_like(s)
    reps = LANES // HEAD_DIM
    return (jnp.tile(jnp.concatenate([c, c], 1), (1, reps)),
            jnp.tile(jnp.concatenate([-s, z], 1), (1, reps)),
            jnp.tile(jnp.concatenate([z, s], 1), (1, reps)))


def _block_diag(w):
    L, n, k, _ = w.shape
    eye = jnp.eye(n, dtype=w.dtype)
    return (eye[None, :, None, :, None] * w[:, :, :, None, :]).reshape(L, n * k, n * k)


def _small_params(p):
    vec = lambda t: t.reshape(t.shape[0], 1, -1)
    W = {}
    for n in ("norm1_g", "norm2_g", "conv_b_b", "lru_ba", "lru_bx", "lru_lambda", "sgu_ln_g", "sgu_ln_b",
              "ssd_conv_b", "ssd_norm_g"):
        W[n] = vec(p[n])
    reps = A_WIDTH // HEAD_DIM
    W["q_norm_g"] = vec(jnp.tile(p["q_norm_g"], (1, reps)))
    W["k_norm_g"] = vec(jnp.tile(p["k_norm_g"], (1, reps)))
    W["conv_b_w"] = p["conv_b_w"]
    W["ssd_conv_w"] = p["ssd_conv_w"]
    W["sgu_w"] = p["sgu_w"]
    gw = SGU_WIDTH // SGU_GROUPS
    W["sgu_b_tab"] = jnp.repeat(jnp.swapaxes(p["sgu_b"], 1, 2), gw, axis=2)
    W["sgu_w00_tab"] = vec(jnp.repeat(p["sgu_w"][:, :, 0, 0], gw, axis=1))
    W["sgu_b0_tab"] = vec(jnp.repeat(p["sgu_b"][:, :, 0], gw, axis=1))
    padh = lambda t: vec(jnp.pad(t, ((0, 0), (0, LANES - SSD_HEADS))))
    W["ssd_dt_bias"], W["ssd_a_log"] = padh(p["ssd_dt_bias"]), padh(p["ssd_a_log"])
    W["ssd_d_tab"] = vec(jnp.repeat(p["ssd_d"], HEAD_DIM, axis=1))
    we = jnp.transpose(p["router_exp_w"], (0, 2, 1, 3)).reshape(-1, D_MODEL, N_EXPERTS)
    wr = jnp.concatenate([p["router_group_w"], we], axis=2)
    W["w_router"] = jnp.pad(wr, ((0, 0), (0, 0), (0, LANES - wr.shape[2])))
    br = jnp.concatenate([p["router_group_b"], p["router_exp_b"].reshape(-1, N_EXPERTS)], axis=1)
    W["b_router"] = vec(jnp.pad(br, ((0, 0), (0, LANES - br.shape[1]))))
    return W


def _matrix_params(p, wdt):
    W = {}
    w_in = p["w_in"].astype(wdt)
    W["w_in"] = w_in
    o_xbc, o_dt, o_g = IN_OFFS[6], IN_OFFS[7], IN_OFFS[8]
    W["w_xbcdt"] = jnp.concatenate(
        [w_in[:, :, o_xbc:o_dt], jnp.pad(w_in[:, :, o_dt:o_g], ((0, 0), (0, 0), (0, LANES - SSD_HEADS)))], axis=2)
    W["w_gates"] = w_in[:, :, o_g:]
    reps = A_WIDTH // HEAD_DIM
    W["gbd"] = _block_diag(jnp.ones((1, reps, HEAD_DIM, HEAD_DIM), wdt))[0]
    W["lru_wa_bd"] = _block_diag(p["lru_wa"]).astype(wdt)
    W["lru_wx_bd"] = _block_diag(p["lru_wx"]).astype(wdt)
    for n in ("w_pa", "w_pb", "w_pc", "w_pd", "w_o", "moe_w_gate", "moe_w_up", "moe_w_down"):
        W[n] = p[n].astype(wdt)
    return W


def _kv_rows(k, v, gi):
    sl = slice(gi * A_OUT, (gi + 1) * A_OUT)
    shp = k.shape[:-1] + (A_HPG, HEAD_DIM)
    return jnp.stack([k[..., sl].reshape(shp), v[..., sl].reshape(shp)], axis=-3)


def _prompt_layer(x, W, l, tabs):
    B, S, _ = x.shape
    x2d = x.reshape(B * S, D_MODEL)
    *qkv, k, v = _qkv(x2d, W, l, tabs, TM_QKV, split=True)
    att = _attn_prompt(qkv, B, S)
    ob, tail_b = _lru_prompt(x, W, l, T_SEQ)
    oc = _sgu_prompt(x2d, W, l, T_SGU)
    od, tail_d, fin = _ssd_prompt(x, W, l, T_SEQ)
    x2d = _merge(x2d, att, ob.reshape(B * S, -1), oc, od.reshape(B * S, -1), W, l, TM_MERGE,
                 tuple(dil for _, dil in A_GROUPS))
    x2d = _moe_sorted(x2d, W, l)
    k3, v3 = k.reshape(B, S, -1), v.reshape(B, S, -1)
    kvs = [_kv_rows(k3[:, S - min(w, S):], v3[:, S - min(w, S):], gi) for gi, (w, _) in enumerate(A_GROUPS)]
    nb = CONV_W - 1
    states = (tail_b[:, SUBLANES - nb:], ob[:, -1], tail_d[:, SUBLANES - nb:],
              fin.reshape(B, SSD_HEADS, HEAD_DIM, SSD_STATE))
    return x2d.reshape(B, S, D_MODEL), kvs, states


def _decode_layer(x2d, W, l, tabs, caches, scb, hb, scd, ssm_all):
    Bd = x2d.shape[0]
    q, k, v = _qkv(x2d, W, l, tabs, Bd, split=False)
    att = _attn_step(q, k, v, caches, l)
    col0 = IN_OFFS[3]
    proj = _proj(x2d, W, l, col0, sum(IN_SIZES) - col0, 256)
    dtr = proj[:, IN_OFFS[7] - col0:IN_OFFS[7] - col0 + LANES]
    gates = proj[:, IN_OFFS[8] - col0:IN_OFFS[8] - col0 + N_BRANCH * D_MODEL]
    offs = (0, IN_OFFS[4] - col0, IN_OFFS[6] - col0)
    nb = CONV_W - 1
    ob, ncb, oc, vc, ncd, xs, xdt, ea, bc = _dec_branches(
        proj, dtr, offs, scb.reshape(Bd, nb * LRU_WIDTH), hb, scd.reshape(Bd, nb * SSD_CONV_CH), W, l)
    z = proj[:, IN_OFFS[5] - col0:IN_OFFS[5] - col0 + SSD_INNER]
    od, nssm = _dec_ssd(xs, xdt, ea, bc, z, ssm_all, W, l)
    x2d = _merge(x2d, att, ob, oc, od, W, l, Bd, (1,) * len(A_GROUPS), gates=gates)
    x2d = _moe(x2d, W, l, Bd)
    kvs = [_kv_rows(k, v, gi)[:, None] for gi in range(len(A_GROUPS))]
    states = (ncb.reshape(Bd, nb, LRU_WIDTH), ob, vc[:, None], ncd.reshape(Bd, nb, SSD_CONV_CH), nssm)
    return x2d, kvs, states


def kernel(x_prompt, x_sample, cache_kv_a1, cache_kv_a2, cache_kv_a3, state_conv_b, state_h_b, state_conv_d, state_ssm_d, norm1_g, w_in, q_norm_g, k_norm_g, conv_b_w, conv_b_b, lru_wa, lru_ba, lru_wx, lru_bx, lru_lambda, sgu_ln_g, sgu_ln_b, sgu_w, sgu_b, ssd_conv_w, ssd_conv_b, ssd_dt_bias, ssd_a_log, ssd_d, ssd_norm_g, w_pa, w_pb, w_pc, w_pd, w_o, norm2_g, router_group_w, router_group_b, router_exp_w, router_exp_b, moe_w_gate, moe_w_up, moe_w_down):
    p = dict(norm1_g=norm1_g, w_in=w_in, q_norm_g=q_norm_g, k_norm_g=k_norm_g, conv_b_w=conv_b_w,
             conv_b_b=conv_b_b, lru_wa=lru_wa, lru_ba=lru_ba, lru_wx=lru_wx, lru_bx=lru_bx,
             lru_lambda=lru_lambda, sgu_ln_g=sgu_ln_g, sgu_ln_b=sgu_ln_b, sgu_w=sgu_w, sgu_b=sgu_b,
             ssd_conv_w=ssd_conv_w, ssd_conv_b=ssd_conv_b, ssd_dt_bias=ssd_dt_bias, ssd_a_log=ssd_a_log,
             ssd_d=ssd_d, ssd_norm_g=ssd_norm_g, w_pa=w_pa, w_pb=w_pb, w_pc=w_pc, w_pd=w_pd, w_o=w_o,
             norm2_g=norm2_g, router_group_w=router_group_w, router_group_b=router_group_b,
             router_exp_w=router_exp_w, router_exp_b=router_exp_b, moe_w_gate=moe_w_gate,
             moe_w_up=moe_w_up, moe_w_down=moe_w_down)
    B, S, _ = x_prompt.shape
    Bd = x_sample.shape[0]
    depth = w_in.shape[0]
    small = _small_params(p)
    Wp = dict(small, **_matrix_params(p, BF16))
    Wd = dict(small, **_matrix_params(p, F32))
    tabs_p = _rope_tables(jnp.arange(S))
    tabs_s = tuple(jnp.broadcast_to(t, (Bd, LANES)) for t in _rope_tables(PAST_LEN + jnp.arange(1)))
    caches = (cache_kv_a1, cache_kv_a2, cache_kv_a3)
    yp, ys = x_prompt, x_sample.reshape(Bd, D_MODEL)
    P = [[] for _ in range(7)]
    Sx = [[] for _ in range(8)]
    for l in range(depth):
        yp, kvs, st = _prompt_layer(yp, Wp, l, tabs_p)
        for dst, val in zip(P, list(kvs) + list(st)):
            dst.append(val)
        ys, kvs, st = _decode_layer(ys, Wd, l, tabs_s, caches, state_conv_b[l], state_h_b[l],
                                    state_conv_d[l], state_ssm_d)
        for dst, val in zip(Sx, list(kvs) + list(st)):
            dst.append(val)
    st = jnp.stack
    return (yp, ys.reshape(Bd, 1, D_MODEL)) + tuple(st(t) for t in P) + tuple(st(t) for t in Sx)
```

```python
import functools

import jax
import jax.numpy as jnp
from jax import lax
from jax.experimental import pallas as pl
from jax.experimental.pallas import tpu as pltpu

F32 = jnp.float32
BF16 = jnp.bfloat16
HIGHEST = lax.Precision.HIGHEST

D_MODEL = 1024
DEPTH = 4
PAST_LEN = 8192
EPS = 1e-6
HEAD_DIM = 64
A_HPG = 4
A_GROUPS = ((128, 1), (512, 4), (2048, 16))
A_WIDTH = 768
A_OUT = 256
BLK = 128
ROPE_THETA = 10000.0
LRU_WIDTH = 768
LRU_C = 8.0
CONV_W = 4
SGU_WIDTH = 768
SGU_GROUPS = 4
SGU_CHUNK = 128
SSD_INNER = 768
SSD_HEADS = 12
SSD_GROUPS = 2
SSD_STATE = 128
SSD_CHUNK = 128
SSD_CONV_CH = 1280
N_BRANCH = 4
N_EXPERT_GROUPS = 4
EXPERTS_PER_GROUP = 4
N_EXPERTS = 16
D_EXPERT = 512
IN_SIZES = (768, 768, 768, 768, 1536, 768, 1280, 12, 4096)
IN_OFFS = tuple(sum(IN_SIZES[:i]) for i in range(len(IN_SIZES)))

LANES = 128
SUBLANES = 8
VMEM_LIMIT = 56 * 1024 * 1024

TM_QKV = 512
T_SEQ = 256
T_SGU = 512
TM_MERGE = 512
TM_MOE = 1024


def _cp(*sem):
    return pltpu.CompilerParams(dimension_semantics=sem, vmem_limit_bytes=VMEM_LIMIT)


def _const_spec(shape):
    nd = len(shape)
    return pl.BlockSpec(shape, lambda *_: (0,) * nd, pipeline_mode=pl.Buffered(1))


def _lspec(a, l, block=None, idx=None):
    shape = tuple(a.shape[1:]) if block is None else tuple(block)
    tail = (0,) * len(shape) if idx is None else tuple(idx)
    return pl.BlockSpec((None,) + shape, lambda *_: (l,) + tail, pipeline_mode=pl.Buffered(1))


def _rms(x, g):
    return x * lax.rsqrt(jnp.mean(x * x, axis=-1, keepdims=True) + EPS) * g


def _dot3(a, w):
    a1 = a.astype(BF16)
    a2 = (a - a1.astype(F32)).astype(BF16)
    w1 = w.astype(BF16)
    w2 = (w - w1.astype(F32)).astype(BF16)
    d = lambda x, y: jnp.dot(x, y, preferred_element_type=F32)
    return d(a1, w1) + (d(a1, w2) + d(a2, w1))


def _mm(a, w):
    if w.dtype == F32:
        return _dot3(a.astype(F32), w)
    return jnp.dot(a.astype(BF16), w, preferred_element_type=F32)


def _bdot(a, b):
    return jnp.dot(a.astype(BF16), b.astype(BF16), preferred_element_type=F32)


def _bdot_nt(a, b):
    return lax.dot_general(a.astype(BF16), b.astype(BF16), (((1,), (1,)), ((), ())),
                           preferred_element_type=F32)


def _bdot_tn(a, b):
    return lax.dot_general(a.astype(BF16), b.astype(BF16), (((0,), (0,)), ((), ())),
                           preferred_element_type=F32)


def _sigmoid(x):
    return jax.nn.sigmoid(x)


def _silu(x):
    return x * jax.nn.sigmoid(x)


def _neg_expm1(x):
    t = jnp.tanh(0.5 * x)
    return -2.0 * t / (1.0 - t)


def _lanes6(t):
    return jnp.concatenate([t] * 6, axis=1)


def _store_strided_view(t, out_ref, slab_ref, dil):
    if dil == 1:
        out_ref[...] = t.astype(out_ref.dtype)
        return
    T = t.shape[0]
    halves = A_OUT // LANES
    for s in range(halves):
        slab_ref[s] = t[:, s * LANES:(s + 1) * LANES]
    for r in range(dil):
        for s in range(halves):
            c0 = r * A_OUT + s * LANES
            out_ref[:, c0:c0 + LANES] = slab_ref[s, pl.ds(r, T // dil, stride=dil), :].astype(out_ref.dtype)


def _load_strided_view(view_ref, slab_ref, dil):
    if dil == 1:
        return view_ref[...]
    n = view_ref.shape[0]
    halves = A_OUT // LANES
    for r in range(dil):
        for s in range(halves):
            c0 = r * A_OUT + s * LANES
            slab_ref[s, pl.ds(r, n, stride=dil), :] = view_ref[:, c0:c0 + LANES]
    return jnp.concatenate([slab_ref[s] for s in range(halves)], axis=1)


def _qkv_body(x_ref, g1_ref, wq_ref, wk_ref, wv_ref, qg_ref, kg_ref, gbd_ref,
              cos_ref, sa_ref, sb_ref, *outs, split):
    h = _rms(x_ref[...], g1_ref[...]).astype(wq_ref.dtype)
    cos = _lanes6(cos_ref[...])
    sa = _lanes6(sa_ref[...])
    sb = _lanes6(sb_ref[...])
    gbd = gbd_ref[...]

    def normed_rotated(w_ref, hg_ref):
        t = _mm(h, w_ref[...])
        ss = _mm(t * t, gbd)
        tn = t * lax.rsqrt(ss * (1.0 / HEAD_DIM) + EPS) * hg_ref[...]
        return (tn * cos + pltpu.roll(tn, A_WIDTH - HEAD_DIM // 2, 1) * sa
                + pltpu.roll(tn, HEAD_DIM // 2, 1) * sb)

    q = normed_rotated(wq_ref, qg_ref) * (HEAD_DIM ** -0.5)
    k = normed_rotated(wk_ref, kg_ref)
    v = _mm(h, wv_ref[...])
    if split:
        *outs, slab_ref = outs
        for gi, (_, dil) in enumerate(A_GROUPS):
            sl = slice(gi * A_OUT, (gi + 1) * A_OUT)
            for j, t in enumerate((q, k, v)):
                _store_strided_view(t[:, sl], outs[3 * j + gi], slab_ref, dil)
        outs[9][...] = k
        outs[10][...] = v
    else:
        outs[0][...] = q
        outs[1][...] = k
        outs[2][...] = v


def _qkv(x2d, W, l, tabs, tm, split):
    R = x2d.shape[0]
    cos, sa, sb = tabs
    npos = cos.shape[0] // tm
    row = lambda i: (i, 0)
    pos = lambda i: (i % npos, 0)
    full = pl.BlockSpec((tm, A_WIDTH), row)
    scratch = []
    w_in = W["w_in"]
    wcol = lambda j: _lspec(w_in, l, (D_MODEL, A_WIDTH), (0, j))
    if split:
        dils = [dil for _, dil in A_GROUPS] * 3
        out_specs = [pl.BlockSpec((tm // d, d * A_OUT), row) for d in dils] + [full, full]
        out_shape = ([jax.ShapeDtypeStruct((R // d, d * A_OUT), BF16) for d in dils]
                     + [jax.ShapeDtypeStruct((R, A_WIDTH), F32)] * 2)
        scratch = [pltpu.VMEM((A_OUT // LANES, tm, LANES), F32)]
    else:
        out_specs = [full] * 3
        out_shape = [jax.ShapeDtypeStruct((R, A_WIDTH), F32)] * 3
    return pl.pallas_call(
        functools.partial(_qkv_body, split=split),
        grid=(R // tm,), scratch_shapes=scratch,
        in_specs=[pl.BlockSpec((tm, D_MODEL), row), _lspec(W["norm1_g"], l),
                  wcol(0), wcol(1), wcol(2),
                  _lspec(W["q_norm_g"], l), _lspec(W["k_norm_g"], l),
                  _const_spec((A_WIDTH, A_WIDTH)),
                  pl.BlockSpec((tm, LANES), pos), pl.BlockSpec((tm, LANES), pos),
                  pl.BlockSpec((tm, LANES), pos)],
        out_specs=out_specs, out_shape=out_shape,
        compiler_params=_cp("parallel"), name="qkv_proj",
    )(x2d, W["norm1_g"], w_in, w_in, w_in, W["q_norm_g"], W["k_norm_g"], W["gbd"], cos, sa, sb)


def _attn_group(q_ref, kp_ref, kc_ref, vp_ref, vc_ref, o_ref, l_ref, mask, lane_head):
    q = q_ref[...]
    zero = jnp.zeros_like(q)
    qs = jnp.concatenate([jnp.where(lane_head == hh, q, zero) for hh in range(A_HPG)], axis=0)
    kk = jnp.concatenate([kp_ref[...], kc_ref[...]], axis=0)
    vv = jnp.concatenate([vp_ref[...], vc_ref[...]], axis=0)
    s = jnp.where(mask, _bdot_nt(qs, kk), -jnp.inf)
    m = jnp.max(s, axis=-1, keepdims=True)
    e = jnp.exp(s - m)
    den = jnp.sum(e, axis=-1, keepdims=True)
    o4 = _bdot(e * (1.0 / den), vv)
    lse = m + jnp.log(den)
    o = o4[(A_HPG - 1) * BLK:]
    lo = jnp.broadcast_to(lse[(A_HPG - 1) * BLK:], (BLK, A_OUT))
    for hh in range(A_HPG - 2, -1, -1):
        sel = lane_head == hh
        o = jnp.where(sel, o4[hh * BLK:(hh + 1) * BLK], o)
        lo = jnp.where(sel, jnp.broadcast_to(lse[hh * BLK:(hh + 1) * BLK], (BLK, A_OUT)), lo)
    o_ref[...] = o
    l_ref[...] = lo


def _attn_prompt_body(*refs):
    i = pl.program_id(1)
    ins, outs = refs[:15], refs[15:]
    rows = A_HPG * BLK
    qi = lax.broadcasted_iota(jnp.int32, (rows, 2 * BLK), 0) % BLK
    kj = lax.broadcasted_iota(jnp.int32, (rows, 2 * BLK), 1)
    mask_cur = jnp.logical_and(kj >= BLK, kj - BLK <= qi)
    mask_prev = jnp.logical_and(kj < BLK, kj >= qi)
    lane_head = lax.broadcasted_iota(jnp.int32, (BLK, A_OUT), 1) // HEAD_DIM
    for gi, (_, dil) in enumerate(A_GROUPS):
        q_ref, kp_ref, kc_ref, vp_ref, vc_ref = ins[5 * gi:5 * gi + 5]
        o_ref, l_ref = outs[2 * gi:2 * gi + 2]
        mask = jnp.logical_or(mask_cur, jnp.logical_and(mask_prev, (i // dil) > 0))
        _attn_group(q_ref, kp_ref, kc_ref, vp_ref, vc_ref, o_ref, l_ref, mask, lane_head)


def _attn_prompt(qkv, B, S):
    nblk = S // BLK
    args, in_specs, out_specs, out_shape = [], [], [], []
    for gi, (_, dil) in enumerate(A_GROUPS):
        assert S % (dil * BLK) == 0
        rows = S // dil
        cur = lambda b, i, dil=dil: (b, i // dil, i % dil)
        prev = lambda b, i, dil=dil: (b, jnp.maximum(i // dil - 1, 0), i % dil)
        blk = (None, BLK, A_OUT)
        view = lambda t: t.reshape(B, rows, dil * A_OUT)
        qv, kv, vv = view(qkv[gi]), view(qkv[3 + gi]), view(qkv[6 + gi])
        args += [qv, kv, kv, vv, vv]
        in_specs += [pl.BlockSpec(blk, cur), pl.BlockSpec(blk, prev), pl.BlockSpec(blk, cur),
                     pl.BlockSpec(blk, prev), pl.BlockSpec(blk, cur)]
        out_specs += [pl.BlockSpec(blk, cur)] * 2
        out_shape += [jax.ShapeDtypeStruct((B, rows, dil * A_OUT), F32)] * 2
    res = pl.pallas_call(
        _attn_prompt_body, grid=(B, nblk), in_specs=in_specs, out_specs=out_specs,
        out_shape=out_shape, compiler_params=_cp("parallel", "parallel"), name="attn_prompt",
    )(*args)
    return [r.reshape(-1, r.shape[-1]) for r in res]


def _attn_step_body(q_ref, k_ref, v_ref, c1_ref, c2_ref, c3_ref, *outs):
    down = lambda r: jnp.broadcast_to(r[...], (LANES, A_WIDTH)).T
    q_cols, k_cols, v_cols = down(q_ref), down(k_ref), down(v_ref)
    lane2 = lax.broadcasted_iota(jnp.int32, (A_OUT, LANES), 1)
    for gi, (c_ref, (_, dil)) in enumerate(zip((c1_ref, c2_ref, c3_ref), A_GROUPS)):
        o_ref, l_ref = outs[2 * gi:2 * gi + 2]
        W = c_ref.shape[-1]
        rows = slice(gi * A_OUT, (gi + 1) * A_OUT)
        heads = lambda t: t[rows, 0:1].reshape(A_HPG, HEAD_DIM, 1)
        qh, kh, vh = heads(q_cols), heads(k_cols), heads(v_cols)
        pos = lax.broadcasted_iota(jnp.int32, (1, 1, W), 2)
        valid = (pos % dil) == 0
        s_c = jnp.where(valid, jnp.sum(c_ref[0] * qh, axis=1, keepdims=True), -jnp.inf)
        s_n = jnp.sum(qh * kh, axis=1, keepdims=True)
        m = jnp.maximum(jnp.max(s_c, axis=2, keepdims=True), s_n)
        e_c = jnp.exp(s_c - m)
        e_n = jnp.exp(s_n - m)
        den = jnp.sum(e_c, axis=2, keepdims=True) + e_n
        o = jnp.sum(c_ref[1] * (e_c / den), axis=2, keepdims=True) + (e_n / den) * vh
        lse = jnp.broadcast_to(m + jnp.log(den), (A_HPG, HEAD_DIM, 1))
        both = jnp.where(lane2 == 0, o.reshape(A_OUT, 1), jnp.where(lane2 == 1, lse.reshape(A_OUT, 1), 0.0))
        both_t = both.T
        o_ref[...] = both_t[0:1]
        l_ref[...] = both_t[1:2]


def _attn_step(q, k, v, caches, l):
    Bd = q.shape[0]
    row = lambda b: (b, 0, 0)
    args = [q.reshape(Bd, 1, A_WIDTH), k.reshape(Bd, 1, A_WIDTH), v.reshape(Bd, 1, A_WIDTH)]
    in_specs = [pl.BlockSpec((None, 1, A_WIDTH), row)] * 3
    for (window, dil), c in zip(A_GROUPS, caches):
        assert c.shape[2] == window and window % dil == 0
        args.append(jnp.transpose(c, (0, 1, 3, 4, 5, 2)))
        in_specs.append(pl.BlockSpec((None, None, 2, A_HPG, HEAD_DIM, window), lambda b: (l, b, 0, 0, 0, 0)))
    res = pl.pallas_call(
        _attn_step_body, grid=(Bd,), in_specs=in_specs,
        out_specs=[pl.BlockSpec((None, 1, A_OUT), row)] * 6,
        out_shape=[jax.ShapeDtypeStruct((Bd, 1, A_OUT), F32)] * 6,
        compiler_params=_cp("parallel"), name="attn_step",
    )(*args)
    return [r.reshape(Bd, A_OUT) for r in res]


def _lru_gates(xc, wa_ref, ba_ref, wx_ref, bx_ref, lam_ref):
    r = _sigmoid(_mm(xc, wa_ref[...]) + ba_ref[...])
    i = _sigmoid(_mm(xc, wx_ref[...]) + bx_ref[...])
    log_a = -LRU_C * r * jax.nn.softplus(-lam_ref[...])
    a = jnp.exp(log_a)
    b = jnp.sqrt(_neg_expm1(2.0 * log_a)) * (i * xc)
    return a, b


def _scan_rows(a, b, h0):
    T = a.shape[0]
    row = lax.broadcasted_iota(jnp.int32, a.shape, 0) % SUBLANES
    k = 1
    while k < SUBLANES:
        keep = row >= k
        a_s = jnp.where(keep, pltpu.roll(a, k, 0), 1.0)
        b_s = jnp.where(keep, pltpu.roll(b, k, 0), 0.0)
        b = a * b_s + b
        a = a * a_s
        k *= 2
    out, carry = [], h0
    for j in range(T // SUBLANES):
        rows = slice(j * SUBLANES, (j + 1) * SUBLANES)
        hj = b[rows] + a[rows] * carry
        out.append(hj)
        carry = hj[SUBLANES - 1:SUBLANES]
    return jnp.concatenate(out, axis=0)


def _lru_body(x_ref, g1_ref, w_ref, cw_ref, cb_ref, wa_ref, ba_ref, wx_ref, bx_ref, lam_ref,
              ob_ref, tail_ref, ext_ref, hc_ref, *, T):
    @pl.when(pl.program_id(1) == 0)
    def _():
        ext_ref[0:SUBLANES, :] = jnp.zeros((SUBLANES, LRU_WIDTH), F32)
        hc_ref[...] = jnp.zeros_like(hc_ref)

    h = _rms(x_ref[...], g1_ref[...])
    xb = _mm(h, w_ref[...])
    ext_ref[SUBLANES:SUBLANES + T, :] = xb
    xc = cb_ref[...]
    for kk in range(CONV_W - 1):
        xc = xc + ext_ref[pl.ds(SUBLANES - (CONV_W - 1) + kk, T), :] * cw_ref[kk:kk + 1, :]
    xc = xc + xb * cw_ref[CONV_W - 1:CONV_W, :]
    tail = xb[T - SUBLANES:T]
    ext_ref[0:SUBLANES, :] = tail
    tail_ref[...] = tail

    a, b = _lru_gates(xc, wa_ref, ba_ref, wx_ref, bx_ref, lam_ref)
    hfull = _scan_rows(a, b, hc_ref[0:1, :])
    ob_ref[...] = hfull
    hc_ref[...] = jnp.broadcast_to(hfull[T - 1:T], hc_ref.shape)


def _lru_prompt(x, W, l, T):
    B, S, _ = x.shape
    tile = lambda b, s: (b, s, 0)
    names = ("norm1_g", None, "conv_b_w", "conv_b_b", "lru_wa_bd", "lru_ba", "lru_wx_bd", "lru_bx", "lru_lambda")
    specs = [_lspec(W[n], l) if n else _lspec(W["w_in"], l, (D_MODEL, LRU_WIDTH), (0, IN_OFFS[3] // LRU_WIDTH))
             for n in names]
    return pl.pallas_call(
        functools.partial(_lru_body, T=T),
        grid=(B, S // T),
        in_specs=[pl.BlockSpec((None, T, D_MODEL), tile)] + specs,
        out_specs=[pl.BlockSpec((None, T, LRU_WIDTH), tile),
                   pl.BlockSpec((None, SUBLANES, LRU_WIDTH), lambda b, s: (b, 0, 0))],
        out_shape=[jax.ShapeDtypeStruct((B, S, LRU_WIDTH), F32),
                   jax.ShapeDtypeStruct((B, SUBLANES, LRU_WIDTH), F32)],
        scratch_shapes=[pltpu.VMEM((T + SUBLANES, LRU_WIDTH), F32),
                        pltpu.VMEM((SUBLANES, LRU_WIDTH), F32)],
        compiler_params=_cp("parallel", "arbitrary"), name="lru_prompt",
    )(x, *[W[n] if n else W["w_in"] for n in names])


def _gelu_ln(uv, lg_ref, lb_ref):
    uv = jax.nn.gelu(uv)
    u = uv[:, :SGU_WIDTH]
    v = uv[:, SGU_WIDTH:]
    mu = jnp.mean(v, axis=-1, keepdims=True)
    var = jnp.mean(jnp.square(v - mu), axis=-1, keepdims=True)
    v = (v - mu) * lax.rsqrt(var + EPS) * lg_ref[...] + lb_ref[...]
    return u, v


def _sgu_body(x_ref, g1_ref, w_ref, lg_ref, lb_ref, ws_ref, bs_ref, oc_ref, *, T):
    h = _rms(x_ref[...], g1_ref[...])
    u, v = _gelu_ln(_mm(h, w_ref[...]), lg_ref, lb_ref)
    qi = lax.broadcasted_iota(jnp.int32, (SGU_CHUNK, SGU_CHUNK), 0)
    kj = lax.broadcasted_iota(jnp.int32, (SGU_CHUNK, SGU_CHUNK), 1)
    tril = (kj <= qi).astype(F32)
    lane = lax.broadcasted_iota(jnp.int32, (SGU_CHUNK, SGU_WIDTH), 1)
    gw = SGU_WIDTH // SGU_GROUPS
    wms = [(ws_ref[g] * tril).astype(BF16) for g in range(SGU_GROUPS)]
    for c in range(T // SGU_CHUNK):
        rows = slice(c * SGU_CHUNK, (c + 1) * SGU_CHUNK)
        vc = v[rows].astype(BF16)
        mixed = jnp.dot(wms[SGU_GROUPS - 1], vc, preferred_element_type=F32)
        for g in range(SGU_GROUPS - 2, -1, -1):
            mixed = jnp.where(lane < (g + 1) * gw, jnp.dot(wms[g], vc, preferred_element_type=F32), mixed)
        oc_ref[rows, :] = u[rows] * (mixed + bs_ref[...])


def _sgu_prompt(x2d, W, l, T):
    R = x2d.shape[0]
    row = lambda i: (i, 0)
    return pl.pallas_call(
        functools.partial(_sgu_body, T=T),
        grid=(R // T,),
        in_specs=[pl.BlockSpec((T, D_MODEL), row), _lspec(W["norm1_g"], l),
                  _lspec(W["w_in"], l, (D_MODEL, 2 * SGU_WIDTH), (0, IN_OFFS[4] // (2 * SGU_WIDTH))),
                  _lspec(W["sgu_ln_g"], l), _lspec(W["sgu_ln_b"], l), _lspec(W["sgu_w"], l),
                  _lspec(W["sgu_b_tab"], l)],
        out_specs=pl.BlockSpec((T, SGU_WIDTH), row),
        out_shape=jax.ShapeDtypeStruct((R, SGU_WIDTH), F32),
        compiler_params=_cp("parallel"), name="sgu_prompt",
    )(x2d, W["norm1_g"], W["w_in"], W["sgu_ln_g"], W["sgu_ln_b"], W["sgu_w"], W["sgu_b_tab"])


def _ssd_dt_a(dtr, dtb_ref, alog_ref):
    lane = lax.broadcasted_iota(jnp.int32, (1, LANES), 1)
    dt = jax.nn.softplus(dtr + dtb_ref[...])
    A = jnp.where(lane < SSD_HEADS, -jnp.exp(alog_ref[...]), 0.0)
    return dt, A * dt


def _ssd_gate_norm(y, xs, z, dsk_ref, ng_ref):
    y = y + dsk_ref[...] * xs
    y = y * _silu(z)
    gw = SSD_INNER // SSD_GROUPS
    parts = []
    for g in range(SSD_GROUPS):
        yg = y[:, g * gw:(g + 1) * gw]
        parts.append(yg * lax.rsqrt(jnp.mean(yg * yg, axis=-1, keepdims=True) + EPS))
    return jnp.concatenate(parts, axis=1) * ng_ref[...]


def _ssd_chunk(xs, dt, a, Bm, Cm, st_ref, y_ref, row0):
    Q = SSD_CHUNK
    qi = lax.broadcasted_iota(jnp.int32, (Q, Q), 0)
    kj = lax.broadcasted_iota(jnp.int32, (Q, Q), 1)
    tril = kj <= qi
    cs = jnp.dot(tril.astype(F32), a, preferred_element_type=F32, precision=HIGHEST)
    cs_t = cs.T
    ecs = jnp.exp(cs)
    cs_last = cs[Q - 1:Q, :]
    to_end = jnp.exp(cs_last - cs)
    e_last = jnp.exp(cs_last)
    hpg = SSD_HEADS // SSD_GROUPS
    first = lax.broadcasted_iota(jnp.int32, (Q, LANES), 1) < HEAD_DIM
    first_row = lax.broadcasted_iota(jnp.int32, (LANES, 1), 0) < HEAD_DIM
    for g in range(SSD_GROUPS):
        Cg = Cm[:, g * SSD_STATE:(g + 1) * SSD_STATE].astype(BF16)
        Bg = Bm[:, g * SSD_STATE:(g + 1) * SSD_STATE].astype(BF16)
        G = _bdot_nt(Cg, Bg)
        for pp in range(hpg // 2):
            h0 = g * hpg + 2 * pp
            h1 = h0 + 1
            pl2 = slice(h0 * HEAD_DIM, (h0 + 2) * HEAD_DIM)
            per_head = lambda t: jnp.where(first, t[:, h0:h0 + 1], t[:, h1:h1 + 1])
            X2 = xs[:, pl2] * per_head(dt)
            L0 = jnp.exp(jnp.where(tril, cs[:, h0:h0 + 1] - cs_t[h0:h0 + 1, :], -jnp.inf))
            L1 = jnp.exp(jnp.where(tril, cs[:, h1:h1 + 1] - cs_t[h1:h1 + 1, :], -jnp.inf))
            Sp = st_ref[pl2, :]
            y = (_bdot(G * L0, jnp.where(first, X2, 0.0)) + _bdot(G * L1, jnp.where(first, 0.0, X2))
                 + per_head(ecs) * _bdot_nt(Cg, Sp))
            y_ref[row0:row0 + Q, pl2] = y
            keep = jnp.where(first_row, e_last[:, h0:h0 + 1], e_last[:, h1:h1 + 1])
            st_ref[pl2, :] = keep * Sp + _bdot_tn(X2 * per_head(to_end), Bg)


def _ssd_body(x_ref, g1_ref, wz_ref, wxbc_ref, wdt_ref, cw_ref, cb_ref, dtb_ref, alog_ref, dsk_ref,
              ng_ref, od_ref, tail_ref, fin_ref, ext_ref, st_ref, y_ref, *, T):
    @pl.when(pl.program_id(1) == 0)
    def _():
        ext_ref[0:SUBLANES, :] = jnp.zeros((SUBLANES, SSD_CONV_CH), F32)
        st_ref[...] = jnp.zeros_like(st_ref)

    h = _rms(x_ref[...], g1_ref[...]).astype(BF16)
    z = _mm(h, wz_ref[...])
    xbc = _mm(h, wxbc_ref[...])
    dtr = _mm(h, wdt_ref[...])
    ext_ref[SUBLANES:SUBLANES + T, :] = xbc
    xc = cb_ref[...]
    for kk in range(CONV_W - 1):
        xc = xc + ext_ref[pl.ds(SUBLANES - (CONV_W - 1) + kk, T), :] * cw_ref[kk:kk + 1, :]
    xc = xc + xbc * cw_ref[CONV_W - 1:CONV_W, :]
    tail = xbc[T - SUBLANES:T]
    ext_ref[0:SUBLANES, :] = tail
    tail_ref[...] = tail

    xc = _silu(xc)
    xs = xc[:, :SSD_INNER]
    gn = SSD_GROUPS * SSD_STATE
    Bm = xc[:, SSD_INNER:SSD_INNER + gn]
    Cm = xc[:, SSD_INNER + gn:]
    dt, a = _ssd_dt_a(dtr, dtb_ref, alog_ref)
    for c in range(T // SSD_CHUNK):
        rows = slice(c * SSD_CHUNK, (c + 1) * SSD_CHUNK)
        _ssd_chunk(xs[rows], dt[rows], a[rows], Bm[rows], Cm[rows], st_ref, y_ref, c * SSD_CHUNK)
    od_ref[...] = _ssd_gate_norm(y_ref[...], xs, z, dsk_ref, ng_ref)
    fin_ref[...] = st_ref[...]


def _ssd_prompt(x, W, l, T):
    B, S, _ = x.shape
    tile = lambda b, s: (b, s, 0)
    perb = lambda b, s: (b, 0, 0)
    wx = W["w_xbcdt"]
    return pl.pallas_call(
        functools.partial(_ssd_body, T=T),
        grid=(B, S // T),
        in_specs=[pl.BlockSpec((None, T, D_MODEL), tile), _lspec(W["norm1_g"], l),
                  _lspec(W["w_in"], l, (D_MODEL, SSD_INNER), (0, IN_OFFS[5] // SSD_INNER)),
                  _lspec(wx, l, (D_MODEL, SSD_CONV_CH), (0, 0)),
                  _lspec(wx, l, (D_MODEL, LANES), (0, SSD_CONV_CH // LANES)),
                  _lspec(W["ssd_conv_w"], l), _lspec(W["ssd_conv_b"], l), _lspec(W["ssd_dt_bias"], l),
                  _lspec(W["ssd_a_log"], l), _lspec(W["ssd_d_tab"], l), _lspec(W["ssd_norm_g"], l)],
        out_specs=[pl.BlockSpec((None, T, SSD_INNER), tile),
                   pl.BlockSpec((None, SUBLANES, SSD_CONV_CH), perb),
                   pl.BlockSpec((None, SSD_INNER, SSD_STATE), perb)],
        out_shape=[jax.ShapeDtypeStruct((B, S, SSD_INNER), F32),
                   jax.ShapeDtypeStruct((B, SUBLANES, SSD_CONV_CH), F32),
                   jax.ShapeDtypeStruct((B, SSD_INNER, SSD_STATE), F32)],
        scratch_shapes=[pltpu.VMEM((T + SUBLANES, SSD_CONV_CH), F32),
                        pltpu.VMEM((SSD_INNER, SSD_STATE), F32),
                        pltpu.VMEM((T, SSD_INNER), F32)],
        compiler_params=_cp("parallel", "arbitrary"), name="ssd_prompt",
    )(x, W["norm1_g"], W["w_in"], wx, wx, W["ssd_conv_w"], W["ssd_conv_b"],
      W["ssd_dt_bias"], W["ssd_a_log"], W["ssd_d_tab"], W["ssd_norm_g"])


def _proj_body(x_ref, g1_ref, w_ref, o_ref):
    o_ref[...] = _mm(_rms(x_ref[...], g1_ref[...]), w_ref[...])


def _proj(x2d, W, l, col0, ncols, tn):
    R = x2d.shape[0]
    assert col0 % tn == 0
    j0 = col0 // tn
    nblk = pl.cdiv(ncols, tn)
    return pl.pallas_call(
        _proj_body, grid=(nblk,),
        in_specs=[_const_spec((R, D_MODEL)), _lspec(W["norm1_g"], l),
                  pl.BlockSpec((None, D_MODEL, tn), lambda j: (l, 0, j0 + j))],
        out_specs=pl.BlockSpec((R, tn), lambda j: (0, j)),
        out_shape=jax.ShapeDtypeStruct((R, nblk * tn), F32),
        compiler_params=_cp("parallel"), name="proj_rest",
    )(x2d, W["norm1_g"], W["w_in"])


def _conv_step(xnew, st_ref, cw_ref, cb_ref, C):
    out = cb_ref[...]
    for kk in range(CONV_W - 1):
        out = out + st_ref[:, kk * C:(kk + 1) * C] * cw_ref[kk:kk + 1, :]
    return out + xnew * cw_ref[CONV_W - 1:CONV_W, :]


def _dec_branches_body(proj_ref, dtr_ref, scb_ref, hb_ref, scd_ref,
                       cwb_ref, cbb_ref, wa_ref, ba_ref, wx_ref, bx_ref, lam_ref,
                       lg_ref, lb_ref, w00_ref, b0_ref,
                       cwd_ref, cbd_ref, dtb_ref, alog_ref,
                       ob_ref, ncb_ref, oc_ref, vc_ref, ncd_ref, xs_ref, xdt_ref, ea_ref, bc_ref,
                       *, offs):
    o_xb, o_uv, o_xbc = offs
    xb = proj_ref[:, o_xb:o_xb + LRU_WIDTH]
    xc = _conv_step(xb, scb_ref, cwb_ref, cbb_ref, LRU_WIDTH)
    a, b = _lru_gates(xc, wa_ref, ba_ref, wx_ref, bx_ref, lam_ref)
    ob_ref[...] = a * hb_ref[...] + b
    ncb_ref[:, 0:2 * LRU_WIDTH] = scb_ref[:, LRU_WIDTH:3 * LRU_WIDTH]
    ncb_ref[:, 2 * LRU_WIDTH:3 * LRU_WIDTH] = xb
    u, v = _gelu_ln(proj_ref[:, o_uv:o_uv + 2 * SGU_WIDTH], lg_ref, lb_ref)
    vc_ref[...] = v
    oc_ref[...] = u * (w00_ref[...] * v + b0_ref[...])
    xbc = proj_ref[:, o_xbc:o_xbc + SSD_CONV_CH]
    xcd = _silu(_conv_step(xbc, scd_ref, cwd_ref, cbd_ref, SSD_CONV_CH))
    ncd_ref[:, 0:2 * SSD_CONV_CH] = scd_ref[:, SSD_CONV_CH:3 * SSD_CONV_CH]
    ncd_ref[:, 2 * SSD_CONV_CH:3 * SSD_CONV_CH] = xbc
    xs = xcd[:, :SSD_INNER]
    dt, a_dt = _ssd_dt_a(dtr_ref[...], dtb_ref, alog_ref)
    hrow = lax.broadcasted_iota(jnp.int32, (LANES, SSD_INNER), 0)
    hlane = lax.broadcasted_iota(jnp.int32, (LANES, SSD_INNER), 1)
    expand = (hlane // HEAD_DIM == hrow).astype(F32)
    dt_e = jnp.dot(dt, expand, preferred_element_type=F32, precision=HIGHEST)
    a_e = jnp.dot(a_dt, expand, preferred_element_type=F32, precision=HIGHEST)
    xs_ref[...] = xs
    xdt_ref[...] = xs * dt_e
    ea_ref[...] = jnp.exp(a_e)
    bc_ref[...] = xcd[:, SSD_INNER:]


def _dec_branches(proj, dtr, offs, scb, hb, scd, W, l):
    Bd = proj.shape[0]
    f = lambda n: jax.ShapeDtypeStruct((Bd, n), F32)
    whole = lambda a: pl.BlockSpec(a.shape, lambda i: (0,) * a.ndim)
    names = ("conv_b_w", "conv_b_b", "lru_wa_bd", "lru_ba", "lru_wx_bd", "lru_bx", "lru_lambda",
             "sgu_ln_g", "sgu_ln_b", "sgu_w00_tab", "sgu_b0_tab",
             "ssd_conv_w", "ssd_conv_b", "ssd_dt_bias", "ssd_a_log")
    acts = (proj, dtr, scb, hb, scd)
    widths = (LRU_WIDTH, 3 * LRU_WIDTH, SGU_WIDTH, SGU_WIDTH, 3 * SSD_CONV_CH, SSD_INNER, SSD_INNER,
              SSD_INNER, 2 * SSD_GROUPS * SSD_STATE)
    return pl.pallas_call(
        functools.partial(_dec_branches_body, offs=offs),
        grid=(1,),
        in_specs=[whole(a) for a in acts] + [_lspec(W[n], l) for n in names],
        out_specs=[pl.BlockSpec((Bd, n), lambda i: (0, 0)) for n in widths],
        out_shape=[f(n) for n in widths],
        compiler_params=_cp("arbitrary"), name="dec_branches",
    )(*acts, *[W[n] for n in names])


def _dec_ssd_body(xs_ref, xdt_ref, ea_ref, bc_ref, z_ref, st_ref, dsk_ref, ng_ref, od_ref, ns_ref):
    gn = SSD_GROUPS * SSD_STATE
    half = SSD_INNER // SSD_GROUPS
    ridx = lax.broadcasted_iota(jnp.int32, (LANES, SSD_INNER), 0)
    rows = jnp.where(ridx == 0, jnp.broadcast_to(xdt_ref[...], (LANES, SSD_INNER)),
                     jnp.where(ridx == 1, jnp.broadcast_to(ea_ref[...], (LANES, SSD_INNER)), 0.0))
    cols = rows.T
    xdt_c = cols[:, 0:1]
    ea_c = cols[:, 1:2]
    bc = bc_ref[...]
    Bm, Cm = bc[:, :gn], bc[:, gn:]
    rowi = lax.broadcasted_iota(jnp.int32, (SSD_INNER, SSD_STATE), 0)
    b_full = jnp.where(rowi < half, jnp.broadcast_to(Bm[:, :SSD_STATE], (SSD_INNER, SSD_STATE)),
                       jnp.broadcast_to(Bm[:, SSD_STATE:], (SSD_INNER, SSD_STATE)))
    new = ea_c * st_ref[...] + xdt_c * b_full
    ns_ref[...] = new
    cidx = lax.broadcasted_iota(jnp.int32, (SUBLANES, SSD_STATE), 0)
    c8 = jnp.where(cidx == 0, jnp.broadcast_to(Cm[:, :SSD_STATE], (SUBLANES, SSD_STATE)),
                   jnp.where(cidx == 1, jnp.broadcast_to(Cm[:, SSD_STATE:], (SUBLANES, SSD_STATE)), 0.0))
    y8 = lax.dot_general(c8, new, (((1,), (1,)), ((), ())), preferred_element_type=F32,
                         precision=HIGHEST)
    lane = lax.broadcasted_iota(jnp.int32, (1, SSD_INNER), 1)
    y = jnp.where(lane < half, y8[0:1], y8[1:2])
    od_ref[...] = _ssd_gate_norm(y, xs_ref[...], z_ref[...], dsk_ref, ng_ref)


def _dec_ssd(xs, xdt, ea, bc, z, state_all, W, l):
    Bd = xs.shape[0]
    row = lambda b: (b, 0, 0)
    r3 = lambda t: t.reshape(Bd, 1, t.shape[-1])
    st = state_all.reshape(state_all.shape[0], Bd, SSD_INNER, SSD_STATE)
    od, ns = pl.pallas_call(
        _dec_ssd_body, grid=(Bd,),
        in_specs=[pl.BlockSpec((None, 1, SSD_INNER), row)] * 3
                 + [pl.BlockSpec((None, 1, 2 * SSD_GROUPS * SSD_STATE), row),
                    pl.BlockSpec((None, 1, SSD_INNER), row),
                    pl.BlockSpec((None, None, SSD_INNER, SSD_STATE), lambda b: (l, b, 0, 0)),
                    _lspec(W["ssd_d_tab"], l), _lspec(W["ssd_norm_g"], l)],
        out_specs=[pl.BlockSpec((None, 1, SSD_INNER), row),
                   pl.BlockSpec((None, SSD_INNER, SSD_STATE), row)],
        out_shape=[jax.ShapeDtypeStruct((Bd, 1, SSD_INNER), F32),
                   jax.ShapeDtypeStruct((Bd, SSD_INNER, SSD_STATE), F32)],
        compiler_params=_cp("parallel"), name="dec_ssd",
    )(r3(xs), r3(xdt), r3(ea), r3(bc), r3(z), st, W["ssd_d_tab"], W["ssd_norm_g"])
    return od.reshape(Bd, SSD_INNER), ns.reshape(state_all.shape[1:])


def _merge_body(x_ref, o1_ref, l1_ref, o2_ref, l2_ref, o3_ref, l3_ref, ob_ref, oc_ref, od_ref,
                g1_ref, wg_ref, wpa_ref, wpb_ref, wpc_ref, wpd_ref, wo_ref, out_ref, *slab, dils, pre_gates):
    x = x_ref[...]
    h = None if pre_gates else _rms(x, g1_ref[...]).astype(wg_ref.dtype)
    tok = lambda ref, d: _load_strided_view(ref, slab[0], d) if d > 1 else ref[...]
    o1, o2, o3 = (tok(r, d) for r, d in zip((o1_ref, o2_ref, o3_ref), dils))
    l1, l2, l3 = (tok(r, d) for r, d in zip((l1_ref, l2_ref, l3_ref), dils))
    m = jnp.maximum(jnp.maximum(l1, l2), l3)
    e1, e2, e3 = jnp.exp(l1 - m), jnp.exp(l2 - m), jnp.exp(l3 - m)
    den = e1 + e2 + e3
    oa = (e1 / den) * o1 + (e2 / den) * o2 + (e3 / den) * o3
    merged = None
    for bi, (o, w_ref) in enumerate(((oa, wpa_ref), (ob_ref[...], wpb_ref),
                                     (oc_ref[...], wpc_ref), (od_ref[...], wpd_ref))):
        cols = slice(bi * D_MODEL, (bi + 1) * D_MODEL)
        gate = _sigmoid(wg_ref[:, cols] if pre_gates else _mm(h, wg_ref[:, cols]))
        term = gate * _mm(o, w_ref[...])
        merged = term if merged is None else merged + term
    out_ref[...] = x + _mm(merged, wo_ref[...])


def _merge(x2d, att, ob, oc, od, W, l, tm, dils, gates=None):
    R = x2d.shape[0]
    row = lambda i: (i, 0)
    names = ("norm1_g", "w_gates", "w_pa", "w_pb", "w_pc", "w_pd", "w_o")
    params = [W[n] for n in names]
    pspecs = [_lspec(a, l) for a in params]
    if gates is not None:
        params[1], pspecs[1] = gates, pl.BlockSpec((tm, N_BRANCH * D_MODEL), row)
    att_specs = [pl.BlockSpec((tm // d, d * A_OUT), row) for d in dils for _ in range(2)]
    scratch = [pltpu.VMEM((A_OUT // LANES, tm, LANES), F32)] if max(dils) > 1 else []
    return pl.pallas_call(
        functools.partial(_merge_body, dils=dils, pre_gates=gates is not None), grid=(R // tm,),
        in_specs=[pl.BlockSpec((tm, D_MODEL), row)] + att_specs
                 + [pl.BlockSpec((tm, LRU_WIDTH), row)] * 3 + pspecs,
        out_specs=pl.BlockSpec((tm, D_MODEL), row),
        out_shape=jax.ShapeDtypeStruct((R, D_MODEL), F32), scratch_shapes=scratch,
        compiler_params=_cp("parallel"), name="merge",
    )(x2d, *att, ob, oc, od, *params)


def _route(logits):
    lane = lax.broadcasted_iota(jnp.int32, logits.shape, 1)
    big = jnp.int32(LANES)
    ninf = -jnp.inf
    gl = jnp.where(lane < N_EXPERT_GROUPS, logits, ninf)
    gm = jnp.max(gl, axis=-1, keepdims=True)
    gsel = jnp.min(jnp.where(gl == gm, lane, big), axis=-1, keepdims=True)
    pg = 1.0 / jnp.sum(jnp.exp(gl - gm), axis=-1, keepdims=True)
    lo = N_EXPERT_GROUPS + EXPERTS_PER_GROUP * gsel
    el = jnp.where(jnp.logical_and(lane >= lo, lane < lo + EXPERTS_PER_GROUP), logits, ninf)
    t1 = jnp.max(el, axis=-1, keepdims=True)
    i1 = jnp.min(jnp.where(el == t1, lane, big), axis=-1, keepdims=True)
    el2 = jnp.where(lane == i1, ninf, el)
    t2 = jnp.max(el2, axis=-1, keepdims=True)
    i2 = jnp.min(jnp.where(el2 == t2, lane, big), axis=-1, keepdims=True)
    e2 = jnp.exp(t2 - t1)
    den = 1.0 + e2
    w1 = (1.0 / den) * pg
    w2 = (e2 / den) * pg
    return jnp.where(lane == i1, w1, 0.0) + jnp.where(lane == i2, w2, 0.0), gsel


def _moe_body(x_ref, g2_ref, wr_ref, br_ref, wg_ref, wu_ref, wd_ref, out_ref, h_ref, comb_ref, acc_ref):
    e = pl.program_id(1)

    @pl.when(e == 0)
    def _():
        h = _rms(x_ref[...], g2_ref[...])
        h_ref[...] = h.astype(h_ref.dtype)
        logits = jnp.dot(h, wr_ref[...], preferred_element_type=F32, precision=HIGHEST) + br_ref[...]
        comb_ref[...] = _route(logits)[0]
        acc_ref[...] = jnp.zeros_like(acc_ref)

    h = h_ref[...]
    lane = lax.broadcasted_iota(jnp.int32, comb_ref.shape, 1)
    c = jnp.sum(jnp.where(lane == N_EXPERT_GROUPS + e, comb_ref[...], 0.0), axis=-1, keepdims=True)
    hg = _mm(h, wg_ref[...])
    hu = _mm(h, wu_ref[...])
    act = _silu(hg) * hu * c
    acc_ref[...] += _mm(act, wd_ref[...])

    @pl.when(e == N_EXPERTS - 1)
    def _():
        out_ref[...] = x_ref[...] + acc_ref[...]


def _moe(x2d, W, l, tm):
    R = x2d.shape[0]
    row = lambda i, e: (i, 0)
    expert = lambda r, c: pl.BlockSpec((None, None, r, c), lambda i, e: (l, e, 0, 0))
    return pl.pallas_call(
        _moe_body, grid=(R // tm, N_EXPERTS),
        in_specs=[pl.BlockSpec((tm, D_MODEL), row), _lspec(W["norm2_g"], l), _lspec(W["w_router"], l),
                  _lspec(W["b_router"], l),
                  expert(D_MODEL, D_EXPERT), expert(D_MODEL, D_EXPERT), expert(D_EXPERT, D_MODEL)],
        out_specs=pl.BlockSpec((tm, D_MODEL), row),
        out_shape=jax.ShapeDtypeStruct((R, D_MODEL), F32),
        scratch_shapes=[pltpu.VMEM((tm, D_MODEL), W["moe_w_gate"].dtype), pltpu.VMEM((tm, LANES), F32),
                        pltpu.VMEM((tm, D_MODEL), F32)],
        compiler_params=_cp("parallel", "arbitrary"), name="moe",
    )(x2d, W["norm2_g"], W["w_router"], W["b_router"], W["moe_w_gate"], W["moe_w_up"],
      W["moe_w_down"])


GID_LANE = N_EXPERT_GROUPS + N_EXPERTS
SEG_ALIGN = 16
MOE_RB = TM_MOE // N_EXPERT_GROUPS + 32
MOE_TAIL = 64
PERM_BLK = 128
UNPERM_BLK = 256
MOE_USED = -(-(TM_MOE + N_EXPERT_GROUPS * SEG_ALIGN) // PERM_BLK) * PERM_BLK
MOE_TMP = MOE_USED + MOE_RB
assert MOE_RB % SEG_ALIGN == 0 and TM_MOE % UNPERM_BLK == 0


def _route_body(x_ref, g2_ref, wr_ref, br_ref, comb_ref, cnt_ref):
    h = _rms(x_ref[...], g2_ref[...])
    logits = _dot3(h, wr_ref[...]) + br_ref[...]
    comb, gsel = _route(logits)
    lane = lax.broadcasted_iota(jnp.int32, comb.shape, 1)
    comb_ref[...] = jnp.where(lane == GID_LANE, gsel.astype(F32), comb)
    counts = jnp.sum((lane == gsel).astype(F32), axis=0, keepdims=True)
    cnt_ref[...] = jnp.broadcast_to(counts, cnt_ref.shape)


def _route_call(x2d, W, l, tm):
    R = x2d.shape[0]
    return pl.pallas_call(
        _route_body, grid=(R // tm,),
        in_specs=[pl.BlockSpec((tm, D_MODEL), lambda i: (i, 0)), _lspec(W["norm2_g"], l),
                  _lspec(W["w_router"], l), _lspec(W["b_router"], l)],
        out_specs=[pl.BlockSpec((tm, LANES), lambda i: (i, 0)),
                   pl.BlockSpec((None, SUBLANES, LANES), lambda i: (i, 0, 0))],
        out_shape=[jax.ShapeDtypeStruct((R, LANES), F32),
                   jax.ShapeDtypeStruct((R // tm, SUBLANES, LANES), F32)],
        compiler_params=_cp("parallel"), name="moe_route",
    )(x2d, W["norm2_g"], W["w_router"], W["b_router"])


def _moe_sorted_body(off_ref, nch_ref, ntail_ref, x_ref, comb_ref, g2_ref, wg_ref, wu_ref, wd_ref, out_ref,
                     sh_ref, sc_ref, acc_ref, rcol_ref, lo_ref, *, TM, RB):
    i = pl.program_id(0)
    g = pl.program_id(1)
    TMP = sh_ref.shape[0]

    @pl.when(g == 0)
    def _():
        h2 = _rms(x_ref[...], g2_ref[...]).astype(BF16)
        comb = comb_ref[...]
        lane = lax.broadcasted_iota(jnp.int32, comb.shape, 1)
        gid = jnp.sum(jnp.where(lane == GID_LANE, comb, 0.0), axis=-1, keepdims=True)
        onehot = jnp.logical_and(lane < N_EXPERT_GROUPS, lane.astype(F32) == gid).astype(F32)
        trow = lax.broadcasted_iota(jnp.int32, onehot.shape, 0)
        run = onehot
        k = 1
        while k < TM:
            run = run + jnp.where(trow >= k, pltpu.roll(run, k, 0), 0.0)
            k *= 2
        prefix = run - onehot
        lane1 = lax.broadcasted_iota(jnp.int32, (1, LANES), 1)
        offv = jnp.zeros((1, LANES), F32)
        for gg in range(N_EXPERT_GROUPS):
            offv = jnp.where(lane1 == gg, off_ref[i * N_EXPERT_GROUPS + gg].astype(F32), offv)
        rank = jnp.sum(onehot * (offv + prefix), axis=-1, keepdims=True)
        rank_b = jnp.broadcast_to(rank, (TM, LANES))
        rcol_ref[...] = rank_b
        rank_row = rank_b.T[0:1]
        c1 = comb.astype(BF16)
        c2 = (comb - c1.astype(F32)).astype(BF16)
        payload = jnp.concatenate([h2, c1, c2], axis=1)
        for blk in range(MOE_USED // PERM_BLK):
            rows = slice(blk * PERM_BLK, (blk + 1) * PERM_BLK)
            srow = lax.broadcasted_iota(jnp.int32, (PERM_BLK, TM), 0) + blk * PERM_BLK
            perm = (srow.astype(F32) == rank_row).astype(BF16)
            moved = jnp.dot(perm, payload, preferred_element_type=F32)
            sh_ref[rows, :] = moved[:, :D_MODEL].astype(BF16)
            sc_ref[rows, :] = moved[:, D_MODEL:D_MODEL + LANES] + moved[:, D_MODEL + LANES:]
        sh_ref[MOE_USED:, :] = jnp.zeros((TMP - MOE_USED, D_MODEL), BF16)
        sc_ref[MOE_USED:, :] = jnp.zeros((TMP - MOE_USED, LANES), F32)
        acc_ref[...] = jnp.zeros_like(acc_ref)

    off = off_ref[i * N_EXPERT_GROUPS + g]
    nbig = nch_ref[i * N_EXPERT_GROUPS + g]

    def expert_pass(r0, n):
        lane_c = lax.broadcasted_iota(jnp.int32, (n, LANES), 1)
        rows = sh_ref[pl.ds(r0, n), :]
        cc = sc_ref[pl.ds(r0, n), :]
        y = jnp.zeros((n, D_MODEL), F32)
        for e in range(EXPERTS_PER_GROUP):
            ce = jnp.sum(jnp.where(lane_c == N_EXPERT_GROUPS + EXPERTS_PER_GROUP * g + e, cc, 0.0),
                         axis=-1, keepdims=True)
            hg = jnp.dot(rows, wg_ref[e], preferred_element_type=F32)
            hu = jnp.dot(rows, wu_ref[e], preferred_element_type=F32)
            y = y + _bdot(_silu(hg) * hu * ce, wd_ref[e])
        acc_ref[pl.ds(r0, n), :] += y

    def full_pass(j, carry):
        expert_pass(pl.multiple_of(off + j * RB, SEG_ALIGN), RB)
        return carry

    def tail_pass(j, carry):
        expert_pass(pl.multiple_of(off + nbig * RB, SEG_ALIGN), MOE_TAIL)
        return carry

    lax.fori_loop(0, nbig, full_pass, 0)
    lax.fori_loop(0, ntail_ref[i * N_EXPERT_GROUPS + g], tail_pass, 0)

    @pl.when(g == N_EXPERT_GROUPS - 1)
    def _():
        for blk in range(MOE_USED // PERM_BLK):
            rows = slice(blk * PERM_BLK, (blk + 1) * PERM_BLK)
            a = acc_ref[rows, :]
            hi = a.astype(BF16)
            sh_ref[rows, :] = hi
            lo_ref[rows, :] = (a - hi.astype(F32)).astype(BF16)
        scol = lax.broadcasted_iota(jnp.int32, (UNPERM_BLK, MOE_USED), 1).astype(F32)
        for blk in range(TM // UNPERM_BLK):
            rows = slice(blk * UNPERM_BLK, (blk + 1) * UNPERM_BLK)
            unperm = (scol == rcol_ref[rows, 0:1]).astype(BF16)
            out_ref[rows, :] = x_ref[rows, :] + (
                jnp.dot(unperm, sh_ref[0:MOE_USED, :], preferred_element_type=F32)
                + jnp.dot(unperm, lo_ref[0:MOE_USED, :], preferred_element_type=F32))


def _moe_sorted(x2d, W, l):
    R = x2d.shape[0]
    TM, RB, TMP = TM_MOE, MOE_RB, MOE_TMP
    nt = R // TM
    comb, counts = _route_call(x2d, W, l, TM)
    cnt = counts[:, 0, :N_EXPERT_GROUPS].astype(jnp.int32)
    padded = (cnt + SEG_ALIGN - 1) // SEG_ALIGN * SEG_ALIGN
    off = (jnp.cumsum(padded, axis=1) - padded).reshape(-1)
    rem = cnt % RB
    nch = (cnt // RB + (rem > MOE_TAIL)).astype(jnp.int32).reshape(-1)
    ntail = jnp.logical_and(rem > 0, rem <= MOE_TAIL).astype(jnp.int32).reshape(-1)
    row = lambda i, g, *_: (i, 0)
    const = lambda a: pl.BlockSpec((None,) + tuple(a.shape[1:]), lambda i, g, *_: (l,) + (0,) * (a.ndim - 1))
    grp = lambda r, c: pl.BlockSpec((None, None, EXPERTS_PER_GROUP, r, c), lambda i, g, *_: (l, g, 0, 0, 0))
    gview = lambda a: a.reshape(a.shape[0], N_EXPERT_GROUPS, EXPERTS_PER_GROUP, a.shape[2], a.shape[3])
    return pl.pallas_call(
        functools.partial(_moe_sorted_body, TM=TM, RB=RB),
        grid_spec=pltpu.PrefetchScalarGridSpec(
            num_scalar_prefetch=3, grid=(nt, N_EXPERT_GROUPS),
            in_specs=[pl.BlockSpec((TM, D_MODEL), row, pipeline_mode=pl.Buffered(1)),
                      pl.BlockSpec((TM, LANES), row), const(W["norm2_g"]),
                      grp(D_MODEL, D_EXPERT), grp(D_MODEL, D_EXPERT), grp(D_EXPERT, D_MODEL)],
            out_specs=pl.BlockSpec((TM, D_MODEL), row),
            scratch_shapes=[pltpu.VMEM((TMP, D_MODEL), BF16), pltpu.VMEM((TMP, LANES), F32),
                            pltpu.VMEM((TMP, D_MODEL), F32), pltpu.VMEM((TM, LANES), F32),
                            pltpu.VMEM((TMP, D_MODEL), BF16)]),
        out_shape=jax.ShapeDtypeStruct((R, D_MODEL), F32),
        compiler_params=_cp("parallel", "arbitrary"), name="moe_sorted",
    )(off, nch, ntail, x2d, comb, W["norm2_g"], gview(W["moe_w_gate"]), gview(W["moe_w_up"]), gview(W["moe_w_down"]))


def _rope_tables(pos):
    half = HEAD_DIM // 2
    inv = ROPE_THETA ** (-jnp.arange(half, dtype=F32) / half)
    ang = pos.astype(F32)[:, None] * inv[None]
    c, s = jnp.cos(ang), jnp.sin(ang)
    z = jnp.zeros_like(s)
    reps = LANES // HEAD_DIM
    return (jnp.tile(jnp.concatenate([c, c], 1), (1, reps)),
            jnp.tile(jnp.concatenate([-s, z], 1), (1, reps)),
            jnp.tile(jnp.concatenate([z, s], 1), (1, reps)))


def _block_diag(w):
    L, n, k, _ = w.shape
    eye = jnp.eye(n, dtype=w.dtype)
    return (eye[None, :, None, :, None] * w[:, :, :, None, :]).reshape(L, n * k, n * k)


def _small_params(p):
    vec = lambda t: t.reshape(t.shape[0], 1, -1)
    W = {}
    for n in ("norm1_g", "norm2_g", "conv_b_b", "lru_ba", "lru_bx", "lru_lambda", "sgu_ln_g", "sgu_ln_b",
              "ssd_conv_b", "ssd_norm_g"):
        W[n] = vec(p[n])
    reps = A_WIDTH // HEAD_DIM
    W["q_norm_g"] = vec(jnp.tile(p["q_norm_g"], (1, reps)))
    W["k_norm_g"] = vec(jnp.tile(p["k_norm_g"], (1, reps)))
    W["conv_b_w"] = p["conv_b_w"]
    W["ssd_conv_w"] = p["ssd_conv_w"]
    W["sgu_w"] = p["sgu_w"]
    gw = SGU_WIDTH // SGU_GROUPS
    W["sgu_b_tab"] = jnp.repeat(jnp.swapaxes(p["sgu_b"], 1, 2), gw, axis=2)
    W["sgu_w00_tab"] = vec(jnp.repeat(p["sgu_w"][:, :, 0, 0], gw, axis=1))
    W["sgu_b0_tab"] = vec(jnp.repeat(p["sgu_b"][:, :, 0], gw, axis=1))
    padh = lambda t: vec(jnp.pad(t, ((0, 0), (0, LANES - SSD_HEADS))))
    W["ssd_dt_bias"], W["ssd_a_log"] = padh(p["ssd_dt_bias"]), padh(p["ssd_a_log"])
    W["ssd_d_tab"] = vec(jnp.repeat(p["ssd_d"], HEAD_DIM, axis=1))
    we = jnp.transpose(p["router_exp_w"], (0, 2, 1, 3)).reshape(-1, D_MODEL, N_EXPERTS)
    wr = jnp.concatenate([p["router_group_w"], we], axis=2)
    W["w_router"] = jnp.pad(wr, ((0, 0), (0, 0), (0, LANES - wr.shape[2])))
    br = jnp.concatenate([p["router_group_b"], p["router_exp_b"].reshape(-1, N_EXPERTS)], axis=1)
    W["b_router"] = vec(jnp.pad(br, ((0, 0), (0, LANES - br.shape[1]))))
    return W


def _matrix_params(p, wdt):
    W = {}
    w_in = p["w_in"].astype(wdt)
    W["w_in"] = w_in
    o_xbc, o_dt, o_g = IN_OFFS[6], IN_OFFS[7], IN_OFFS[8]
    W["w_xbcdt"] = jnp.concatenate(
        [w_in[:, :, o_xbc:o_dt], jnp.pad(w_in[:, :, o_dt:o_g], ((0, 0), (0, 0), (0, LANES - SSD_HEADS)))], axis=2)
    W["w_gates"] = w_in[:, :, o_g:]
    reps = A_WIDTH // HEAD_DIM
    W["gbd"] = _block_diag(jnp.ones((1, reps, HEAD_DIM, HEAD_DIM), wdt))[0]
    W["lru_wa_bd"] = _block_diag(p["lru_wa"]).astype(wdt)
    W["lru_wx_bd"] = _block_diag(p["lru_wx"]).astype(wdt)
    for n in ("w_pa", "w_pb", "w_pc", "w_pd", "w_o", "moe_w_gate", "moe_w_up", "moe_w_down"):
        W[n] = p[n].astype(wdt)
    return W


def _kv_rows(k, v, gi):
    sl = slice(gi * A_OUT, (gi + 1) * A_OUT)
    shp = k.shape[:-1] + (A_HPG, HEAD_DIM)
    return jnp.stack([k[..., sl].reshape(shp), v[..., sl].reshape(shp)], axis=-3)


def _prompt_layer(x, W, l, tabs):
    B, S, _ = x.shape
    x2d = x.reshape(B * S, D_MODEL)
    *qkv, k, v = _qkv(x2d, W, l, tabs, TM_QKV, split=True)
    att = _attn_prompt(qkv, B, S)
    ob, tail_b = _lru_prompt(x, W, l, T_SEQ)
    oc = _sgu_prompt(x2d, W, l, T_SGU)
    od, tail_d, fin = _ssd_prompt(x, W, l, T_SEQ)
    x2d = _merge(x2d, att, ob.reshape(B * S, -1), oc, od.reshape(B * S, -1), W, l, TM_MERGE,
                 tuple(dil for _, dil in A_GROUPS))
    x2d = _moe_sorted(x2d, W, l)
    k3, v3 = k.reshape(B, S, -1), v.reshape(B, S, -1)
    kvs = [_kv_rows(k3[:, S - min(w, S):], v3[:, S - min(w, S):], gi) for gi, (w, _) in enumerate(A_GROUPS)]
    nb = CONV_W - 1
    states = (tail_b[:, SUBLANES - nb:], ob[:, -1], tail_d[:, SUBLANES - nb:],
              fin.reshape(B, SSD_HEADS, HEAD_DIM, SSD_STATE))
    return x2d.reshape(B, S, D_MODEL), kvs, states


def _decode_layer(x2d, W, l, tabs, caches, scb, hb, scd, ssm_all):
    Bd = x2d.shape[0]
    q, k, v = _qkv(x2d, W, l, tabs, Bd, split=False)
    att = _attn_step(q, k, v, caches, l)
    col0 = IN_OFFS[3]
    proj = _proj(x2d, W, l, col0, sum(IN_SIZES) - col0, 256)
    dtr = proj[:, IN_OFFS[7] - col0:IN_OFFS[7] - col0 + LANES]
    gates = proj[:, IN_OFFS[8] - col0:IN_OFFS[8] - col0 + N_BRANCH * D_MODEL]
    offs = (0, IN_OFFS[4] - col0, IN_OFFS[6] - col0)
    nb = CONV_W - 1
    ob, ncb, oc, vc, ncd, xs, xdt, ea, bc = _dec_branches(
        proj, dtr, offs, scb.reshape(Bd, nb * LRU_WIDTH), hb, scd.reshape(Bd, nb * SSD_CONV_CH), W, l)
    z = proj[:, IN_OFFS[5] - col0:IN_OFFS[5] - col0 + SSD_INNER]
    od, nssm = _dec_ssd(xs, xdt, ea, bc, z, ssm_all, W, l)
    x2d = _merge(x2d, att, ob, oc, od, W, l, Bd, (1,) * len(A_GROUPS), gates=gates)
    x2d = _moe(x2d, W, l, Bd)
    kvs = [_kv_rows(k, v, gi)[:, None] for gi in range(len(A_GROUPS))]
    states = (ncb.reshape(Bd, nb, LRU_WIDTH), ob, vc[:, None], ncd.reshape(Bd, nb, SSD_CONV_CH), nssm)
    return x2d, kvs, states


def kernel(x_prompt, x_sample, cache_kv_a1, cache_kv_a2, cache_kv_a3, state_conv_b, state_h_b, state_conv_d, state_ssm_d, norm1_g, w_in, q_norm_g, k_norm_g, conv_b_w, conv_b_b, lru_wa, lru_ba, lru_wx, lru_bx, lru_lambda, sgu_ln_g, sgu_ln_b, sgu_w, sgu_b, ssd_conv_w, ssd_conv_b, ssd_dt_bias, ssd_a_log, ssd_d, ssd_norm_g, w_pa, w_pb, w_pc, w_pd, w_o, norm2_g, router_group_w, router_group_b, router_exp_w, router_exp_b, moe_w_gate, moe_w_up, moe_w_down):
    p = dict(norm1_g=norm1_g, w_in=w_in, q_norm_g=q_norm_g, k_norm_g=k_norm_g, conv_b_w=conv_b_w,
             conv_b_b=conv_b_b, lru_wa=lru_wa, lru_ba=lru_ba, lru_wx=lru_wx, lru_bx=lru_bx,
             lru_lambda=lru_lambda, sgu_ln_g=sgu_ln_g, sgu_ln_b=sgu_ln_b, sgu_w=sgu_w, sgu_b=sgu_b,
             ssd_conv_w=ssd_conv_w, ssd_conv_b=ssd_conv_b, ssd_dt_bias=ssd_dt_bias, ssd_a_log=ssd_a_log,
             ssd_d=ssd_d, ssd_norm_g=ssd_norm_g, w_pa=w_pa, w_pb=w_pb, w_pc=w_pc, w_pd=w_pd, w_o=w_o,
             norm2_g=norm2_g, router_group_w=router_group_w, router_group_b=router_group_b,
             router_exp_w=router_exp_w, router_exp_b=router_exp_b, moe_w_gate=moe_w_gate,
             moe_w_up=moe_w_up, moe_w_down=moe_w_down)
    B, S, _ = x_prompt.shape
    Bd = x_sample.shape[0]
    depth = w_in.shape[0]
    small = _small_params(p)
    Wp = dict(small, **_matrix_params(p, BF16))
    Wd = dict(small, **_matrix_params(p, F32))
    tabs_p = _rope_tables(jnp.arange(S))
    tabs_s = tuple(jnp.broadcast_to(t, (Bd, LANES)) for t in _rope_tables(PAST_LEN + jnp.arange(1)))
    caches = (cache_kv_a1, cache_kv_a2, cache_kv_a3)
    yp, ys = x_prompt, x_sample.reshape(Bd, D_MODEL)
    P = [[] for _ in range(7)]
    Sx = [[] for _ in range(8)]
    for l in range(depth):
        yp, kvs, st = _prompt_layer(yp, Wp, l, tabs_p)
        for dst, val in zip(P, list(kvs) + list(st)):
            dst.append(val)
        ys, kvs, st = _decode_layer(ys, Wd, l, tabs_s, caches, state_conv_b[l], state_h_b[l],
                                    state_conv_d[l], state_ssm_d)
        for dst, val in zip(Sx, list(kvs) + list(st)):
            dst.append(val)
    st = jnp.stack
    return (yp, ys.reshape(Bd, 1, D_MODEL)) + tuple(st(t) for t in P) + tuple(st(t) for t in Sx)
```
